```python
import math
import jax, jax.numpy as jnp
from jax import lax
import numpy as np

D_MODEL = 1024
BATCH = 8
SEQ = 2048
DEPTH = 1
DEC_BATCH = 128
DEC_SEQ = 4
PAST_LEN = 2048
PAGE_SIZE = 128

D_MIX = D_MODEL
H_ATT = 4
ATT_DQ = 64
ATT_KDIM = 2 * ATT_DQ
ATT_DV = D_MIX // 2 // H_ATT
H_ML = 4
ML_DH = D_MIX // 2 // H_ML
FFN_DIM = 2816
NUM_BUCKETS = 32
MAX_DISTANCE = 128
Q_BLOCK = 128
MLSTM_CHUNK = 64
EPS = 1e-6
D_IN = H_ATT * (2 * ATT_KDIM + ATT_DV) + 4 * H_ML * ML_DH + 2 * H_ML

kernel_name = 'hymba_diffattn_mlstm_macaron_step'


def rmsnorm(x, g):
    xf = x.astype(jnp.float32)
    y = xf * lax.rsqrt(jnp.mean(xf * xf, axis=-1, keepdims=True) + EPS)
    return (y * g.astype(jnp.float32)).astype(x.dtype)


def half_ffn(x, g_pre, g_post, w_gate, w_up, w_down):
    h = rmsnorm(x, g_pre)
    u = jax.nn.silu(h @ w_gate) * (h @ w_up)
    return x + 0.5 * rmsnorm(u @ w_down, g_post)


def t5_bucket(rel):
    n = jnp.maximum(rel, 0)
    max_exact = NUM_BUCKETS // 2
    large = max_exact + (jnp.log(jnp.maximum(n, max_exact).astype(jnp.float32) / max_exact)
                         / math.log(MAX_DISTANCE / max_exact) * (NUM_BUCKETS - max_exact)).astype(jnp.int32)
    large = jnp.minimum(large, NUM_BUCKETS - 1)
    return jnp.where(n < max_exact, n, large)


def diff_attention(q, k, v, q_pos, k_pos, rel_bias, lam):
    logits = jnp.einsum('bqhcd,bkhcd->bchqk', q, k, preferred_element_type=jnp.float32) * (ATT_DQ ** -0.5)
    bias = rel_bias[t5_bucket(q_pos[:, None] - k_pos[None, :])].astype(jnp.float32)
    bias = jnp.transpose(bias, (2, 0, 1))
    mask = k_pos[None, :] <= q_pos[:, None]
    logits = jnp.where(mask, logits + bias, -jnp.inf)
    p = jax.nn.softmax(logits, axis=-1)
    a = p[:, 0] - lam * p[:, 1]
    return jnp.einsum('bhqk,bkhd->bqhd', a.astype(v.dtype), v)


def mlstm_chunkwise(q, k, v, ig, fg, C0, n0, m0):
    B, S, H, DK = q.shape
    DV = v.shape[-1]
    L = math.gcd(S, MLSTM_CHUNK)
    nc = S // L
    f32 = jnp.float32

    def chunks(t):
        return t.astype(f32).reshape(B, nc, L, H, -1).transpose(1, 0, 3, 2, 4)

    qc = chunks(q)
    kc = chunks(k) * (DK ** -0.5)
    vc = chunks(v)
    ic = chunks(ig[..., None])[..., 0]
    lfc = jax.nn.log_sigmoid(chunks(fg[..., None])[..., 0])
    causal = jnp.tril(jnp.ones((L, L), dtype=bool))

    def step(carry, xs):
        C, n, m = carry
        qt, kt, vt, it, lf = xs
        b = jnp.cumsum(lf, axis=-1)
        log_d = jnp.where(causal, b[..., :, None] - b[..., None, :] + it[..., None, :], -jnp.inf)
        inter = b + m[..., None]
        m_t = jnp.maximum(inter, jnp.max(log_d, axis=-1))
        d = jnp.exp(log_d - m_t[..., None])
        w_inter = jnp.exp(inter - m_t)
        s = jnp.einsum('bhtd,bhsd->bhts', qt, kt) * d
        num = jnp.einsum('bhts,bhsd->bhtd', s, vt) + w_inter[..., None] * jnp.einsum('bhtd,bhde->bhte', qt, C)
        den = jnp.sum(s, axis=-1) + w_inter * jnp.einsum('bhtd,bhd->bht', qt, n)
        h = num / jnp.maximum(jnp.abs(den), jnp.exp(-m_t))[..., None]
        b_last = b[..., -1]
        log_w = b_last[..., None] - b + it
        m_new = jnp.maximum(b_last + m, jnp.max(log_w, axis=-1))
        ws = jnp.exp(log_w - m_new[..., None])
        fw = jnp.exp(b_last + m - m_new)
        C_new = fw[..., None, None] * C + jnp.einsum('bhs,bhsd,bhse->bhde', ws, kt, vt)
        n_new = fw[..., None] * n + jnp.einsum('bhs,bhsd->bhd', ws, kt)
        return (C_new, n_new, m_new), h

    (C, n, m), h = lax.scan(step, (C0.astype(f32), n0.astype(f32), m0.astype(f32)), (qc, kc, vc, ic, lfc))
    h = h.transpose(1, 0, 3, 2, 4).reshape(B, S, H, DV)
    return h, C, n, m


def token_mix(h, past_k, past_v, C0, n0, m0, rel_bias, lam_init,
              w_in, b_gates, lam_q, lam_k, attn_norm, mlstm_norm, w_out):
    B, S, _ = h.shape
    proj = h @ w_in
    sizes = [H_ATT * ATT_KDIM, H_ATT * ATT_KDIM, H_ATT * ATT_DV,
             H_ML * ML_DH, H_ML * ML_DH, H_ML * ML_DH, H_ML * ML_DH]
    idx = np.cumsum(sizes).tolist()
    q_a, k_a, v_a, q_m, k_m, v_m, o_m, gates = jnp.split(proj, idx, axis=-1)

    q_a = q_a.reshape(B, S, H_ATT, 2, ATT_DQ)
    k_a = k_a.reshape(B, S, H_ATT, 2, ATT_DQ)
    v_a = v_a.reshape(B, S, H_ATT, ATT_DV)
    k_rows = k_a.reshape(B, S, H_ATT, ATT_KDIM)
    if past_k is None:
        P = 0
        k_all, v_all = k_a, v_a
    else:
        P = past_k.shape[1]
        k_all = jnp.concatenate([past_k.reshape(B, P, H_ATT, 2, ATT_DQ).astype(k_a.dtype), k_a], axis=1)
        v_all = jnp.concatenate([past_v.astype(v_a.dtype), v_a], axis=1)
    k_pos = jnp.arange(P + S)
    q_pos = P + jnp.arange(S)
    lam = jnp.exp(jnp.sum(lam_q[0] * lam_k[0])) - jnp.exp(jnp.sum(lam_q[1] * lam_k[1])) + lam_init
    if S > Q_BLOCK and S % Q_BLOCK == 0:
        nb = S // Q_BLOCK
        qb = jnp.swapaxes(q_a.reshape(B, nb, Q_BLOCK, H_ATT, 2, ATT_DQ), 0, 1)
        pb = q_pos.reshape(nb, Q_BLOCK)
        ob = lax.map(lambda a: diff_attention(a[0], k_all, v_all, a[1], k_pos, rel_bias, lam), (qb, pb))
        attn = jnp.swapaxes(ob, 0, 1).reshape(B, S, H_ATT, ATT_DV)
    else:
        attn = diff_attention(q_a, k_all, v_all, q_pos, k_pos, rel_bias, lam)
    attn = (rmsnorm(attn, attn_norm) * (1.0 - lam_init)).reshape(B, S, H_ATT * ATT_DV)

    gates = gates.reshape(B, S, 2, H_ML) + b_gates
    h_m, C, n, m = mlstm_chunkwise(q_m.reshape(B, S, H_ML, ML_DH), k_m.reshape(B, S, H_ML, ML_DH),
                                   v_m.reshape(B, S, H_ML, ML_DH), gates[:, :, 0], gates[:, :, 1], C0, n0, m0)
    h_m = rmsnorm(h_m, mlstm_norm).astype(h.dtype).reshape(B, S, H_ML * ML_DH) * jax.nn.sigmoid(o_m)

    mixed = jnp.concatenate([attn.astype(h.dtype), h_m], axis=-1)
    return mixed @ w_out, k_rows, v_a, C, n, m


def setup_inputs(seed: int = 0) -> dict:
    key = jax.random.key(seed)
    ks = jax.random.split(key, 24)
    n_pages = PAST_LEN // PAGE_SIZE
    n_used = DEC_BATCH * n_pages
    n_phys = n_used + n_used // 4

    def nrm(k, shape, s):
        return jax.random.normal(k, shape, jnp.float32) * s

    x_prompt = nrm(ks[0], (BATCH, SEQ, D_MODEL), 1.0)
    x_sample = nrm(ks[1], (DEC_BATCH, DEC_SEQ, D_MODEL), 1.0)
    cache_k = nrm(ks[2], (DEPTH, n_phys, PAGE_SIZE, H_ATT, ATT_KDIM), 1.0)
    cache_v = nrm(ks[3], (DEPTH, n_phys, PAGE_SIZE, H_ATT, ATT_DV), 1.0)
    state_C = nrm(ks[4], (DEPTH, DEC_BATCH, H_ML, ML_DH, ML_DH), 0.1)
    state_n = nrm(ks[5], (DEPTH, DEC_BATCH, H_ML, ML_DH), 0.1)
    state_m = nrm(ks[6], (DEPTH, DEC_BATCH, H_ML), 1.0)
    page_table = jax.random.permutation(ks[7], n_phys)[:n_used].reshape(DEC_BATCH, n_pages).astype(jnp.int32)
    rel_bias = nrm(ks[8], (NUM_BUCKETS, H_ATT), 0.5)
    norm_gains = 1.0 + nrm(ks[9], (DEPTH, 6, D_MODEL), 0.05)
    ffn_w_gate = nrm(ks[10], (DEPTH, 2, D_MODEL, FFN_DIM), D_MODEL ** -0.5)
    ffn_w_up = nrm(ks[11], (DEPTH, 2, D_MODEL, FFN_DIM), D_MODEL ** -0.5)
    ffn_w_down = nrm(ks[12], (DEPTH, 2, FFN_DIM, D_MODEL), FFN_DIM ** -0.5)
    w_in = nrm(ks[13], (DEPTH, D_MODEL, D_IN), D_MODEL ** -0.5)
    b_i = nrm(ks[14], (DEPTH, 1, H_ML), 0.1)
    b_f = jnp.linspace(3.0, 6.0, H_ML, dtype=jnp.float32)[None, None, :] + nrm(ks[15], (DEPTH, 1, H_ML), 0.1)
    b_gates = jnp.concatenate([b_i, b_f], axis=1)
    lam_q = nrm(ks[16], (DEPTH, 2, ATT_DQ), 0.1)
    lam_k = nrm(ks[17], (DEPTH, 2, ATT_DQ), 0.1)
    attn_norm = 1.0 + nrm(ks[18], (DEPTH, ATT_DV), 0.05)
    mlstm_norm = 1.0 + nrm(ks[19], (DEPTH, ML_DH), 0.05)
    w_out = nrm(ks[20], (DEPTH, D_MIX, D_MODEL), D_MIX ** -0.5)
    return {'x_prompt': x_prompt, 'x_sample': x_sample, 'cache_k': cache_k, 'cache_v': cache_v,
            'state_C': state_C, 'state_n': state_n, 'state_m': state_m, 'page_table': page_table,
            'rel_bias': rel_bias, 'norm_gains': norm_gains, 'ffn_w_gate': ffn_w_gate, 'ffn_w_up': ffn_w_up,
            'ffn_w_down': ffn_w_down, 'w_in': w_in, 'b_gates': b_gates, 'lam_q': lam_q, 'lam_k': lam_k,
            'attn_norm': attn_norm, 'mlstm_norm': mlstm_norm, 'w_out': w_out}


def reference(x_prompt, x_sample, cache_k, cache_v, state_C, state_n, state_m, page_table,
              rel_bias, norm_gains, ffn_w_gate, ffn_w_up, ffn_w_down, w_in, b_gates, lam_q, lam_k,
              attn_norm, mlstm_norm, w_out):
    xp, xs = x_prompt, x_sample
    B, DB = xp.shape[0], xs.shape[0]
    kp_l, vp_l, Cp_l, np_l, mp_l = [], [], [], [], []
    ks_l, vs_l, Cs_l, ns_l, ms_l = [], [], [], [], []
    for l in range(DEPTH):
        g = norm_gains[l]
        lam_init = 0.8 - 0.6 * math.exp(-0.3 * l)
        mix_w = (w_in[l], b_gates[l], lam_q[l], lam_k[l], attn_norm[l], mlstm_norm[l], w_out[l])
        xp = half_ffn(xp, g[0], g[1], ffn_w_gate[l, 0], ffn_w_up[l, 0], ffn_w_down[l, 0])
        xs = half_ffn(xs, g[0], g[1], ffn_w_gate[l, 0], ffn_w_up[l, 0], ffn_w_down[l, 0])
        zC = jnp.zeros((B, H_ML, ML_DH, ML_DH), jnp.float32)
        zn = jnp.zeros((B, H_ML, ML_DH), jnp.float32)
        zm = jnp.zeros((B, H_ML), jnp.float32)
        yp, kp, vp, Cp, npr, mp = token_mix(rmsnorm(xp, g[2]), None, None, zC, zn, zm, rel_bias, lam_init, *mix_w)
        past_k = cache_k[l][page_table].reshape(DB, -1, H_ATT, ATT_KDIM)
        past_v = cache_v[l][page_table].reshape(DB, -1, H_ATT, ATT_DV)
        ys, ksn, vsn, Cs, nsn, msn = token_mix(rmsnorm(xs, g[2]), past_k, past_v, state_C[l], state_n[l], state_m[l],
                                               rel_bias, lam_init, *mix_w)
        xp = xp + rmsnorm(yp, g[3])
        xs = xs + rmsnorm(ys, g[3])
        xp = half_ffn(xp, g[4], g[5], ffn_w_gate[l, 1], ffn_w_up[l, 1], ffn_w_down[l, 1])
        xs = half_ffn(xs, g[4], g[5], ffn_w_gate[l, 1], ffn_w_up[l, 1], ffn_w_down[l, 1])
        kp_l.append(kp); vp_l.append(vp); Cp_l.append(Cp); np_l.append(npr); mp_l.append(mp)
        ks_l.append(ksn); vs_l.append(vsn); Cs_l.append(Cs); ns_l.append(nsn); ms_l.append(msn)
    return (xp, xs,
            jnp.stack(kp_l), jnp.stack(vp_l), jnp.stack(Cp_l), jnp.stack(np_l), jnp.stack(mp_l),
            jnp.stack(ks_l), jnp.stack(vs_l), jnp.stack(Cs_l), jnp.stack(ns_l), jnp.stack(ms_l))
```

```python
import functools
import math

import jax
import jax.numpy as jnp
from jax import lax
from jax.experimental import pallas as pl
from jax.experimental.pallas import tpu as pltpu

F32 = jnp.float32
BF16 = jnp.bfloat16

H_ATT = 4
ATT_DQ = 64
H_ML = 4
NUM_BUCKETS = 32
MAX_DISTANCE = 128
MLSTM_CHUNK = 64
EPS = 1e-6

LANES = 128
SUBLANES = 8
VMEM_LIMIT_BYTES = 56 * 1024 * 1024

TOKEN_TILE = 512
FFN_CHUNK = 256
ATT_TILE = 256
SAMPLE_ROWS = 8


def _params(*sem):
    return pltpu.CompilerParams(dimension_semantics=sem, vmem_limit_bytes=VMEM_LIMIT_BYTES)


def _rms(x, g):
    return x * lax.rsqrt(jnp.mean(x * x, axis=-1, keepdims=True) + EPS) * g


def _dot(a, b):
    return jnp.dot(a, b, preferred_element_type=F32)


def _dot_nt(a, b):
    return lax.dot_general(a, b, (((1,), (1,)), ((), ())), preferred_element_type=F32)


def _dot_tn(a, b):
    return lax.dot_general(a, b, (((0,), (0,)), ((), ())), preferred_element_type=F32)


def _resident(shape):
    return pl.BlockSpec(shape, lambda *_: (0,) * len(shape), pipeline_mode=pl.Buffered(1))


def _ffn_kernel(*refs, with_mix, n_chunks, gain_rows):
    mix_row, pre_row, post_row = gain_rows
    if with_mix:
        (x_ref, ma_ref, mm_ref, wo_ref, g_ref, wgu_ref, wd_ref, o_ref, u_ref) = refs
        half = ma_ref.shape[-1]
        y = _dot(ma_ref[...].astype(BF16), wo_ref[:half, :]) + _dot(mm_ref[...].astype(BF16), wo_ref[half:, :])
        x = x_ref[...] + _rms(y, g_ref[mix_row:mix_row + 1, :])
    else:
        (x_ref, g_ref, wgu_ref, wd_ref, o_ref, u_ref) = refs
        x = x_ref[...]
    h = _rms(x, g_ref[pre_row:pre_row + 1, :]).astype(BF16)
    for c in range(n_chunks):
        gu = _dot(h, wgu_ref[:, 2 * c * FFN_CHUNK:2 * (c + 1) * FFN_CHUNK])
        gate = gu[:, :FFN_CHUNK]
        up = gu[:, FFN_CHUNK:]
        u_ref[:, c * FFN_CHUNK:(c + 1) * FFN_CHUNK] = (gate * jax.nn.sigmoid(gate) * up).astype(BF16)
    y = _dot(u_ref[...], wd_ref[...])
    o_ref[...] = x + 0.5 * _rms(y, g_ref[post_row:post_row + 1, :])


def _ffn(x, gains, gain_rows, wgu, wd, mix=None):
    n, d = x.shape
    f = wd.shape[0]
    n_chunks = f // FFN_CHUNK
    tm = TOKEN_TILE
    row = lambda w: pl.BlockSpec((tm, w), lambda i: (i, 0))
    in_specs = [row(d)]
    args = [x]
    if mix is not None:
        ma, mm, wo = mix
        in_specs += [row(ma.shape[1]), row(mm.shape[1]), _resident(wo.shape)]
        args += [ma, mm, wo]
    in_specs += [_resident(gains.shape), _resident(wgu.shape), _resident(wd.shape)]
    args += [gains, wgu, wd]
    return pl.pallas_call(
        functools.partial(_ffn_kernel, with_mix=mix is not None, n_chunks=n_chunks, gain_rows=gain_rows),
        grid=(n // tm,),
        in_specs=in_specs,
        out_specs=row(d),
        out_shape=jax.ShapeDtypeStruct((n, d), F32),
        scratch_shapes=[pltpu.VMEM((tm, f), BF16)],
        compiler_params=_params("parallel"),
        name="ffn_mix" if mix is not None else "ffn",
    )(*args)


def _proj_kernel(x_ref, g_ref, w_ref, *outs, head_major, width):
    (qa_ref, kaf_ref, vaf_ref, ka_ref, va_ref, qm_ref, km_ref, vm_ref, om_ref, gt_ref, gtt_ref) = outs
    xn = _rms(x_ref[...], g_ref[...]).astype(BF16)

    def group(i):
        return _dot(xn, w_ref[:, i * width:(i + 1) * width])

    def put(ref, val):
        if head_major:
            for h in range(width // LANES):
                ref[h] = val[:, h * LANES:(h + 1) * LANES].astype(ref.dtype)
        else:
            ref[...] = val.astype(ref.dtype)

    put(qa_ref, group(0) * (ATT_DQ ** -0.5))
    ka = group(1)
    kaf_ref[...] = ka
    put(ka_ref, ka)
    va = group(2)
    vaf_ref[...] = va
    put(va_ref, va)
    put(qm_ref, group(3))
    put(km_ref, group(4) * (LANES ** -0.5))
    put(vm_ref, group(5))
    om_ref[...] = group(6)
    gates = _dot(xn, w_ref[:, 7 * width:7 * width + LANES])
    gt_ref[...] = gates
    gtt_ref[...] = gates.T[:2 * H_ML, :]


def _proj(x, gain, w_in_p, head_major):
    n, d = x.shape
    width = H_ATT * LANES
    tm = TOKEN_TILE
    row = lambda w: pl.BlockSpec((tm, w), lambda i: (i, 0))
    if head_major:
        hm_spec = pl.BlockSpec((width // LANES, tm, LANES), lambda i: (0, i, 0))
        hm_shape = jax.ShapeDtypeStruct((width // LANES, n, LANES), BF16)
    else:
        hm_spec = row(width)
        hm_shape = jax.ShapeDtypeStruct((n, width), F32)
    f32_shape = jax.ShapeDtypeStruct((n, width), F32)
    out_specs = [hm_spec, row(width), row(width), hm_spec, hm_spec, hm_spec, hm_spec, hm_spec, row(width),
                 row(LANES), pl.BlockSpec((2 * H_ML, tm), lambda i: (0, i))]
    out_shape = [hm_shape, f32_shape, f32_shape, hm_shape, hm_shape, hm_shape, hm_shape, hm_shape, f32_shape,
                 jax.ShapeDtypeStruct((n, LANES), F32), jax.ShapeDtypeStruct((2 * H_ML, n), F32)]
    return pl.pallas_call(
        functools.partial(_proj_kernel, head_major=head_major, width=width),
        grid=(n // tm,),
        in_specs=[row(d), _resident(gain.shape), _resident(w_in_p.shape)],
        out_specs=out_specs,
        out_shape=out_shape,
        compiler_params=_params("parallel"),
        name="proj_prompt" if head_major else "proj_sample",
    )(x, gain, w_in_p)


def _t5_bias(rel, rb_ref, h):
    n = jnp.maximum(rel, 0)
    max_exact = NUM_BUCKETS // 2
    large = max_exact + (jnp.log(jnp.maximum(n, max_exact).astype(F32) / max_exact)
                         / math.log(MAX_DISTANCE / max_exact) * (NUM_BUCKETS - max_exact)).astype(jnp.int32)
    large = jnp.minimum(large, NUM_BUCKETS - 1)
    bucket = jnp.where(n < max_exact, n, large)
    out = jnp.zeros(rel.shape, F32)
    for i in range(NUM_BUCKETS):
        out = jnp.where(bucket == i, rb_ref[i, h], out)
    return out - rb_ref[NUM_BUCKETS - 1, h]


def _prompt_bias_kernel(rb_ref, o_ref, *, tile):
    r = lax.broadcasted_iota(jnp.int32, (tile, tile), 0)
    c = lax.broadcasted_iota(jnp.int32, (tile, tile), 1)
    for h in range(H_ATT):
        for t in range(2):
            o_ref[h, t] = _t5_bias(t * tile + r - c, rb_ref, h)


def _prompt_bias(rel_bias, tile):
    return pl.pallas_call(
        functools.partial(_prompt_bias_kernel, tile=tile),
        in_specs=[pl.BlockSpec(memory_space=pltpu.SMEM)],
        out_shape=jax.ShapeDtypeStruct((H_ATT, 2, tile, tile), F32),
        name="prompt_bias",
    )(rel_bias)


def _sample_bias_kernel(rb_ref, o_ref, *, page, valid):
    qi = lax.broadcasted_iota(jnp.int32, (SUBLANES, 2 * LANES), 0)
    col = lax.broadcasted_iota(jnp.int32, (SUBLANES, 2 * LANES), 1)
    kj = col - page
    for h in range(H_ATT):
        past = _t5_bias(page + qi - col, rb_ref, h)
        new = jnp.where((kj <= qi) & (kj < valid), _t5_bias(qi - kj, rb_ref, h), -jnp.inf)
        tile = jnp.where(col < page, past, new)
        for comp in range(2):
            o_ref[(comp * H_ATT + h) * SUBLANES:(comp * H_ATT + h + 1) * SUBLANES, :] = tile


def _sample_bias(rel_bias, page, valid):
    return pl.pallas_call(
        functools.partial(_sample_bias_kernel, page=page, valid=valid),
        in_specs=[pl.BlockSpec(memory_space=pltpu.SMEM)],
        out_shape=jax.ShapeDtypeStruct((2 * H_ATT * SUBLANES, 2 * LANES), F32),
        name="sample_bias",
    )(rel_bias)


def _lambda(lq_ref, lk_ref, lam_init):
    prod = lq_ref[...] * lk_ref[...]
    return (jnp.exp(jnp.sum(prod[0:1, :], axis=-1, keepdims=True))
            - jnp.exp(jnp.sum(prod[1:2, :], axis=-1, keepdims=True)) + lam_init)


def _attn_kernel(q_ref, k_ref, v_ref, bias_ref, lq_ref, lk_ref, gn_ref, o_ref, m_scr, l_scr, acc_scr,
                 *, tile, lam_init):
    i = pl.program_id(2)
    q = q_ref[0]
    lane = lax.broadcasted_iota(jnp.int32, q.shape, 1)
    zero = jnp.zeros_like(q)
    qc = (jnp.where(lane < ATT_DQ, q, zero), jnp.where(lane >= ATT_DQ, q, zero))
    m_scr[...] = jnp.full(m_scr.shape, -jnp.inf, F32)
    l_scr[...] = jnp.zeros(l_scr.shape, F32)
    acc_scr[...] = jnp.zeros(acc_scr.shape, F32)

    def update(j, bias, causal):
        rows = pl.ds(pl.multiple_of(j * tile, tile), tile)
        kt = k_ref[0, rows, :]
        vt = v_ref[0, rows, :]
        for c in range(2):
            s = _dot_nt(qc[c], kt)
            if bias is not None:
                s = s + bias
            if causal:
                r = lax.broadcasted_iota(jnp.int32, s.shape, 0)
                col = lax.broadcasted_iota(jnp.int32, s.shape, 1)
                s = jnp.where(col <= r, s, -jnp.inf)
            m_prev = m_scr[c]
            m_new = jnp.maximum(m_prev, jnp.max(s, axis=-1, keepdims=True))
            alpha = jnp.exp(m_prev - m_new)
            p = jnp.exp(s - m_new)
            l_scr[c] = alpha * l_scr[c] + jnp.sum(p, axis=-1, keepdims=True)
            acc_scr[c] = alpha * acc_scr[c] + _dot(p.astype(BF16), vt)
            m_scr[c] = m_new

    def far(j, carry):
        update(j, None, False)
        return carry

    lax.fori_loop(0, jnp.maximum(i - 1, 0), far, 0)

    @pl.when(i >= 1)
    def _():
        update(i - 1, bias_ref[0, 1], False)

    update(i, bias_ref[0, 0], True)

    lam = _lambda(lq_ref, lk_ref, lam_init)
    out = acc_scr[0] / l_scr[0] - lam * (acc_scr[1] / l_scr[1])
    o_ref[...] = (_rms(out, gn_ref[...]) * (1.0 - lam_init)).astype(o_ref.dtype)


def _prompt_attention(q, k, v, bias, lam_q, lam_k, gain, batch, lam_init):
    h, n, dv = q.shape
    seq = n // batch
    tile = ATT_TILE
    nq = seq // tile
    return pl.pallas_call(
        functools.partial(_attn_kernel, tile=tile, lam_init=lam_init),
        grid=(batch, h, nq),
        in_specs=[pl.BlockSpec((1, tile, dv), lambda b, hh, i: (hh, b * nq + i, 0)),
                  pl.BlockSpec((1, seq, dv), lambda b, hh, i: (hh, b, 0)),
                  pl.BlockSpec((1, seq, dv), lambda b, hh, i: (hh, b, 0)),
                  pl.BlockSpec((1, 2, tile, tile), lambda b, hh, i: (hh, 0, 0, 0)),
                  pl.BlockSpec(lam_q.shape, lambda b, hh, i: (0, 0)),
                  pl.BlockSpec(lam_k.shape, lambda b, hh, i: (0, 0)),
                  pl.BlockSpec(gain.shape, lambda b, hh, i: (0, 0))],
        out_specs=pl.BlockSpec((tile, dv), lambda b, hh, i: (b * nq + i, hh)),
        out_shape=jax.ShapeDtypeStruct((n, h * dv), BF16),
        scratch_shapes=[pltpu.VMEM((2, tile, 1), F32), pltpu.VMEM((2, tile, 1), F32),
                        pltpu.VMEM((2, tile, dv), F32)],
        compiler_params=_params("parallel", "parallel", "arbitrary"),
        name="attn_prompt",
    )(q, k, v, bias, lam_q, lam_k, gain)


def _sample_attn_kernel(pt_ref, q_ref, kn_ref, vn_ref, bias_ref, lq_ref, lk_ref, gn_ref, *rest,
                        n_pages, page, lam_init):
    del pt_ref
    kp = rest[:n_pages]
    vp = rest[n_pages:2 * n_pages]
    o_ref, s_scr = rest[2 * n_pages:]
    q = q_ref[0]
    width = q.shape[-1]
    n_rows = 2 * H_ATT * SUBLANES
    qt = jnp.concatenate([q] * (2 * H_ATT), axis=0)
    r = lax.broadcasted_iota(jnp.int32, (n_rows, width), 0)
    c = lax.broadcasted_iota(jnp.int32, (n_rows, width), 1)
    comp = r // (H_ATT * SUBLANES)
    head = (r // SUBLANES) % H_ATT
    own = (c // LANES == head) & ((c // ATT_DQ) % 2 == comp)
    qbd = jnp.where(own, qt, 0.0).astype(BF16)

    for p in range(n_pages):
        s = _dot_nt(qbd, kp[p][0].astype(BF16))
        if p == n_pages - 1:
            s = s + bias_ref[:, :page]
        s_scr[:, p * page:(p + 1) * page] = s
    pad = jnp.zeros((2 * SUBLANES - kn_ref.shape[1], width), F32)
    k_new = jnp.concatenate([kn_ref[0], pad], axis=0).astype(BF16)
    v_new = jnp.concatenate([vn_ref[0], pad], axis=0).astype(BF16)
    n_new = k_new.shape[0]
    s_new = _dot_nt(qbd, k_new) + bias_ref[:, page:page + n_new]

    s_all = s_scr[...]
    m = jnp.maximum(jnp.max(s_all, axis=-1, keepdims=True), jnp.max(s_new, axis=-1, keepdims=True))
    p_all = jnp.exp(s_all - m)
    p_new = jnp.exp(s_new - m)
    inv = 1.0 / (jnp.sum(p_all, axis=-1, keepdims=True) + jnp.sum(p_new, axis=-1, keepdims=True))
    lam = _lambda(lq_ref, lk_ref, lam_init)
    half = n_rows // 2
    a_all = (p_all[:half] * inv[:half] - lam * (p_all[half:] * inv[half:])).astype(BF16)
    a_new = (p_new[:half] * inv[:half] - lam * (p_new[half:] * inv[half:])).astype(BF16)
    out = _dot(a_new, v_new)
    for p in range(n_pages):
        out = out + _dot(a_all[:, p * page:(p + 1) * page], vp[p][0].astype(BF16))
    for h in range(H_ATT):
        blk = out[h * SUBLANES:(h + 1) * SUBLANES, h * LANES:(h + 1) * LANES]
        o_ref[0, :, h * LANES:(h + 1) * LANES] = _rms(blk, gn_ref[...]) * (1.0 - lam_init)


def _sample_attention(page_table, q, k_new, v_new, cache_k, cache_v, bias, lam_q, lam_k, gain, lam_init):
    nb, rows, width = q.shape
    n_pages = page_table.shape[1]
    page = cache_k.shape[1]

    def page_spec(p):
        return pl.BlockSpec((1, page, width), lambda b, pt: (pt[b, p], 0, 0))

    seq_spec = pl.BlockSpec((1, rows, width), lambda b, pt: (b, 0, 0))
    full = lambda a: pl.BlockSpec(a.shape, lambda b, pt: (0,) * a.ndim)
    grid_spec = pltpu.PrefetchScalarGridSpec(
        num_scalar_prefetch=1,
        grid=(nb,),
        in_specs=[seq_spec, seq_spec, seq_spec, full(bias), full(lam_q), full(lam_k), full(gain)]
                 + [page_spec(p) for p in range(n_pages)] * 2,
        out_specs=seq_spec,
        scratch_shapes=[pltpu.VMEM((bias.shape[0], n_pages * page), F32)],
    )
    return pl.pallas_call(
        functools.partial(_sample_attn_kernel, n_pages=n_pages, page=page, lam_init=lam_init),
        grid_spec=grid_spec,
        out_shape=jax.ShapeDtypeStruct((nb, rows, width), F32),
        compiler_params=_params("parallel"),
        name="attn_sample",
    )(page_table, q, k_new, v_new, bias, lam_q, lam_k, gain, *([cache_k] * n_pages), *([cache_v] * n_pages))


def _log_sigmoid(x):
    return jnp.minimum(x, 0.0) - jnp.log1p(jnp.exp(-jnp.abs(x)))


def _mlstm_kernel(*refs, chunk, n_chunks, valid, head_major, has_state):
    refs = list(refs)
    q_ref, k_ref, v_ref, g_ref, gt_ref, om_ref, brow_ref, bcol_ref, gn_ref = refs[:9]
    refs = refs[9:]
    if has_state:
        c0_ref, n0_ref, m0_ref = refs[:3]
        refs = refs[3:]
    hm_ref, c_ref, n_ref, m_ref = refs
    L = chunk

    if has_state:
        c_ref[...] = c0_ref[...]
        n_ref[...] = n0_ref[...]
        m_ref[...] = m0_ref[...]
    else:
        c_ref[...] = jnp.zeros(c_ref.shape, F32)
        n_ref[...] = jnp.zeros(n_ref.shape, F32)
        m_ref[...] = jnp.zeros(m_ref.shape, F32)

    t_i = lax.broadcasted_iota(jnp.int32, (L, L), 0)
    s_i = lax.broadcasted_iota(jnp.int32, (L, L), 1)
    causal = s_i <= t_i
    lower = causal.astype(F32)
    upper = (t_i <= s_i).astype(F32)

    def load(ref, rows, h):
        if head_major:
            return ref[h, rows, :]
        return ref[0, :, h * LANES:(h + 1) * LANES].astype(BF16)

    def step(c, carry):
        rows = pl.ds(pl.multiple_of(c * L, L), L)
        g_col = g_ref[rows, :] + brow_ref[...]
        g_row = gt_ref[0, c] + bcol_ref[...]
        lf_col = _log_sigmoid(g_col)
        lf_row = _log_sigmoid(g_row)
        if valid < L:
            tok_c = lax.broadcasted_iota(jnp.int32, g_col.shape, 0)
            tok_r = lax.broadcasted_iota(jnp.int32, g_row.shape, 1)
            g_col = jnp.where(tok_c < valid, g_col, -1e30)
            g_row = jnp.where(tok_r < valid, g_row, -1e30)
            lf_col = jnp.where(tok_c < valid, lf_col, 0.0)
            lf_row = jnp.where(tok_r < valid, lf_row, 0.0)
        b_col = jnp.dot(lower, lf_col, preferred_element_type=F32, precision=lax.Precision.HIGHEST)
        b_row = jnp.dot(lf_row, upper, preferred_element_type=F32, precision=lax.Precision.HIGHEST)
        for h in range(H_ML):
            q = load(q_ref, rows, h)
            k = load(k_ref, rows, h)
            v = load(v_ref, rows, h)
            cm = c_ref[0, h]
            nr = n_ref[0, h:h + 1, :]
            m = m_ref[0, h:h + 1, 0:1]
            bc = b_col[:, H_ML + h:H_ML + h + 1]
            ic = g_col[:, h:h + 1]
            br = b_row[H_ML + h:H_ML + h + 1, :]
            ir = g_row[h:h + 1, :]
            log_d = jnp.where(causal, bc - br + ir, -jnp.inf)
            inter = bc + m
            m_t = jnp.maximum(inter, jnp.max(log_d, axis=-1, keepdims=True))
            d = jnp.exp(log_d - m_t)
            w_inter = jnp.exp(inter - m_t)
            s = _dot_nt(q, k) * d
            num = _dot(s.astype(BF16), v) + w_inter * _dot(q, cm.astype(BF16))
            qn = jnp.sum(q.astype(F32) * nr, axis=-1, keepdims=True)
            den = jnp.sum(s, axis=-1, keepdims=True) + w_inter * qn
            hv = num / jnp.maximum(jnp.abs(den), jnp.exp(-m_t))
            b_last = bc[L - 1:L, :]
            log_w = b_last - bc + ic
            m_new = jnp.maximum(b_last + m, jnp.max(log_w, axis=0, keepdims=True))
            ws = jnp.exp(log_w - m_new)
            fw = jnp.exp(b_last + m - m_new)
            wv = (ws * v.astype(F32)).astype(BF16)
            c_ref[0, h] = fw * cm + _dot_tn(k, wv)
            n_ref[0, h:h + 1, :] = fw * nr + jnp.sum(ws * k.astype(F32), axis=0, keepdims=True)
            m_ref[0, h:h + 1, :] = jnp.broadcast_to(m_new, (1, LANES))
            gate = jax.nn.sigmoid(om_ref[rows, h * LANES:(h + 1) * LANES])
            hm_ref[rows, h * LANES:(h + 1) * LANES] = (_rms(hv, gn_ref[...]) * gate).astype(hm_ref.dtype)
        return carry

    lax.fori_loop(0, n_chunks, step, 0)


def _mlstm(q, k, v, gates, gates_t, om, b_row, b_col, gain, state, *, batch, chunk, valid, head_major):
    n, width = om.shape
    seq = n // batch
    n_chunks = seq // chunk
    dh = LANES
    if head_major:
        qkv_spec = pl.BlockSpec((H_ML, seq, dh), lambda b: (0, b, 0))
    else:
        qkv_spec = pl.BlockSpec((1, seq, width), lambda b: (b, 0, 0))
    row = lambda w: pl.BlockSpec((seq, w), lambda b: (b, 0))
    full = lambda a: pl.BlockSpec(a.shape, lambda b: (0,) * a.ndim)
    c_spec = pl.BlockSpec((1, H_ML, dh, dh), lambda b: (b, 0, 0, 0))
    v_spec = pl.BlockSpec((1, H_ML, dh), lambda b: (b, 0, 0))
    in_specs = [qkv_spec, qkv_spec, qkv_spec, row(LANES),
                pl.BlockSpec((1, n_chunks, 2 * H_ML, chunk), lambda b: (b, 0, 0, 0)),
                row(width), full(b_row), full(b_col), full(gain)]
    args = [q, k, v, gates, gates_t, om, b_row, b_col, gain]
    if state is not None:
        in_specs += [c_spec, v_spec, v_spec]
        args += list(state)
    return pl.pallas_call(
        functools.partial(_mlstm_kernel, chunk=chunk, n_chunks=n_chunks, valid=valid,
                          head_major=head_major, has_state=state is not None),
        grid=(batch,),
        in_specs=in_specs,
        out_specs=[row(width), c_spec, v_spec, v_spec],
        out_shape=[jax.ShapeDtypeStruct((n, width), BF16 if head_major else F32),
                   jax.ShapeDtypeStruct((batch, H_ML, dh, dh), F32),
                   jax.ShapeDtypeStruct((batch, H_ML, dh), F32),
                   jax.ShapeDtypeStruct((batch, H_ML, dh), F32)],
        compiler_params=_params("parallel"),
        name="mlstm_prompt" if head_major else "mlstm_sample",
    )(*args)


def _fuse_gate_up(w_gate, w_up):
    d, f = w_gate.shape
    nch = f // FFN_CHUNK
    gu = jnp.concatenate([w_gate.reshape(d, nch, FFN_CHUNK), w_up.reshape(d, nch, FFN_CHUNK)], axis=2)
    return gu.reshape(d, 2 * f).astype(BF16)


def kernel(x_prompt, x_sample, cache_k, cache_v, state_C, state_n, state_m, page_table, rel_bias, norm_gains,
           ffn_w_gate, ffn_w_up, ffn_w_down, w_in, b_gates, lam_q, lam_k, attn_norm, mlstm_norm, w_out):
    batch, seq, d = x_prompt.shape
    dec_batch, dec_seq, _ = x_sample.shape
    depth = w_in.shape[0]
    page = cache_k.shape[2]
    width = H_ATT * LANES
    d_in = w_in.shape[-1]
    n_gate = 2 * H_ML
    assert d_in == 7 * width + n_gate and dec_seq <= SAMPLE_ROWS
    assert seq % ATT_TILE == 0 and seq % MLSTM_CHUNK == 0 and (batch * seq) % TOKEN_TILE == 0

    xp = x_prompt.reshape(batch * seq, d)
    xs = jnp.pad(x_sample, ((0, 0), (0, SAMPLE_ROWS - dec_seq), (0, 0))).reshape(dec_batch * SAMPLE_ROWS, d)

    prompt_bias = _prompt_bias(rel_bias, ATT_TILE)
    sample_bias = _sample_bias(rel_bias, page, dec_seq)

    outs = [[] for _ in range(10)]
    for l in range(depth):
        lam_init = 0.8 - 0.6 * math.exp(-0.3 * l)
        g = norm_gains[l]
        wgu = [_fuse_gate_up(ffn_w_gate[l, i], ffn_w_up[l, i]) for i in range(2)]
        wd = [ffn_w_down[l, i].astype(BF16) for i in range(2)]
        w_in_p = jnp.pad(w_in[l], ((0, 0), (0, LANES - n_gate))).astype(BF16)
        wo = w_out[l].astype(BF16)
        b_row = jnp.pad(b_gates[l].reshape(1, n_gate), ((0, 0), (0, LANES - n_gate)))
        b_col = b_gates[l].reshape(n_gate, 1)
        gn_att = attn_norm[l].reshape(1, -1)
        gn_ml = mlstm_norm[l].reshape(1, -1)
        ck = cache_k[l].reshape(cache_k.shape[1], page, width)
        cv = cache_v[l].reshape(cache_v.shape[1], page, width)

        xp = _ffn(xp, g, (None, 0, 1), wgu[0], wd[0])
        xs = _ffn(xs, g, (None, 0, 1), wgu[0], wd[0])

        (qa, kaf, vaf, ka, va, qm, km, vm, om, gt, gtt) = _proj(xp, g[2:3], w_in_p, True)
        mix_a = _prompt_attention(qa, ka, va, prompt_bias, lam_q[l], lam_k[l], gn_att, batch, lam_init)
        gtt = gtt.reshape(n_gate, batch, seq // MLSTM_CHUNK, MLSTM_CHUNK).transpose(1, 2, 0, 3)
        mix_m, c_p, n_p, m_p = _mlstm(qm, km, vm, gt, gtt, om, b_row, b_col, gn_ml, None,
                                      batch=batch, chunk=MLSTM_CHUNK, valid=MLSTM_CHUNK, head_major=True)
        xp = _ffn(xp, g, (3, 4, 5), wgu[1], wd[1], mix=(mix_a, mix_m, wo))

        (qa_s, kaf_s, vaf_s, _, _, qm_s, km_s, vm_s, om_s, gt_s, gtt_s) = _proj(xs, g[2:3], w_in_p, False)
        seq3 = lambda a: a.reshape(dec_batch, SAMPLE_ROWS, width)
        mix_a_s = _sample_attention(page_table, seq3(qa_s), seq3(kaf_s), seq3(vaf_s), ck, cv, sample_bias,
                                    lam_q[l], lam_k[l], gn_att, lam_init)
        gtt_s = gtt_s.reshape(n_gate, dec_batch, 1, SAMPLE_ROWS).transpose(1, 2, 0, 3)
        m0 = jnp.broadcast_to(state_m[l][:, :, None], (dec_batch, H_ML, LANES))
        mix_m_s, c_s, n_s, m_s = _mlstm(seq3(qm_s), seq3(km_s), seq3(vm_s), gt_s, gtt_s, om_s, b_row, b_col, gn_ml,
                                        (state_C[l], state_n[l], m0),
                                        batch=dec_batch, chunk=SAMPLE_ROWS, valid=dec_seq, head_major=False)
        xs = _ffn(xs, g, (3, 4, 5), wgu[1], wd[1], mix=(mix_a_s.reshape(-1, width), mix_m_s, wo))

        new = lambda a: a.reshape(dec_batch, SAMPLE_ROWS, H_ATT, LANES)[:, :dec_seq]
        for lst, val in zip(outs, (kaf.reshape(batch, seq, H_ATT, LANES), vaf.reshape(batch, seq, H_ATT, LANES),
                                   c_p, n_p, m_p[:, :, 0], new(kaf_s), new(vaf_s), c_s, n_s, m_s[:, :, 0])):
            lst.append(val)

    y_prompt = xp.reshape(batch, seq, d)
    y_sample = xs.reshape(dec_batch, SAMPLE_ROWS, d)[:, :dec_seq]
    return (y_prompt, y_sample) + tuple(jnp.stack(o) for o in outs)
```

```python
import functools
import math

import jax
import jax.numpy as jnp
from jax import lax
from jax.experimental import pallas as pl
from jax.experimental.pallas import tpu as pltpu

F32 = jnp.float32
BF16 = jnp.bfloat16

H_ATT = 4
ATT_DQ = 64
H_ML = 4
NUM_BUCKETS = 32
MAX_DISTANCE = 128
EPS = 1e-6

LANES = 128
SUBLANES = 8
VMEM_LIMIT_BYTES = 56 * 1024 * 1024

TOKEN_TILE = 512
FFN_CHUNK = 256
ATT_TILE = 256
SAMPLE_ROWS = 8
WEIGHT_ROW_TILE = 256
MLSTM_CHUNK = 256
MLSTM_UNROLL = 1
MLSTM_SAMPLE_SEQS = 8


def _params(*sem):
    return pltpu.CompilerParams(dimension_semantics=sem, vmem_limit_bytes=VMEM_LIMIT_BYTES)


def _rms(x, g):
    return x * lax.rsqrt(jnp.mean(x * x, axis=-1, keepdims=True) + EPS) * g


def _dot(a, b):
    return jnp.dot(a, b, preferred_element_type=F32)


def _dot_nt(a, b):
    return lax.dot_general(a, b, (((1,), (1,)), ((), ())), preferred_element_type=F32)


def _dot_tn(a, b):
    return lax.dot_general(a, b, (((0,), (0,)), ((), ())), preferred_element_type=F32)


def _resident(arr, lead=None):
    if lead is None:
        return pl.BlockSpec(arr.shape, lambda *_: (0,) * arr.ndim, pipeline_mode=pl.Buffered(1))
    return pl.BlockSpec((None,) + arr.shape[1:], lambda *_: (lead,) + (0,) * (arr.ndim - 1),
                        pipeline_mode=pl.Buffered(1))


def _cast_kernel(x_ref, o_ref):
    w = x_ref.shape[-1]
    if o_ref.shape[-1] != w:
        o_ref[...] = jnp.zeros(o_ref.shape, o_ref.dtype)
    o_ref[:, :w] = x_ref[...].astype(o_ref.dtype)


def _cast_groups(x3, first, count, cols_out, row_tile):
    _, rows, cols = x3.shape
    return pl.pallas_call(
        _cast_kernel,
        grid=(count, rows // row_tile),
        in_specs=[pl.BlockSpec((None, row_tile, cols), lambda g, r: (first + g, r, 0))],
        out_specs=pl.BlockSpec((None, row_tile, cols_out), lambda g, r: (g, r, 0)),
        out_shape=jax.ShapeDtypeStruct((count, rows, cols_out), BF16),
        compiler_params=_params("parallel", "parallel"),
        name="cast_weights",
    )(x3)


def _gate_up_kernel(g_ref, u_ref, o_ref):
    for c in range(g_ref.shape[-1] // FFN_CHUNK):
        src = slice(c * FFN_CHUNK, (c + 1) * FFN_CHUNK)
        o_ref[:, 2 * c * FFN_CHUNK:(2 * c + 1) * FFN_CHUNK] = g_ref[:, src].astype(BF16)
        o_ref[:, (2 * c + 1) * FFN_CHUNK:(2 * c + 2) * FFN_CHUNK] = u_ref[:, src].astype(BF16)


def _fuse_gate_up(wg3, wu3, first, count, row_tile):
    _, rows, cols = wg3.shape
    spec = pl.BlockSpec((None, row_tile, cols), lambda g, r: (first + g, r, 0))
    return pl.pallas_call(
        _gate_up_kernel,
        grid=(count, rows // row_tile),
        in_specs=[spec, spec],
        out_specs=pl.BlockSpec((None, row_tile, 2 * cols), lambda g, r: (g, r, 0)),
        out_shape=jax.ShapeDtypeStruct((count, rows, 2 * cols), BF16),
        compiler_params=_params("parallel", "parallel"),
        name="fuse_gate_up",
    )(wg3, wu3)


def _ffn_kernel(*refs, with_mix, n_chunks, gain_rows):
    mix_row, pre_row, post_row = gain_rows
    if with_mix:
        (x_ref, ma_ref, mm_ref, wo_ref, g_ref, wgu_ref, wd_ref, o_ref, u_ref) = refs
        half = ma_ref.shape[-1]
        y = _dot(ma_ref[...].astype(BF16), wo_ref[:half, :]) + _dot(mm_ref[...].astype(BF16), wo_ref[half:, :])
        x = x_ref[...] + _rms(y, g_ref[mix_row:mix_row + 1, :])
    else:
        (x_ref, g_ref, wgu_ref, wd_ref, o_ref, u_ref) = refs
        x = x_ref[...]
    h = _rms(x, g_ref[pre_row:pre_row + 1, :]).astype(BF16)
    for c in range(n_chunks):
        gu = _dot(h, wgu_ref[:, 2 * c * FFN_CHUNK:2 * (c + 1) * FFN_CHUNK])
        gate = gu[:, :FFN_CHUNK]
        up = gu[:, FFN_CHUNK:]
        u_ref[:, c * FFN_CHUNK:(c + 1) * FFN_CHUNK] = (gate * jax.nn.sigmoid(gate) * up).astype(BF16)
    y = _dot(u_ref[...], wd_ref[...])
    o_ref[...] = x + 0.5 * _rms(y, g_ref[post_row:post_row + 1, :])


def _ffn(x, gains, gain_rows, wgu, wd, which, mix=None):
    n, d = x.shape
    f = wd.shape[1]
    n_chunks = f // FFN_CHUNK
    tm = TOKEN_TILE
    row = lambda w: pl.BlockSpec((tm, w), lambda i: (i, 0))
    in_specs = [row(d)]
    args = [x]
    if mix is not None:
        ma, mm, wo = mix
        in_specs += [row(ma.shape[1]), row(mm.shape[1]), _resident(wo, 0)]
        args += [ma, mm, wo]
    in_specs += [_resident(gains), _resident(wgu, which), _resident(wd, which)]
    args += [gains, wgu, wd]
    return pl.pallas_call(
        functools.partial(_ffn_kernel, with_mix=mix is not None, n_chunks=n_chunks, gain_rows=gain_rows),
        grid=(n // tm,),
        in_specs=in_specs,
        out_specs=row(d),
        out_shape=jax.ShapeDtypeStruct((n, d), F32),
        scratch_shapes=[pltpu.VMEM((tm, f), BF16)],
        compiler_params=_params("parallel"),
        name="ffn_mix" if mix is not None else "ffn",
    )(*args)


def _proj_kernel(x_ref, g_ref, w_ref, *outs, head_major, width, gate_chunk):
    (qa_ref, kaf_ref, vaf_ref, ka_ref, va_ref, qm_ref, km_ref, vm_ref, om_ref, gt_ref, gtt_ref) = outs
    xn = _rms(x_ref[...], g_ref[...]).astype(BF16)
    tm = x_ref.shape[0]
    n_heads = width // LANES

    def group(i):
        return _dot(xn, w_ref[:, i * width:(i + 1) * width])

    def put(ref, val):
        if head_major:
            for h in range(n_heads):
                ref[h] = val[:, h * LANES:(h + 1) * LANES].astype(ref.dtype)
        else:
            ref[...] = val.astype(ref.dtype)

    def put_rows(ref, val):
        for h in range(n_heads):
            ref[pl.ds(h, tm, stride=n_heads), :] = val[:, h * LANES:(h + 1) * LANES]

    put(qa_ref, group(0) * (ATT_DQ ** -0.5))
    ka = group(1)
    put_rows(kaf_ref, ka)
    put(ka_ref, ka)
    va = group(2)
    put_rows(vaf_ref, va)
    put(va_ref, va)
    put(qm_ref, group(3))
    put(km_ref, group(4) * (LANES ** -0.5))
    put(vm_ref, group(5))
    om_ref[...] = group(6)
    gates = _dot(xn, w_ref[:, 7 * width:7 * width + LANES])
    gt_ref[...] = gates
    gates_t = gates.T[:2 * H_ML, :]
    for c in range(tm // gate_chunk):
        gtt_ref[c] = gates_t[:, c * gate_chunk:(c + 1) * gate_chunk]


def _proj(x, gain, w_in_p, head_major, gate_chunk):
    n, d = x.shape
    width = H_ATT * LANES
    n_heads = width // LANES
    tm = TOKEN_TILE
    row = lambda w: pl.BlockSpec((tm, w), lambda i: (i, 0))
    if head_major:
        hm_spec = pl.BlockSpec((n_heads, tm, LANES), lambda i: (0, i, 0))
        hm_shape = jax.ShapeDtypeStruct((n_heads, n, LANES), BF16)
    else:
        hm_spec = row(width)
        hm_shape = jax.ShapeDtypeStruct((n, width), F32)
    f32_shape = jax.ShapeDtypeStruct((n, width), F32)
    rows_spec = pl.BlockSpec((tm * n_heads, LANES), lambda i: (i, 0))
    rows_shape = jax.ShapeDtypeStruct((n * n_heads, LANES), F32)
    out_specs = [hm_spec, rows_spec, rows_spec, hm_spec, hm_spec, hm_spec, hm_spec, hm_spec, row(width),
                 row(LANES), pl.BlockSpec((tm // gate_chunk, 2 * H_ML, gate_chunk), lambda i: (i, 0, 0))]
    out_shape = [hm_shape, rows_shape, rows_shape, hm_shape, hm_shape, hm_shape, hm_shape, hm_shape, f32_shape,
                 jax.ShapeDtypeStruct((n, LANES), F32),
                 jax.ShapeDtypeStruct((n // gate_chunk, 2 * H_ML, gate_chunk), F32)]
    return pl.pallas_call(
        functools.partial(_proj_kernel, head_major=head_major, width=width, gate_chunk=gate_chunk),
        grid=(n // tm,),
        in_specs=[row(d), _resident(gain), _resident(w_in_p, 0)],
        out_specs=out_specs,
        out_shape=out_shape,
        compiler_params=_params("parallel"),
        name="proj_prompt" if head_major else "proj_sample",
    )(x, gain, w_in_p)


def _t5_bias(rel, rb_ref, h):
    n = jnp.maximum(rel, 0)
    max_exact = NUM_BUCKETS // 2
    large = max_exact + (jnp.log(jnp.maximum(n, max_exact).astype(F32) / max_exact)
                         / math.log(MAX_DISTANCE / max_exact) * (NUM_BUCKETS - max_exact)).astype(jnp.int32)
    large = jnp.minimum(large, NUM_BUCKETS - 1)
    bucket = jnp.where(n < max_exact, n, large)
    out = jnp.zeros(rel.shape, F32)
    for i in range(NUM_BUCKETS):
        out = jnp.where(bucket == i, rb_ref[i, h], out)
    return out - rb_ref[NUM_BUCKETS - 1, h]


def _prompt_bias_kernel(rb_ref, o_ref, *, tile):
    r = lax.broadcasted_iota(jnp.int32, (tile, tile), 0)
    c = lax.broadcasted_iota(jnp.int32, (tile, tile), 1)
    for h in range(H_ATT):
        o_ref[h, 0] = jnp.where(c <= r, _t5_bias(r - c, rb_ref, h), -jnp.inf)
        o_ref[h, 1] = _t5_bias(tile + r - c, rb_ref, h)


def _prompt_bias(rel_bias, tile):
    return pl.pallas_call(
        functools.partial(_prompt_bias_kernel, tile=tile),
        in_specs=[pl.BlockSpec(memory_space=pltpu.SMEM)],
        out_shape=jax.ShapeDtypeStruct((H_ATT, 2, tile, tile), F32),
        name="prompt_bias",
    )(rel_bias)


def _sample_bias_kernel(rb_ref, o_ref, *, page, valid):
    qi = lax.broadcasted_iota(jnp.int32, (SUBLANES, 2 * LANES), 0)
    col = lax.broadcasted_iota(jnp.int32, (SUBLANES, 2 * LANES), 1)
    kj = col - page
    for h in range(H_ATT):
        past = _t5_bias(page + qi - col, rb_ref, h)
        new = jnp.where((kj <= qi) & (kj < valid), _t5_bias(qi - kj, rb_ref, h), -jnp.inf)
        tile = jnp.where(col < page, past, new)
        for comp in range(2):
            o_ref[(comp * H_ATT + h) * SUBLANES:(comp * H_ATT + h + 1) * SUBLANES, :] = tile


def _sample_bias(rel_bias, page, valid):
    return pl.pallas_call(
        functools.partial(_sample_bias_kernel, page=page, valid=valid),
        in_specs=[pl.BlockSpec(memory_space=pltpu.SMEM)],
        out_shape=jax.ShapeDtypeStruct((2 * H_ATT * SUBLANES, 2 * LANES), F32),
        name="sample_bias",
    )(rel_bias)


def _lambda(lq_ref, lk_ref, lam_init):
    prod = lq_ref[...] * lk_ref[...]
    return (jnp.exp(jnp.sum(prod[0:1, :], axis=-1, keepdims=True))
            - jnp.exp(jnp.sum(prod[1:2, :], axis=-1, keepdims=True)) + lam_init)


def _attn_kernel(q_ref, k_ref, v_ref, bias_ref, lq_ref, lk_ref, gn_ref, o_ref, vx_scr, *, tile, n_q, lam_init):
    i = pl.program_id(2)
    dv = v_ref.shape[-1]

    @pl.when(i == 0)
    def _():
        vx_scr[:, :dv] = v_ref[0]
        lane = lax.broadcasted_iota(jnp.int32, (vx_scr.shape[0], vx_scr.shape[1] - dv), 1)
        vx_scr[:, dv:] = jnp.where(lane == 0, 1.0, 0.0).astype(BF16)

    q = q_ref[0]
    lane = lax.broadcasted_iota(jnp.int32, q.shape, 1)
    zero = jnp.zeros_like(q)
    qq = jnp.concatenate([jnp.where(lane < ATT_DQ, q, zero), jnp.where(lane >= ATT_DQ, q, zero)], axis=0)
    lam = _lambda(lq_ref, lk_ref, lam_init)

    for n in range(n_q):
        @pl.when(i == n)
        def _(n=n):
            n_keys = (n + 1) * tile
            s = _dot_nt(qq, k_ref[0, :n_keys, :])
            diag = bias_ref[0, 0]
            parts = [s[:, n_keys - tile:] + jnp.concatenate([diag, diag], axis=0)]
            if n >= 1:
                near = bias_ref[0, 1]
                parts.insert(0, s[:, n_keys - 2 * tile:n_keys - tile] + jnp.concatenate([near, near], axis=0))
            if n >= 2:
                parts.insert(0, s[:, :n_keys - 2 * tile])
            m = functools.reduce(jnp.maximum, [jnp.max(x, axis=-1, keepdims=True) for x in parts])
            p = jnp.concatenate([jnp.exp(x - m) for x in parts], axis=1).astype(BF16)
            vx = vx_scr[:n_keys, :]
            acc0 = _dot(p[:tile], vx)
            acc1 = _dot(p[tile:], vx)
            out = acc0[:, :dv] / acc0[:, dv:dv + 1] - lam * (acc1[:, :dv] / acc1[:, dv:dv + 1])
            o_ref[...] = (_rms(out, gn_ref[...]) * (1.0 - lam_init)).astype(o_ref.dtype)


def _prompt_attention(q, k, v, bias, lam_q, lam_k, gain, batch, lam_init):
    h, n, dv = q.shape
    seq = n // batch
    tile = ATT_TILE
    nq = seq // tile
    return pl.pallas_call(
        functools.partial(_attn_kernel, tile=tile, n_q=nq, lam_init=lam_init),
        grid=(batch, h, nq),
        in_specs=[pl.BlockSpec((1, tile, dv), lambda b, hh, i: (hh, b * nq + i, 0)),
                  pl.BlockSpec((1, seq, dv), lambda b, hh, i: (hh, b, 0)),
                  pl.BlockSpec((1, seq, dv), lambda b, hh, i: (hh, b, 0)),
                  pl.BlockSpec((1, 2, tile, tile), lambda b, hh, i: (hh, 0, 0, 0)),
                  pl.BlockSpec(lam_q.shape, lambda b, hh, i: (0, 0)),
                  pl.BlockSpec(lam_k.shape, lambda b, hh, i: (0, 0)),
                  pl.BlockSpec(gain.shape, lambda b, hh, i: (0, 0))],
        out_specs=pl.BlockSpec((tile, dv), lambda b, hh, i: (b * nq + i, hh)),
        out_shape=jax.ShapeDtypeStruct((n, h * dv), BF16),
        scratch_shapes=[pltpu.VMEM((seq, 2 * dv), BF16)],
        compiler_params=_params("parallel", "parallel", "arbitrary"),
        name="attn_prompt",
    )(q, k, v, bias, lam_q, lam_k, gain)


def _sample_attn_kernel(pt_ref, q_ref, kn_ref, vn_ref, bias_ref, lq_ref, lk_ref, gn_ref, *rest,
                        n_pages, page, lam_init):
    del pt_ref
    kp = rest[:n_pages]
    vp = rest[n_pages:2 * n_pages]
    o_ref, s_scr = rest[2 * n_pages:]
    q = q_ref[0]
    width = q.shape[-1]
    n_rows = 2 * H_ATT * SUBLANES
    qt = jnp.concatenate([q] * (2 * H_ATT), axis=0)
    r = lax.broadcasted_iota(jnp.int32, (n_rows, width), 0)
    c = lax.broadcasted_iota(jnp.int32, (n_rows, width), 1)
    comp = r // (H_ATT * SUBLANES)
    head = (r // SUBLANES) % H_ATT
    own = (c // LANES == head) & ((c // ATT_DQ) % 2 == comp)
    qbd = jnp.where(own, qt, 0.0).astype(BF16)

    def tokens(ref, n_tok):
        return jnp.concatenate([ref[pl.ds(h, n_tok, stride=H_ATT), :] for h in range(H_ATT)], axis=1)

    for p in range(n_pages):
        s = _dot_nt(qbd, tokens(kp[p], page).astype(BF16))
        if p == n_pages - 1:
            s = s + bias_ref[:, :page]
        s_scr[:, p * page:(p + 1) * page] = s
    n_tok_new = kn_ref.shape[0] // H_ATT
    pad = jnp.zeros((2 * SUBLANES - n_tok_new, width), F32)
    k_new = jnp.concatenate([tokens(kn_ref, n_tok_new), pad], axis=0).astype(BF16)
    v_new = jnp.concatenate([tokens(vn_ref, n_tok_new), pad], axis=0).astype(BF16)
    n_new = k_new.shape[0]
    s_new = _dot_nt(qbd, k_new) + bias_ref[:, page:page + n_new]

    s_all = s_scr[...]
    m = jnp.maximum(jnp.max(s_all, axis=-1, keepdims=True), jnp.max(s_new, axis=-1, keepdims=True))
    p_all = jnp.exp(s_all - m)
    p_new = jnp.exp(s_new - m)
    inv = 1.0 / (jnp.sum(p_all, axis=-1, keepdims=True) + jnp.sum(p_new, axis=-1, keepdims=True))
    lam = _lambda(lq_ref, lk_ref, lam_init)
    half = n_rows // 2
    a_all = (p_all[:half] * inv[:half] - lam * (p_all[half:] * inv[half:])).astype(BF16)
    a_new = (p_new[:half] * inv[:half] - lam * (p_new[half:] * inv[half:])).astype(BF16)
    out = _dot(a_new, v_new)
    for p in range(n_pages):
        out = out + _dot(a_all[:, p * page:(p + 1) * page], tokens(vp[p], page).astype(BF16))
    for h in range(H_ATT):
        blk = out[h * SUBLANES:(h + 1) * SUBLANES, h * LANES:(h + 1) * LANES]
        o_ref[0, :, h * LANES:(h + 1) * LANES] = _rms(blk, gn_ref[...]) * (1.0 - lam_init)


def _sample_attention(page_table, q, k_new, v_new, cache_k, cache_v, page, first_page, bias, lam_q, lam_k, gain,
                      lam_init):
    nb, rows, width = q.shape
    n_pages = page_table.shape[1]
    n_heads = width // LANES

    def page_spec(p):
        return pl.BlockSpec((page * n_heads, LANES), lambda b, pt: (first_page + pt[b, p], 0))

    seq_spec = pl.BlockSpec((1, rows, width), lambda b, pt: (b, 0, 0))
    new_spec = pl.BlockSpec((rows * n_heads, LANES), lambda b, pt: (b, 0))
    full = lambda a: pl.BlockSpec(a.shape, lambda b, pt: (0,) * a.ndim)
    grid_spec = pltpu.PrefetchScalarGridSpec(
        num_scalar_prefetch=1,
        grid=(nb,),
        in_specs=[seq_spec, new_spec, new_spec, full(bias), full(lam_q), full(lam_k), full(gain)]
                 + [page_spec(p) for p in range(n_pages)] * 2,
        out_specs=seq_spec,
        scratch_shapes=[pltpu.VMEM((bias.shape[0], n_pages * page), F32)],
    )
    return pl.pallas_call(
        functools.partial(_sample_attn_kernel, n_pages=n_pages, page=page, lam_init=lam_init),
        grid_spec=grid_spec,
        out_shape=jax.ShapeDtypeStruct((nb, rows, width), F32),
        compiler_params=_params("parallel"),
        name="attn_sample",
    )(page_table, q, k_new, v_new, bias, lam_q, lam_k, gain, *([cache_k] * n_pages), *([cache_v] * n_pages))


def _log_sigmoid(x):
    return jnp.minimum(x, 0.0) - jnp.log1p(jnp.exp(-jnp.abs(x)))


def _mlstm_kernel(*refs, chunk, n_chunks, n_seq, valid, head_major, has_state, unroll):
    refs = list(refs)
    q_ref, k_ref, v_ref, g_ref, gt_ref, om_ref, brow_ref, bcol_ref, gn_ref = refs[:9]
    refs = refs[9:]
    if has_state:
        c0_ref, n0_ref, m0_ref = refs[:3]
        refs = refs[3:]
    hm_ref, c_ref, n_ref, m_ref = refs
    L = chunk

    if has_state:
        c_ref[...] = c0_ref[...]
        n_ref[...] = n0_ref[...]
        m_ref[...] = m0_ref[...]
    else:
        c_ref[...] = jnp.zeros(c_ref.shape, F32)
        n_ref[...] = jnp.zeros(n_ref.shape, F32)
        m_ref[...] = jnp.zeros(m_ref.shape, F32)

    t_i = lax.broadcasted_iota(jnp.int32, (L, L), 0)
    s_i = lax.broadcasted_iota(jnp.int32, (L, L), 1)
    causal = s_i <= t_i
    lower = causal.astype(F32)
    upper = (t_i <= s_i).astype(F32)

    def load(ref, seq_i, rows, h):
        if head_major:
            return ref[h, rows, :]
        return ref[seq_i, :, h * LANES:(h + 1) * LANES].astype(BF16)

    def step(seq_i, c):
        row0 = seq_i * (n_chunks * L) + c * L
        rows = pl.ds(row0 if isinstance(row0, int) else pl.multiple_of(row0, L), L)
        g_col = g_ref[rows, :] + brow_ref[...]
        g_row = gt_ref[seq_i * n_chunks + c] + bcol_ref[...]
        lf_col = _log_sigmoid(g_col)
        lf_row = _log_sigmoid(g_row)
        if valid < L:
            tok_c = lax.broadcasted_iota(jnp.int32, g_col.shape, 0)
            tok_r = lax.broadcasted_iota(jnp.int32, g_row.shape, 1)
            g_col = jnp.where(tok_c < valid, g_col, -1e30)
            g_row = jnp.where(tok_r < valid, g_row, -1e30)
            lf_col = jnp.where(tok_c < valid, lf_col, 0.0)
            lf_row = jnp.where(tok_r < valid, lf_row, 0.0)
        b_col = jnp.dot(lower, lf_col, preferred_element_type=F32, precision=lax.Precision.HIGHEST)
        b_row = jnp.dot(lf_row, upper, preferred_element_type=F32, precision=lax.Precision.HIGHEST)
        for h in range(H_ML):
            q = load(q_ref, seq_i, rows, h)
            k = load(k_ref, seq_i, rows, h)
            v = load(v_ref, seq_i, rows, h)
            bc = b_col[:, H_ML + h:H_ML + h + 1]
            ic = g_col[:, h:h + 1]
            br = b_row[H_ML + h:H_ML + h + 1, :]
            ir = g_row[h:h + 1, :]
            log_d = jnp.where(causal, bc - br + ir, -jnp.inf)
            a_loc = jnp.max(log_d, axis=-1, keepdims=True)
            s = _dot_nt(q, k) * jnp.exp(log_d - a_loc)
            num_loc = _dot(s.astype(BF16), v)
            den_loc = jnp.sum(s, axis=-1, keepdims=True)
            b_last = bc[L - 1:L, :]
            log_w = b_last - bc + ic
            w_max = jnp.max(log_w, axis=0, keepdims=True)
            ws = jnp.exp(log_w - w_max)
            kv = _dot_tn(k, (ws * v.astype(F32)).astype(BF16))
            ksum = jnp.sum(ws * k.astype(F32), axis=0, keepdims=True)
            cm = c_ref[seq_i, h]
            nr = n_ref[seq_i, h:h + 1, :]
            m = m_ref[seq_i, h:h + 1, 0:1]
            inter = bc + m
            m_t = jnp.maximum(inter, a_loc)
            e_loc = jnp.exp(a_loc - m_t)
            w_inter = jnp.exp(inter - m_t)
            num = num_loc * e_loc + w_inter * _dot(q, cm.astype(BF16))
            qn = jnp.sum(q.astype(F32) * nr, axis=-1, keepdims=True)
            den = den_loc * e_loc + w_inter * qn
            hv = num / jnp.maximum(jnp.abs(den), jnp.exp(-m_t))
            m_new = jnp.maximum(b_last + m, w_max)
            fw = jnp.exp(b_last + m - m_new)
            fu = jnp.exp(w_max - m_new)
            c_ref[seq_i, h] = fw * cm + fu * kv
            n_ref[seq_i, h:h + 1, :] = fw * nr + fu * ksum
            m_ref[seq_i, h:h + 1, :] = jnp.broadcast_to(m_new, (1, LANES))
            gate = jax.nn.sigmoid(om_ref[rows, h * LANES:(h + 1) * LANES])
            hm_ref[rows, h * LANES:(h + 1) * LANES] = (_rms(hv, gn_ref[...]) * gate).astype(hm_ref.dtype)

    for seq_i in range(n_seq):
        if n_chunks == 1:
            step(seq_i, 0)
        else:
            def body(c, carry, seq_i=seq_i):
                step(seq_i, c)
                return carry
            lax.fori_loop(0, n_chunks, body, 0, unroll=unroll)


def _mlstm(q, k, v, gates, gates_t, om, b_row, b_col, gain, state, *, batch, chunk, valid, head_major,
           n_seq=1, unroll=1):
    n, width = om.shape
    seq = n // batch
    n_chunks = seq // chunk
    dh = LANES
    if head_major:
        assert n_seq == 1
        qkv_spec = pl.BlockSpec((H_ML, seq, dh), lambda b: (0, b, 0))
    else:
        qkv_spec = pl.BlockSpec((n_seq, seq, width), lambda b: (b, 0, 0))
    row = lambda w: pl.BlockSpec((n_seq * seq, w), lambda b: (b, 0))
    full = lambda a: pl.BlockSpec(a.shape, lambda b: (0,) * a.ndim)
    c_spec = pl.BlockSpec((n_seq, H_ML, dh, dh), lambda b: (b, 0, 0, 0))
    v_spec = pl.BlockSpec((n_seq, H_ML, dh), lambda b: (b, 0, 0))
    in_specs = [qkv_spec, qkv_spec, qkv_spec, row(LANES),
                pl.BlockSpec((n_seq * n_chunks, 2 * H_ML, chunk), lambda b: (b, 0, 0)),
                row(width), full(b_row), full(b_col), full(gain)]
    args = [q, k, v, gates, gates_t, om, b_row, b_col, gain]
    if state is not None:
        in_specs += [c_spec, v_spec, v_spec]
        args += list(state)
    return pl.pallas_call(
        functools.partial(_mlstm_kernel, chunk=chunk, n_chunks=n_chunks, n_seq=n_seq, valid=valid,
                          head_major=head_major, has_state=state is not None, unroll=unroll),
        grid=(batch // n_seq,),
        in_specs=in_specs,
        out_specs=[row(width), c_spec, v_spec, v_spec],
        out_shape=[jax.ShapeDtypeStruct((n, width), BF16 if head_major else F32),
                   jax.ShapeDtypeStruct((batch, H_ML, dh, dh), F32),
                   jax.ShapeDtypeStruct((batch, H_ML, dh), F32),
                   jax.ShapeDtypeStruct((batch, H_ML, dh), F32)],
        compiler_params=_params("parallel"),
        name="mlstm_prompt" if head_major else "mlstm_sample",
    )(*args)


def kernel(x_prompt, x_sample, cache_k, cache_v, state_C, state_n, state_m, page_table, rel_bias, norm_gains,
           ffn_w_gate, ffn_w_up, ffn_w_down, w_in, b_gates, lam_q, lam_k, attn_norm, mlstm_norm, w_out):
    batch, seq, d = x_prompt.shape
    dec_batch, dec_seq, _ = x_sample.shape
    depth = w_in.shape[0]
    page = cache_k.shape[2]
    width = H_ATT * LANES
    d_in = w_in.shape[-1]
    n_gate = 2 * H_ML
    assert d_in == 7 * width + n_gate and dec_seq <= SAMPLE_ROWS
    assert seq % ATT_TILE == 0 and seq % MLSTM_CHUNK == 0 and (batch * seq) % TOKEN_TILE == 0

    xp = x_prompt.reshape(batch * seq, d)
    xs = jnp.pad(x_sample, ((0, 0), (0, SAMPLE_ROWS - dec_seq), (0, 0))).reshape(dec_batch * SAMPLE_ROWS, d)

    prompt_bias = _prompt_bias(rel_bias, ATT_TILE)
    sample_bias = _sample_bias(rel_bias, page, dec_seq)

    outs = [[] for _ in range(10)]
    for l in range(depth):
        lam_init = 0.8 - 0.6 * math.exp(-0.3 * l)
        g = norm_gains[l]
        f = ffn_w_gate.shape[-1]
        wgu = _fuse_gate_up(ffn_w_gate.reshape(-1, d, f), ffn_w_up.reshape(-1, d, f), 2 * l, 2, WEIGHT_ROW_TILE)
        wd = _cast_groups(ffn_w_down.reshape(-1, f, d), 2 * l, 2, d, f // 4)
        w_in_p = _cast_groups(w_in, l, 1, d_in - n_gate + LANES, WEIGHT_ROW_TILE)
        wo = _cast_groups(w_out, l, 1, d, WEIGHT_ROW_TILE)
        b_row = jnp.pad(b_gates[l].reshape(1, n_gate), ((0, 0), (0, LANES - n_gate)))
        b_col = b_gates[l].reshape(n_gate, 1)
        gn_att = attn_norm[l].reshape(1, -1)
        gn_ml = mlstm_norm[l].reshape(1, -1)
        ck = cache_k.reshape(-1, LANES)
        cv = cache_v.reshape(-1, LANES)
        first_page = l * cache_k.shape[1]

        xp = _ffn(xp, g, (None, 0, 1), wgu, wd, 0)
        xs = _ffn(xs, g, (None, 0, 1), wgu, wd, 0)

        (qa, kaf, vaf, ka, va, qm, km, vm, om, gt, gtt) = _proj(xp, g[2:3], w_in_p, True, MLSTM_CHUNK)
        mix_a = _prompt_attention(qa, ka, va, prompt_bias, lam_q[l], lam_k[l], gn_att, batch, lam_init)
        mix_m, c_p, n_p, m_p = _mlstm(qm, km, vm, gt, gtt, om, b_row, b_col, gn_ml, None,
                                      batch=batch, chunk=MLSTM_CHUNK, valid=MLSTM_CHUNK, head_major=True,
                                      unroll=MLSTM_UNROLL)
        xp = _ffn(xp, g, (3, 4, 5), wgu, wd, 1, mix=(mix_a, mix_m, wo))

        (qa_s, kaf_s, vaf_s, _, _, qm_s, km_s, vm_s, om_s, gt_s, gtt_s) = _proj(xs, g[2:3], w_in_p, False,
                                                                                SAMPLE_ROWS)
        seq3 = lambda a: a.reshape(dec_batch, SAMPLE_ROWS, width)
        mix_a_s = _sample_attention(page_table, seq3(qa_s), kaf_s, vaf_s, ck, cv, page, first_page, sample_bias,
                                    lam_q[l], lam_k[l], gn_att, lam_init)
        m0 = jnp.broadcast_to(state_m[l][:, :, None], (dec_batch, H_ML, LANES))
        mix_m_s, c_s, n_s, m_s = _mlstm(seq3(qm_s), seq3(km_s), seq3(vm_s), gt_s, gtt_s, om_s, b_row, b_col, gn_ml,
                                        (state_C[l], state_n[l], m0),
                                        batch=dec_batch, chunk=SAMPLE_ROWS, valid=dec_seq, head_major=False,
                                        n_seq=math.gcd(dec_batch, MLSTM_SAMPLE_SEQS))
        xs = _ffn(xs, g, (3, 4, 5), wgu, wd, 1, mix=(mix_a_s.reshape(-1, width), mix_m_s, wo))

        new = lambda a: a.reshape(dec_batch, SAMPLE_ROWS, H_ATT, LANES)[:, :dec_seq]
        for lst, val in zip(outs, (kaf.reshape(batch, seq, H_ATT, LANES), vaf.reshape(batch, seq, H_ATT, LANES),
                                   c_p, n_p, m_p[:, :, 0], new(kaf_s), new(vaf_s), c_s, n_s, m_s[:, :, 0])):
            lst.append(val)

    y_prompt = xp.reshape(batch, seq, d)
    y_sample = xs.reshape(dec_batch, SAMPLE_ROWS, d)[:, :dec_seq]
    return (y_prompt, y_sample) + tuple(jnp.stack(o) for o in outs)
```

```python
import functools
import math

import jax
import jax.numpy as jnp
from jax import lax
from jax.experimental import pallas as pl
from jax.experimental.pallas import tpu as pltpu

F32 = jnp.float32
BF16 = jnp.bfloat16

H_ATT = 4
ATT_DQ = 64
H_ML = 4
NUM_BUCKETS = 32
MAX_DISTANCE = 128
EPS = 1e-6

LANES = 128
SUBLANES = 8
VMEM_LIMIT_BYTES = 56 * 1024 * 1024

TOKEN_TILE = 512
FFN_CHUNK = 256
ATT_TILE = 256
SAMPLE_ROWS = 8
WEIGHT_ROW_TILE = 256
MLSTM_CHUNK = 256
MLSTM_UNROLL = 1
MLSTM_SAMPLE_SEQS = 8


def _params(*sem):
    return pltpu.CompilerParams(dimension_semantics=sem, vmem_limit_bytes=VMEM_LIMIT_BYTES)


def _rms(x, g):
    return x * lax.rsqrt(jnp.mean(x * x, axis=-1, keepdims=True) + EPS) * g


def _dot(a, b):
    return jnp.dot(a, b, preferred_element_type=F32)


def _dot_nt(a, b):
    return lax.dot_general(a, b, (((1,), (1,)), ((), ())), preferred_element_type=F32)


def _dot_tn(a, b):
    return lax.dot_general(a, b, (((0,), (0,)), ((), ())), preferred_element_type=F32)


def _resident(arr, lead=None):
    if lead is None:
        return pl.BlockSpec(arr.shape, lambda *_: (0,) * arr.ndim, pipeline_mode=pl.Buffered(1))
    return pl.BlockSpec((None,) + arr.shape[1:], lambda *_: (lead,) + (0,) * (arr.ndim - 1),
                        pipeline_mode=pl.Buffered(1))


def _cast_kernel(x_ref, o_ref):
    w = x_ref.shape[-1]
    if o_ref.shape[-1] != w:
        o_ref[...] = jnp.zeros(o_ref.shape, o_ref.dtype)
    o_ref[:, :w] = x_ref[...].astype(o_ref.dtype)


def _cast_groups(x3, first, count, cols_out, row_tile):
    _, rows, cols = x3.shape
    return pl.pallas_call(
        _cast_kernel,
        grid=(count, rows // row_tile),
        in_specs=[pl.BlockSpec((None, row_tile, cols), lambda g, r: (first + g, r, 0))],
        out_specs=pl.BlockSpec((None, row_tile, cols_out), lambda g, r: (g, r, 0)),
        out_shape=jax.ShapeDtypeStruct((count, rows, cols_out), BF16),
        compiler_params=_params("parallel", "parallel"),
        name="cast_weights",
    )(x3)


def _gate_up_kernel(g_ref, u_ref, o_ref):
    for c in range(g_ref.shape[-1] // FFN_CHUNK):
        src = slice(c * FFN_CHUNK, (c + 1) * FFN_CHUNK)
        o_ref[:, 2 * c * FFN_CHUNK:(2 * c + 1) * FFN_CHUNK] = g_ref[:, src].astype(BF16)
        o_ref[:, (2 * c + 1) * FFN_CHUNK:(2 * c + 2) * FFN_CHUNK] = u_ref[:, src].astype(BF16)


def _fuse_gate_up(wg3, wu3, first, count, row_tile):
    _, rows, cols = wg3.shape
    spec = pl.BlockSpec((None, row_tile, cols), lambda g, r: (first + g, r, 0))
    return pl.pallas_call(
        _gate_up_kernel,
        grid=(count, rows // row_tile),
        in_specs=[spec, spec],
        out_specs=pl.BlockSpec((None, row_tile, 2 * cols), lambda g, r: (g, r, 0)),
        out_shape=jax.ShapeDtypeStruct((count, rows, 2 * cols), BF16),
        compiler_params=_params("parallel", "parallel"),
        name="fuse_gate_up",
    )(wg3, wu3)


def _ffn_kernel(*refs, with_mix, n_chunks, gain_rows):
    mix_row, pre_row, post_row = gain_rows
    if with_mix:
        (x_ref, ma_ref, mm_ref, wo_ref, g_ref, wgu_ref, wd_ref, o_ref, u_ref) = refs
        half = ma_ref.shape[-1]
        y = _dot(ma_ref[...].astype(BF16), wo_ref[:half, :]) + _dot(mm_ref[...].astype(BF16), wo_ref[half:, :])
        x = x_ref[...] + _rms(y, g_ref[mix_row:mix_row + 1, :])
    else:
        (x_ref, g_ref, wgu_ref, wd_ref, o_ref, u_ref) = refs
        x = x_ref[...]
    h = _rms(x, g_ref[pre_row:pre_row + 1, :]).astype(BF16)
    for c in range(n_chunks):
        gu = _dot(h, wgu_ref[:, 2 * c * FFN_CHUNK:2 * (c + 1) * FFN_CHUNK])
        gate = gu[:, :FFN_CHUNK]
        up = gu[:, FFN_CHUNK:]
        u_ref[:, c * FFN_CHUNK:(c + 1) * FFN_CHUNK] = (gate * jax.nn.sigmoid(gate) * up).astype(BF16)
    y = _dot(u_ref[...], wd_ref[...])
    o_ref[...] = x + 0.5 * _rms(y, g_ref[post_row:post_row + 1, :])


def _ffn(x, gains, gain_rows, wgu, wd, which, mix=None):
    n, d = x.shape
    f = wd.shape[1]
    n_chunks = f // FFN_CHUNK
    tm = TOKEN_TILE
    row = lambda w: pl.BlockSpec((tm, w), lambda i: (i, 0))
    in_specs = [row(d)]
    args = [x]
    if mix is not None:
        ma, mm, wo = mix
        in_specs += [row(ma.shape[1]), row(mm.shape[1]), _resident(wo, 0)]
        args += [ma, mm, wo]
    in_specs += [_resident(gains), _resident(wgu, which), _resident(wd, which)]
    args += [gains, wgu, wd]
    return pl.pallas_call(
        functools.partial(_ffn_kernel, with_mix=mix is not None, n_chunks=n_chunks, gain_rows=gain_rows),
        grid=(n // tm,),
        in_specs=in_specs,
        out_specs=row(d),
        out_shape=jax.ShapeDtypeStruct((n, d), F32),
        scratch_shapes=[pltpu.VMEM((tm, f), BF16)],
        compiler_params=_params("parallel"),
        name="ffn_mix" if mix is not None else "ffn",
    )(*args)


def _proj_kernel(x_ref, g_ref, w_ref, *outs, head_major, width, gate_chunk):
    (qa_ref, kaf_ref, vaf_ref, ka_ref, va_ref, qm_ref, km_ref, vm_ref, om_ref, gt_ref, gtt_ref) = outs
    xn = _rms(x_ref[...], g_ref[...]).astype(BF16)
    tm = x_ref.shape[0]
    n_heads = width // LANES

    def group(i):
        return _dot(xn, w_ref[:, i * width:(i + 1) * width])

    def put(ref, val):
        if head_major:
            for h in range(n_heads):
                ref[h] = val[:, h * LANES:(h + 1) * LANES].astype(ref.dtype)
        else:
            ref[...] = val.astype(ref.dtype)

    def put_rows(ref, val):
        for h in range(n_heads):
            ref[pl.ds(h, tm, stride=n_heads), :] = val[:, h * LANES:(h + 1) * LANES]

    put(qa_ref, group(0) * (ATT_DQ ** -0.5))
    ka = group(1)
    put_rows(kaf_ref, ka)
    put(ka_ref, ka)
    va = group(2)
    put_rows(vaf_ref, va)
    put(va_ref, va)
    put(qm_ref, group(3))
    put(km_ref, group(4) * (LANES ** -0.5))
    put(vm_ref, group(5))
    om_ref[...] = group(6)
    gates = _dot(xn, w_ref[:, 7 * width:7 * width + LANES])
    gt_ref[...] = gates
    gates_t = gates.T[:2 * H_ML, :]
    for c in range(tm // gate_chunk):
        gtt_ref[c] = gates_t[:, c * gate_chunk:(c + 1) * gate_chunk]


def _proj(x, gain, w_in_p, head_major, gate_chunk):
    n, d = x.shape
    width = H_ATT * LANES
    n_heads = width // LANES
    tm = TOKEN_TILE
    row = lambda w: pl.BlockSpec((tm, w), lambda i: (i, 0))
    if head_major:
        hm_spec = pl.BlockSpec((n_heads, tm, LANES), lambda i: (0, i, 0))
        hm_shape = jax.ShapeDtypeStruct((n_heads, n, LANES), BF16)
    else:
        hm_spec = row(width)
        hm_shape = jax.ShapeDtypeStruct((n, width), F32)
    f32_shape = jax.ShapeDtypeStruct((n, width), F32)
    rows_spec = pl.BlockSpec((tm * n_heads, LANES), lambda i: (i, 0))
    rows_shape = jax.ShapeDtypeStruct((n * n_heads, LANES), F32)
    out_specs = [hm_spec, rows_spec, rows_spec, hm_spec, hm_spec, hm_spec, hm_spec, hm_spec, row(width),
                 row(LANES), pl.BlockSpec((tm // gate_chunk, 2 * H_ML, gate_chunk), lambda i: (i, 0, 0))]
    out_shape = [hm_shape, rows_shape, rows_shape, hm_shape, hm_shape, hm_shape, hm_shape, hm_shape, f32_shape,
                 jax.ShapeDtypeStruct((n, LANES), F32),
                 jax.ShapeDtypeStruct((n // gate_chunk, 2 * H_ML, gate_chunk), F32)]
    return pl.pallas_call(
        functools.partial(_proj_kernel, head_major=head_major, width=width, gate_chunk=gate_chunk),
        grid=(n // tm,),
        in_specs=[row(d), _resident(gain), _resident(w_in_p, 0)],
        out_specs=out_specs,
        out_shape=out_shape,
        compiler_params=_params("parallel"),
        name="proj_prompt" if head_major else "proj_sample",
    )(x, gain, w_in_p)


def _t5_bias(rel, rb_ref, h):
    n = jnp.maximum(rel, 0)
    max_exact = NUM_BUCKETS // 2
    large = max_exact + (jnp.log(jnp.maximum(n, max_exact).astype(F32) / max_exact)
                         / math.log(MAX_DISTANCE / max_exact) * (NUM_BUCKETS - max_exact)).astype(jnp.int32)
    large = jnp.minimum(large, NUM_BUCKETS - 1)
    bucket = jnp.where(n < max_exact, n, large)
    out = jnp.zeros(rel.shape, F32)
    for i in range(NUM_BUCKETS):
        out = jnp.where(bucket == i, rb_ref[i, h], out)
    return out - rb_ref[NUM_BUCKETS - 1, h]


def _prompt_bias_kernel(rb_ref, o_ref, *, tile):
    r = lax.broadcasted_iota(jnp.int32, (tile, tile), 0)
    c = lax.broadcasted_iota(jnp.int32, (tile, tile), 1)
    for h in range(H_ATT):
        o_ref[h, 0] = jnp.where(c <= r, _t5_bias(r - c, rb_ref, h), -jnp.inf)
        o_ref[h, 1] = _t5_bias(tile + r - c, rb_ref, h)


def _prompt_bias(rel_bias, tile):
    return pl.pallas_call(
        functools.partial(_prompt_bias_kernel, tile=tile),
        in_specs=[pl.BlockSpec(memory_space=pltpu.SMEM)],
        out_shape=jax.ShapeDtypeStruct((H_ATT, 2, tile, tile), F32),
        name="prompt_bias",
    )(rel_bias)


def _sample_bias_kernel(rb_ref, o_ref, *, page, valid):
    qi = lax.broadcasted_iota(jnp.int32, (SUBLANES, 2 * page), 0)
    col = lax.broadcasted_iota(jnp.int32, (SUBLANES, 2 * page), 1)
    kj = col - page
    for h in range(H_ATT):
        past = _t5_bias(page + qi - col, rb_ref, h)
        new = jnp.where((kj <= qi) & (kj < valid), _t5_bias(qi - kj, rb_ref, h), -jnp.inf)
        tile = jnp.where(col < page, past, new)
        for comp in range(2):
            o_ref[(comp * H_ATT + h) * SUBLANES:(comp * H_ATT + h + 1) * SUBLANES, :] = tile


def _sample_bias(rel_bias, page, valid):
    return pl.pallas_call(
        functools.partial(_sample_bias_kernel, page=page, valid=valid),
        in_specs=[pl.BlockSpec(memory_space=pltpu.SMEM)],
        out_shape=jax.ShapeDtypeStruct((2 * H_ATT * SUBLANES, 2 * page), F32),
        name="sample_bias",
    )(rel_bias)


def _lambda(lq_ref, lk_ref, lam_init):
    prod = lq_ref[...] * lk_ref[...]
    return (jnp.exp(jnp.sum(prod[0:1, :], axis=-1, keepdims=True))
            - jnp.exp(jnp.sum(prod[1:2, :], axis=-1, keepdims=True)) + lam_init)


def _attn_kernel(q_ref, k_ref, v_ref, bias_ref, lq_ref, lk_ref, gn_ref, o_ref, vx_scr, *, tile, n_q, lam_init):
    dv = v_ref.shape[-1]
    vx_scr[:, :dv] = v_ref[0]
    ones_lane = lax.broadcasted_iota(jnp.int32, (vx_scr.shape[0], vx_scr.shape[1] - dv), 1)
    vx_scr[:, dv:] = jnp.where(ones_lane == 0, 1.0, 0.0).astype(BF16)

    lane = lax.broadcasted_iota(jnp.int32, (tile, dv), 1)
    zero = jnp.zeros((tile, dv), BF16)
    lam = _lambda(lq_ref, lk_ref, lam_init)
    diag = jnp.concatenate([bias_ref[0, 0]] * 2, axis=0)
    near = jnp.concatenate([bias_ref[0, 1]] * 2, axis=0)

    def logits(n):
        q = q_ref[0, n * tile:(n + 1) * tile, :]
        qq = jnp.concatenate([jnp.where(lane < ATT_DQ, q, zero), jnp.where(lane >= ATT_DQ, q, zero)], axis=0)
        n_keys = (n + 1) * tile
        s = _dot_nt(qq, k_ref[0, :n_keys, :])
        parts = [s[:, n_keys - tile:] + diag]
        if n >= 1:
            parts.insert(0, s[:, n_keys - 2 * tile:n_keys - tile] + near)
        if n >= 2:
            parts.insert(0, s[:, :n_keys - 2 * tile])
        return parts

    def probs(parts):
        m = functools.reduce(jnp.maximum, [jnp.max(x, axis=-1, keepdims=True) for x in parts])
        return jnp.concatenate([jnp.exp(x - m) for x in parts], axis=1).astype(BF16)

    def finish(n, p):
        vx = vx_scr[:(n + 1) * tile, :]
        acc0 = _dot(p[:tile], vx)
        acc1 = _dot(p[tile:], vx)
        out = acc0[:, :dv] / acc0[:, dv:dv + 1] - lam * (acc1[:, :dv] / acc1[:, dv:dv + 1])
        o_ref[n * tile:(n + 1) * tile, :] = (_rms(out, gn_ref[...]) * (1.0 - lam_init)).astype(o_ref.dtype)

    for g in range(n_q // 2):
        pair = (g, n_q - 1 - g)
        p_pair = [probs(logits(n)) for n in pair]
        for n, p in zip(pair, p_pair):
            finish(n, p)


def _prompt_attention(q, k, v, bias, lam_q, lam_k, gain, batch, lam_init):
    h, n, dv = q.shape
    seq = n // batch
    tile = ATT_TILE
    nq = seq // tile
    assert nq % 2 == 0
    seq_spec = pl.BlockSpec((1, seq, dv), lambda b, hh: (hh, b, 0))
    return pl.pallas_call(
        functools.partial(_attn_kernel, tile=tile, n_q=nq, lam_init=lam_init),
        grid=(batch, h),
        in_specs=[seq_spec, seq_spec, seq_spec,
                  pl.BlockSpec((1, 2, tile, tile), lambda b, hh: (hh, 0, 0, 0)),
                  pl.BlockSpec(lam_q.shape, lambda b, hh: (0, 0)),
                  pl.BlockSpec(lam_k.shape, lambda b, hh: (0, 0)),
                  pl.BlockSpec(gain.shape, lambda b, hh: (0, 0))],
        out_specs=pl.BlockSpec((seq, dv), lambda b, hh: (b, hh)),
        out_shape=jax.ShapeDtypeStruct((n, h * dv), BF16),
        scratch_shapes=[pltpu.VMEM((seq, 2 * dv), BF16)],
        compiler_params=_params("parallel", "parallel"),
        name="attn_prompt",
    )(q, k, v, bias, lam_q, lam_k, gain)


def _sample_attn_kernel(pt_ref, q_ref, kn_ref, vn_ref, bias_ref, lq_ref, lk_ref, gn_ref, *rest,
                        n_pages, page, lam_init):
    del pt_ref
    kp = rest[:n_pages]
    vp = rest[n_pages:2 * n_pages]
    o_ref, k_buf, v_buf = rest[2 * n_pages:]
    q = q_ref[0]
    width = q.shape[-1]
    n_rows = 2 * H_ATT * SUBLANES
    qt = jnp.concatenate([q] * (2 * H_ATT), axis=0)
    r = lax.broadcasted_iota(jnp.int32, (n_rows, width), 0)
    c = lax.broadcasted_iota(jnp.int32, (n_rows, width), 1)
    comp = r // (H_ATT * SUBLANES)
    head = (r // SUBLANES) % H_ATT
    own = (c // LANES == head) & ((c // ATT_DQ) % 2 == comp)
    qbd = jnp.where(own, qt, 0.0).astype(BF16)

    n_past = n_pages * page
    n_tok_new = kn_ref.shape[0] // H_ATT
    new_rows = 2 * SUBLANES
    for src, buf in ((kp, k_buf), (vp, v_buf)):
        for p in range(n_pages):
            for h in range(H_ATT):
                buf[p * page:(p + 1) * page, h * LANES:(h + 1) * LANES] = (
                    src[p][pl.ds(h, page, stride=H_ATT), :].astype(BF16))
    for src, buf in ((kn_ref, k_buf), (vn_ref, v_buf)):
        new = jnp.concatenate([src[pl.ds(h, n_tok_new, stride=H_ATT), :] for h in range(H_ATT)], axis=1)
        new = jnp.concatenate([new, jnp.zeros((new_rows - n_tok_new, width), F32)], axis=0)
        buf[n_past:n_past + new_rows, :] = new.astype(BF16)
        buf[n_past + new_rows:, :] = jnp.zeros((page - new_rows, width), BF16)

    s = _dot_nt(qbd, k_buf[...])
    parts = [s[:, :n_past - page], s[:, n_past - page:] + bias_ref[...]]
    m = functools.reduce(jnp.maximum, [jnp.max(x, axis=-1, keepdims=True) for x in parts])
    p = jnp.concatenate([jnp.exp(x - m) for x in parts], axis=1)
    inv = 1.0 / jnp.sum(p, axis=-1, keepdims=True)
    lam = _lambda(lq_ref, lk_ref, lam_init)
    half = n_rows // 2
    a = (p[:half] * inv[:half] - lam * (p[half:] * inv[half:])).astype(BF16)
    out = _dot(a, v_buf[...])
    for h in range(H_ATT):
        blk = out[h * SUBLANES:(h + 1) * SUBLANES, h * LANES:(h + 1) * LANES]
        o_ref[0, :, h * LANES:(h + 1) * LANES] = _rms(blk, gn_ref[...]) * (1.0 - lam_init)


def _sample_attention(page_table, q, k_new, v_new, cache_k, cache_v, page, first_page, bias, lam_q, lam_k, gain,
                      lam_init):
    nb, rows, width = q.shape
    n_pages = page_table.shape[1]
    n_heads = width // LANES

    def page_spec(p):
        return pl.BlockSpec((page * n_heads, LANES), lambda b, pt: (first_page + pt[b, p], 0))

    seq_spec = pl.BlockSpec((1, rows, width), lambda b, pt: (b, 0, 0))
    new_spec = pl.BlockSpec((rows * n_heads, LANES), lambda b, pt: (b, 0))
    full = lambda a: pl.BlockSpec(a.shape, lambda b, pt: (0,) * a.ndim)
    grid_spec = pltpu.PrefetchScalarGridSpec(
        num_scalar_prefetch=1,
        grid=(nb,),
        in_specs=[seq_spec, new_spec, new_spec, full(bias), full(lam_q), full(lam_k), full(gain)]
                 + [page_spec(p) for p in range(n_pages)] * 2,
        out_specs=seq_spec,
        scratch_shapes=[pltpu.VMEM(((n_pages + 1) * page, width), BF16)] * 2,
    )
    return pl.pallas_call(
        functools.partial(_sample_attn_kernel, n_pages=n_pages, page=page, lam_init=lam_init),
        grid_spec=grid_spec,
        out_shape=jax.ShapeDtypeStruct((nb, rows, width), F32),
        compiler_params=_params("parallel"),
        name="attn_sample",
    )(page_table, q, k_new, v_new, bias, lam_q, lam_k, gain, *([cache_k] * n_pages), *([cache_v] * n_pages))


def _log_sigmoid(x):
    return jnp.minimum(x, 0.0) - jnp.log1p(jnp.exp(-jnp.abs(x)))


def _mlstm_kernel(*refs, chunk, n_chunks, n_seq, valid, head_major, has_state, unroll):
    refs = list(refs)
    q_ref, k_ref, v_ref, g_ref, gt_ref, om_ref, brow_ref, bcol_ref, gn_ref = refs[:9]
    refs = refs[9:]
    if has_state:
        c0_ref, n0_ref, m0_ref = refs[:3]
        refs = refs[3:]
    hm_ref, c_ref, n_ref, m_ref = refs
    L = chunk

    if has_state:
        c_ref[...] = c0_ref[...]
        n_ref[...] = n0_ref[...]
        m_ref[...] = m0_ref[...]
    else:
        c_ref[...] = jnp.zeros(c_ref.shape, F32)
        n_ref[...] = jnp.zeros(n_ref.shape, F32)
        m_ref[...] = jnp.zeros(m_ref.shape, F32)

    t_i = lax.broadcasted_iota(jnp.int32, (L, L), 0)
    s_i = lax.broadcasted_iota(jnp.int32, (L, L), 1)
    causal = s_i <= t_i
    lower = causal.astype(F32)
    upper = (t_i <= s_i).astype(F32)

    def load(ref, seq_i, rows, h):
        if head_major:
            return ref[h, rows, :]
        return ref[seq_i, :, h * LANES:(h + 1) * LANES].astype(BF16)

    def step(seq_i, c):
        row0 = seq_i * (n_chunks * L) + c * L
        rows = pl.ds(row0 if isinstance(row0, int) else pl.multiple_of(row0, L), L)
        g_col = g_ref[rows, :] + brow_ref[...]
        g_row = gt_ref[seq_i * n_chunks + c] + bcol_ref[...]
        lf_col = _log_sigmoid(g_col)
        lf_row = _log_sigmoid(g_row)
        if valid < L:
            tok_c = lax.broadcasted_iota(jnp.int32, g_col.shape, 0)
            tok_r = lax.broadcasted_iota(jnp.int32, g_row.shape, 1)
            g_col = jnp.where(tok_c < valid, g_col, -1e30)
            g_row = jnp.where(tok_r < valid, g_row, -1e30)
            lf_col = jnp.where(tok_c < valid, lf_col, 0.0)
            lf_row = jnp.where(tok_r < valid, lf_row, 0.0)
        b_col = jnp.dot(lower, lf_col, preferred_element_type=F32, precision=lax.Precision.HIGHEST)
        b_row = jnp.dot(lf_row, upper, preferred_element_type=F32, precision=lax.Precision.HIGHEST)
        for h in range(H_ML):
            q = load(q_ref, seq_i, rows, h)
            k = load(k_ref, seq_i, rows, h)
            v = load(v_ref, seq_i, rows, h)
            bc = b_col[:, H_ML + h:H_ML + h + 1]
            ic = g_col[:, h:h + 1]
            br = b_row[H_ML + h:H_ML + h + 1, :]
            ir = g_row[h:h + 1, :]
            log_d = jnp.where(causal, bc - br + ir, -jnp.inf)
            a_loc = jnp.max(log_d, axis=-1, keepdims=True)
            s = _dot_nt(q, k) * jnp.exp(log_d - a_loc)
            num_loc = _dot(s.astype(BF16), v)
            den_loc = jnp.sum(s, axis=-1, keepdims=True)
            b_last = bc[L - 1:L, :]
            log_w = b_last - bc + ic
            w_max = jnp.max(log_w, axis=0, keepdims=True)
            ws = jnp.exp(log_w - w_max)
            kv = _dot_tn(k, (ws * v.astype(F32)).astype(BF16))
            ksum = jnp.sum(ws * k.astype(F32), axis=0, keepdims=True)
            cm = c_ref[seq_i, h]
            nr = n_ref[seq_i, h:h + 1, :]
            m = m_ref[seq_i, h:h + 1, 0:1]
            inter = bc + m
            m_t = jnp.maximum(inter, a_loc)
            e_loc = jnp.exp(a_loc - m_t)
            w_inter = jnp.exp(inter - m_t)
            num = num_loc * e_loc + w_inter * _dot(q, cm.astype(BF16))
            qn = jnp.sum(q.astype(F32) * nr, axis=-1, keepdims=True)
            den = den_loc * e_loc + w_inter * qn
            hv = num / jnp.maximum(jnp.abs(den), jnp.exp(-m_t))
            m_new = jnp.maximum(b_last + m, w_max)
            fw = jnp.exp(b_last + m - m_new)
            fu = jnp.exp(w_max - m_new)
            c_ref[seq_i, h] = fw * cm + fu * kv
            n_ref[seq_i, h:h + 1, :] = fw * nr + fu * ksum
            m_ref[seq_i, h:h + 1, :] = jnp.broadcast_to(m_new, (1, LANES))
            gate = jax.nn.sigmoid(om_ref[rows, h * LANES:(h + 1) * LANES])
            hm_ref[rows, h * LANES:(h + 1) * LANES] = (_rms(hv, gn_ref[...]) * gate).astype(hm_ref.dtype)

    for seq_i in range(n_seq):
        if n_chunks == 1:
            step(seq_i, 0)
        else:
            def body(c, carry, seq_i=seq_i):
                step(seq_i, c)
                return carry
            lax.fori_loop(0, n_chunks, body, 0, unroll=unroll)


def _mlstm(q, k, v, gates, gates_t, om, b_row, b_col, gain, state, *, batch, chunk, valid, head_major,
           n_seq=1, unroll=1):
    n, width = om.shape
    seq = n // batch
    n_chunks = seq // chunk
    dh = LANES
    if head_major:
        assert n_seq == 1
        qkv_spec = pl.BlockSpec((H_ML, seq, dh), lambda b: (0, b, 0))
    else:
        qkv_spec = pl.BlockSpec((n_seq, seq, width), lambda b: (b, 0, 0))
    row = lambda w: pl.BlockSpec((n_seq * seq, w), lambda b: (b, 0))
    full = lambda a: pl.BlockSpec(a.shape, lambda b: (0,) * a.ndim)
    c_spec = pl.BlockSpec((n_seq, H_ML, dh, dh), lambda b: (b, 0, 0, 0))
    v_spec = pl.BlockSpec((n_seq, H_ML, dh), lambda b: (b, 0, 0))
    in_specs = [qkv_spec, qkv_spec, qkv_spec, row(LANES),
                pl.BlockSpec((n_seq * n_chunks, 2 * H_ML, chunk), lambda b: (b, 0, 0)),
                row(width), full(b_row), full(b_col), full(gain)]
    args = [q, k, v, gates, gates_t, om, b_row, b_col, gain]
    if state is not None:
        in_specs += [c_spec, v_spec, v_spec]
        args += list(state)
    return pl.pallas_call(
        functools.partial(_mlstm_kernel, chunk=chunk, n_chunks=n_chunks, n_seq=n_seq, valid=valid,
                          head_major=head_major, has_state=state is not None, unroll=unroll),
        grid=(batch // n_seq,),
        in_specs=in_specs,
        out_specs=[row(width), c_spec, v_spec, v_spec],
        out_shape=[jax.ShapeDtypeStruct((n, width), BF16 if head_major else F32),
                   jax.ShapeDtypeStruct((batch, H_ML, dh, dh), F32),
                   jax.ShapeDtypeStruct((batch, H_ML, dh), F32),
                   jax.ShapeDtypeStruct((batch, H_ML, dh), F32)],
        compiler_params=_params("parallel"),
        name="mlstm_prompt" if head_major else "mlstm_sample",
    )(*args)


def kernel(x_prompt, x_sample, cache_k, cache_v, state_C, state_n, state_m, page_table, rel_bias, norm_gains,
           ffn_w_gate, ffn_w_up, ffn_w_down, w_in, b_gates, lam_q, lam_k, attn_norm, mlstm_norm, w_out):
    batch, seq, d = x_prompt.shape
    dec_batch, dec_seq, _ = x_sample.shape
    depth = w_in.shape[0]
    page = cache_k.shape[2]
    width = H_ATT * LANES
    d_in = w_in.shape[-1]
    n_gate = 2 * H_ML
    assert d_in == 7 * width + n_gate and dec_seq <= SAMPLE_ROWS
    assert seq % ATT_TILE == 0 and seq % MLSTM_CHUNK == 0 and (batch * seq) % TOKEN_TILE == 0

    xp = x_prompt.reshape(batch * seq, d)
    xs = jnp.pad(x_sample, ((0, 0), (0, SAMPLE_ROWS - dec_seq), (0, 0))).reshape(dec_batch * SAMPLE_ROWS, d)

    prompt_bias = _prompt_bias(rel_bias, ATT_TILE)
    sample_bias = _sample_bias(rel_bias, page, dec_seq)

    outs = [[] for _ in range(10)]
    for l in range(depth):
        lam_init = 0.8 - 0.6 * math.exp(-0.3 * l)
        g = norm_gains[l]
        f = ffn_w_gate.shape[-1]
        wgu = _fuse_gate_up(ffn_w_gate.reshape(-1, d, f), ffn_w_up.reshape(-1, d, f), 2 * l, 2, WEIGHT_ROW_TILE)
        wd = _cast_groups(ffn_w_down.reshape(-1, f, d), 2 * l, 2, d, f // 4)
        w_in_p = _cast_groups(w_in, l, 1, d_in - n_gate + LANES, WEIGHT_ROW_TILE)
        wo = _cast_groups(w_out, l, 1, d, WEIGHT_ROW_TILE)
        b_row = jnp.pad(b_gates[l].reshape(1, n_gate), ((0, 0), (0, LANES - n_gate)))
        b_col = b_gates[l].reshape(n_gate, 1)
        gn_att = attn_norm[l].reshape(1, -1)
        gn_ml = mlstm_norm[l].reshape(1, -1)
        ck = cache_k.reshape(-1, LANES)
        cv = cache_v.reshape(-1, LANES)
        first_page = l * cache_k.shape[1]

        xp = _ffn(xp, g, (None, 0, 1), wgu, wd, 0)
        xs = _ffn(xs, g, (None, 0, 1), wgu, wd, 0)

        (qa, kaf, vaf, ka, va, qm, km, vm, om, gt, gtt) = _proj(xp, g[2:3], w_in_p, True, MLSTM_CHUNK)
        mix_a = _prompt_attention(qa, ka, va, prompt_bias, lam_q[l], lam_k[l], gn_att, batch, lam_init)
        mix_m, c_p, n_p, m_p = _mlstm(qm, km, vm, gt, gtt, om, b_row, b_col, gn_ml, None,
                                      batch=batch, chunk=MLSTM_CHUNK, valid=MLSTM_CHUNK, head_major=True,
                                      unroll=MLSTM_UNROLL)
        xp = _ffn(xp, g, (3, 4, 5), wgu, wd, 1, mix=(mix_a, mix_m, wo))

        (qa_s, kaf_s, vaf_s, _, _, qm_s, km_s, vm_s, om_s, gt_s, gtt_s) = _proj(xs, g[2:3], w_in_p, False,
                                                                                SAMPLE_ROWS)
        seq3 = lambda a: a.reshape(dec_batch, SAMPLE_ROWS, width)
        mix_a_s = _sample_attention(page_table, seq3(qa_s), kaf_s, vaf_s, ck, cv, page, first_page, sample_bias,
                                    lam_q[l], lam_k[l], gn_att, lam_init)
        m0 = jnp.broadcast_to(state_m[l][:, :, None], (dec_batch, H_ML, LANES))
        mix_m_s, c_s, n_s, m_s = _mlstm(seq3(qm_s), seq3(km_s), seq3(vm_s), gt_s, gtt_s, om_s, b_row, b_col, gn_ml,
                                        (state_C[l], state_n[l], m0),
                                        batch=dec_batch, chunk=SAMPLE_ROWS, valid=dec_seq, head_major=False,
                                        n_seq=math.gcd(dec_batch, MLSTM_SAMPLE_SEQS))
        xs = _ffn(xs, g, (3, 4, 5), wgu, wd, 1, mix=(mix_a_s.reshape(-1, width), mix_m_s, wo))

        new = lambda a: a.reshape(dec_batch, SAMPLE_ROWS, H_ATT, LANES)[:, :dec_seq]
        for lst, val in zip(outs, (kaf.reshape(batch, seq, H_ATT, LANES), vaf.reshape(batch, seq, H_ATT, LANES),
                                   c_p, n_p, m_p[:, :, 0], new(kaf_s), new(vaf_s), c_s, n_s, m_s[:, :, 0])):
            lst.append(val)

    y_prompt = xp.reshape(batch, seq, d)
    y_sample = xs.reshape(dec_batch, SAMPLE_ROWS, d)[:, :dec_seq]
    return (y_prompt, y_sample) + tuple(jnp.stack(o) for o in outs)
```

```python
import functools
import math

import jax
import jax.numpy as jnp
from jax import lax
from jax.experimental import pallas as pl
from jax.experimental.pallas import tpu as pltpu

F32 = jnp.float32
BF16 = jnp.bfloat16

H_ATT = 4
ATT_DQ = 64
H_ML = 4
NUM_BUCKETS = 32
MAX_DISTANCE = 128
EPS = 1e-6

LANES = 128
SUBLANES = 8
VMEM_LIMIT_BYTES = 56 * 1024 * 1024
FUSED_VMEM_LIMIT_BYTES = 60 * 1024 * 1024

TOKEN_TILE = 512
FFN_CHUNK = 256
ATT_TILE = 256
SAMPLE_ROWS = 8
WEIGHT_ROW_TILE = 256
MLSTM_CHUNK = 256
MLSTM_UNROLL = 1
MLSTM_SAMPLE_SEQS = 8


def _params(*sem):
    return pltpu.CompilerParams(dimension_semantics=sem, vmem_limit_bytes=VMEM_LIMIT_BYTES)


def _rms(x, g):
    return x * lax.rsqrt(jnp.mean(x * x, axis=-1, keepdims=True) + EPS) * g


def _dot(a, b):
    return jnp.dot(a, b, preferred_element_type=F32)


def _dot_nt(a, b):
    return lax.dot_general(a, b, (((1,), (1,)), ((), ())), preferred_element_type=F32)


def _dot_tn(a, b):
    return lax.dot_general(a, b, (((0,), (0,)), ((), ())), preferred_element_type=F32)


def _resident(arr, lead=None):
    if lead is None:
        return pl.BlockSpec(arr.shape, lambda *_: (0,) * arr.ndim, pipeline_mode=pl.Buffered(1))
    return pl.BlockSpec((None,) + arr.shape[1:], lambda *_: (lead,) + (0,) * (arr.ndim - 1),
                        pipeline_mode=pl.Buffered(1))


def _cast_kernel(x_ref, o_ref):
    w = x_ref.shape[-1]
    if o_ref.shape[-1] != w:
        o_ref[...] = jnp.zeros(o_ref.shape, o_ref.dtype)
    o_ref[:, :w] = x_ref[...].astype(o_ref.dtype)


def _cast_groups(x3, first, count, cols_out, row_tile):
    _, rows, cols = x3.shape
    return pl.pallas_call(
        _cast_kernel,
        grid=(count, rows // row_tile),
        in_specs=[pl.BlockSpec((None, row_tile, cols), lambda g, r: (first + g, r, 0))],
        out_specs=pl.BlockSpec((None, row_tile, cols_out), lambda g, r: (g, r, 0)),
        out_shape=jax.ShapeDtypeStruct((count, rows, cols_out), BF16),
        compiler_params=_params("parallel", "parallel"),
        name="cast_weights",
    )(x3)


def _gate_up_kernel(g_ref, u_ref, o_ref):
    for c in range(g_ref.shape[-1] // FFN_CHUNK):
        src = slice(c * FFN_CHUNK, (c + 1) * FFN_CHUNK)
        o_ref[:, 2 * c * FFN_CHUNK:(2 * c + 1) * FFN_CHUNK] = g_ref[:, src].astype(BF16)
        o_ref[:, (2 * c + 1) * FFN_CHUNK:(2 * c + 2) * FFN_CHUNK] = u_ref[:, src].astype(BF16)


def _fuse_gate_up(wg3, wu3, first, count, row_tile):
    _, rows, cols = wg3.shape
    spec = pl.BlockSpec((None, row_tile, cols), lambda g, r: (first + g, r, 0))
    return pl.pallas_call(
        _gate_up_kernel,
        grid=(count, rows // row_tile),
        in_specs=[spec, spec],
        out_specs=pl.BlockSpec((None, row_tile, 2 * cols), lambda g, r: (g, r, 0)),
        out_shape=jax.ShapeDtypeStruct((count, rows, 2 * cols), BF16),
        compiler_params=_params("parallel", "parallel"),
        name="fuse_gate_up",
    )(wg3, wu3)


def _ffn_kernel(*refs, with_mix, n_chunks, gain_rows):
    mix_row, pre_row, post_row = gain_rows
    if with_mix:
        (x_ref, ma_ref, mm_ref, wo_ref, g_ref, wgu_ref, wd_ref, o_ref, u_ref) = refs
        half = ma_ref.shape[-1]
        y = _dot(ma_ref[...].astype(BF16), wo_ref[:half, :]) + _dot(mm_ref[...].astype(BF16), wo_ref[half:, :])
        x = x_ref[...] + _rms(y, g_ref[mix_row:mix_row + 1, :])
    else:
        (x_ref, g_ref, wgu_ref, wd_ref, o_ref, u_ref) = refs
        x = x_ref[...]
    h = _rms(x, g_ref[pre_row:pre_row + 1, :]).astype(BF16)
    for c in range(n_chunks):
        gu = _dot(h, wgu_ref[:, 2 * c * FFN_CHUNK:2 * (c + 1) * FFN_CHUNK])
        gate = gu[:, :FFN_CHUNK]
        up = gu[:, FFN_CHUNK:]
        u_ref[:, c * FFN_CHUNK:(c + 1) * FFN_CHUNK] = (gate * jax.nn.sigmoid(gate) * up).astype(BF16)
    y = _dot(u_ref[...], wd_ref[...])
    o_ref[...] = x + 0.5 * _rms(y, g_ref[post_row:post_row + 1, :])


def _ffn(x, gains, gain_rows, wgu, wd, which, mix=None):
    n, d = x.shape
    f = wd.shape[1]
    n_chunks = f // FFN_CHUNK
    tm = TOKEN_TILE
    row = lambda w: pl.BlockSpec((tm, w), lambda i: (i, 0))
    in_specs = [row(d)]
    args = [x]
    if mix is not None:
        ma, mm, wo = mix
        in_specs += [row(ma.shape[1]), row(mm.shape[1]), _resident(wo, 0)]
        args += [ma, mm, wo]
    in_specs += [_resident(gains), _resident(wgu, which), _resident(wd, which)]
    args += [gains, wgu, wd]
    return pl.pallas_call(
        functools.partial(_ffn_kernel, with_mix=mix is not None, n_chunks=n_chunks, gain_rows=gain_rows),
        grid=(n // tm,),
        in_specs=in_specs,
        out_specs=row(d),
        out_shape=jax.ShapeDtypeStruct((n, d), F32),
        scratch_shapes=[pltpu.VMEM((tm, f), BF16)],
        compiler_params=_params("parallel"),
        name="ffn_mix" if mix is not None else "ffn",
    )(*args)


def _proj_kernel(x_ref, g_ref, w_ref, *outs, head_major, width, gate_chunk):
    (qa_ref, kaf_ref, vaf_ref, ka_ref, va_ref, qm_ref, km_ref, vm_ref, om_ref, gt_ref, gtt_ref) = outs
    xn = _rms(x_ref[...], g_ref[...]).astype(BF16)
    tm = x_ref.shape[0]
    n_heads = width // LANES

    def group(i):
        return _dot(xn, w_ref[:, i * width:(i + 1) * width])

    def put(ref, val):
        if head_major:
            for h in range(n_heads):
                ref[h] = val[:, h * LANES:(h + 1) * LANES].astype(ref.dtype)
        else:
            ref[...] = val.astype(ref.dtype)

    def put_rows(ref, val):
        for h in range(n_heads):
            ref[pl.ds(h, tm, stride=n_heads), :] = val[:, h * LANES:(h + 1) * LANES]

    put(qa_ref, group(0) * (ATT_DQ ** -0.5))
    ka = group(1)
    put_rows(kaf_ref, ka)
    put(ka_ref, ka)
    va = group(2)
    put_rows(vaf_ref, va)
    put(va_ref, va)
    put(qm_ref, group(3))
    put(km_ref, group(4) * (LANES ** -0.5))
    put(vm_ref, group(5))
    om_ref[...] = group(6)
    gates = _dot(xn, w_ref[:, 7 * width:7 * width + LANES])
    gt_ref[...] = gates
    gates_t = gates.T[:2 * H_ML, :]
    for c in range(tm // gate_chunk):
        gtt_ref[c] = gates_t[:, c * gate_chunk:(c + 1) * gate_chunk]


def _proj(x, gain, w_in_p, head_major, gate_chunk):
    n, d = x.shape
    width = H_ATT * LANES
    n_heads = width // LANES
    tm = TOKEN_TILE
    row = lambda w: pl.BlockSpec((tm, w), lambda i: (i, 0))
    if head_major:
        hm_spec = pl.BlockSpec((n_heads, tm, LANES), lambda i: (0, i, 0))
        hm_shape = jax.ShapeDtypeStruct((n_heads, n, LANES), BF16)
    else:
        hm_spec = row(width)
        hm_shape = jax.ShapeDtypeStruct((n, width), F32)
    f32_shape = jax.ShapeDtypeStruct((n, width), F32)
    rows_spec = pl.BlockSpec((tm * n_heads, LANES), lambda i: (i, 0))
    rows_shape = jax.ShapeDtypeStruct((n * n_heads, LANES), F32)
    out_specs = [hm_spec, rows_spec, rows_spec, hm_spec, hm_spec, hm_spec, hm_spec, hm_spec, row(width),
                 row(LANES), pl.BlockSpec((tm // gate_chunk, 2 * H_ML, gate_chunk), lambda i: (i, 0, 0))]
    out_shape = [hm_shape, rows_shape, rows_shape, hm_shape, hm_shape, hm_shape, hm_shape, hm_shape, f32_shape,
                 jax.ShapeDtypeStruct((n, LANES), F32),
                 jax.ShapeDtypeStruct((n // gate_chunk, 2 * H_ML, gate_chunk), F32)]
    return pl.pallas_call(
        functools.partial(_proj_kernel, head_major=head_major, width=width, gate_chunk=gate_chunk),
        grid=(n // tm,),
        in_specs=[row(d), _resident(gain), _resident(w_in_p, 0)],
        out_specs=out_specs,
        out_shape=out_shape,
        compiler_params=_params("parallel"),
        name="proj_prompt" if head_major else "proj_sample",
    )(x, gain, w_in_p)


def _t5_bias(rel, rb_ref, h):
    n = jnp.maximum(rel, 0)
    max_exact = NUM_BUCKETS // 2
    large = max_exact + (jnp.log(jnp.maximum(n, max_exact).astype(F32) / max_exact)
                         / math.log(MAX_DISTANCE / max_exact) * (NUM_BUCKETS - max_exact)).astype(jnp.int32)
    large = jnp.minimum(large, NUM_BUCKETS - 1)
    bucket = jnp.where(n < max_exact, n, large)
    out = jnp.zeros(rel.shape, F32)
    for i in range(NUM_BUCKETS):
        out = jnp.where(bucket == i, rb_ref[i, h], out)
    return out - rb_ref[NUM_BUCKETS - 1, h]


def _prompt_bias_kernel(rb_ref, o_ref, *, tile):
    r = lax.broadcasted_iota(jnp.int32, (tile, tile), 0)
    c = lax.broadcasted_iota(jnp.int32, (tile, tile), 1)
    for h in range(H_ATT):
        o_ref[h, 0] = jnp.where(c <= r, _t5_bias(r - c, rb_ref, h), -jnp.inf)
        o_ref[h, 1] = _t5_bias(tile + r - c, rb_ref, h)


def _prompt_bias(rel_bias, tile):
    return pl.pallas_call(
        functools.partial(_prompt_bias_kernel, tile=tile),
        in_specs=[pl.BlockSpec(memory_space=pltpu.SMEM)],
        out_shape=jax.ShapeDtypeStruct((H_ATT, 2, tile, tile), F32),
        name="prompt_bias",
    )(rel_bias)


def _sample_bias_kernel(rb_ref, o_ref, *, page, valid):
    qi = lax.broadcasted_iota(jnp.int32, (SUBLANES, 2 * page), 0)
    col = lax.broadcasted_iota(jnp.int32, (SUBLANES, 2 * page), 1)
    kj = col - page
    for h in range(H_ATT):
        past = _t5_bias(page + qi - col, rb_ref, h)
        new = jnp.where((kj <= qi) & (kj < valid), _t5_bias(qi - kj, rb_ref, h), -jnp.inf)
        tile = jnp.where(col < page, past, new)
        for comp in range(2):
            o_ref[(comp * H_ATT + h) * SUBLANES:(comp * H_ATT + h + 1) * SUBLANES, :] = tile


def _sample_bias(rel_bias, page, valid):
    return pl.pallas_call(
        functools.partial(_sample_bias_kernel, page=page, valid=valid),
        in_specs=[pl.BlockSpec(memory_space=pltpu.SMEM)],
        out_shape=jax.ShapeDtypeStruct((2 * H_ATT * SUBLANES, 2 * page), F32),
        name="sample_bias",
    )(rel_bias)


def _lambda(lq_ref, lk_ref, lam_init):
    prod = lq_ref[...] * lk_ref[...]
    return (jnp.exp(jnp.sum(prod[0:1, :], axis=-1, keepdims=True))
            - jnp.exp(jnp.sum(prod[1:2, :], axis=-1, keepdims=True)) + lam_init)


def _attn_kernel(q_ref, k_ref, v_ref, bias_ref, lq_ref, lk_ref, gn_ref, o_ref, vx_scr, *, tile, n_q, lam_init):
    dv = v_ref.shape[-1]
    vx_scr[:, :dv] = v_ref[0]
    ones_lane = lax.broadcasted_iota(jnp.int32, (vx_scr.shape[0], vx_scr.shape[1] - dv), 1)
    vx_scr[:, dv:] = jnp.where(ones_lane == 0, 1.0, 0.0).astype(BF16)

    lane = lax.broadcasted_iota(jnp.int32, (tile, dv), 1)
    zero = jnp.zeros((tile, dv), BF16)
    lam = _lambda(lq_ref, lk_ref, lam_init)
    diag = jnp.concatenate([bias_ref[0, 0]] * 2, axis=0)
    near = jnp.concatenate([bias_ref[0, 1]] * 2, axis=0)

    def logits(n):
        q = q_ref[0, n * tile:(n + 1) * tile, :]
        qq = jnp.concatenate([jnp.where(lane < ATT_DQ, q, zero), jnp.where(lane >= ATT_DQ, q, zero)], axis=0)
        n_keys = (n + 1) * tile
        s = _dot_nt(qq, k_ref[0, :n_keys, :])
        parts = [s[:, n_keys - tile:] + diag]
        if n >= 1:
            parts.insert(0, s[:, n_keys - 2 * tile:n_keys - tile] + near)
        if n >= 2:
            parts.insert(0, s[:, :n_keys - 2 * tile])
        return parts

    def probs(parts):
        m = functools.reduce(jnp.maximum, [jnp.max(x, axis=-1, keepdims=True) for x in parts])
        return jnp.concatenate([jnp.exp(x - m) for x in parts], axis=1).astype(BF16)

    def finish(n, p):
        vx = vx_scr[:(n + 1) * tile, :]
        acc0 = _dot(p[:tile], vx)
        acc1 = _dot(p[tile:], vx)
        out = acc0[:, :dv] / acc0[:, dv:dv + 1] - lam * (acc1[:, :dv] / acc1[:, dv:dv + 1])
        o_ref[n * tile:(n + 1) * tile, :] = (_rms(out, gn_ref[...]) * (1.0 - lam_init)).astype(o_ref.dtype)

    for g in range(n_q // 2):
        pair = (g, n_q - 1 - g)
        p_pair = [probs(logits(n)) for n in pair]
        for n, p in zip(pair, p_pair):
            finish(n, p)


def _prompt_attention(q, k, v, bias, lam_q, lam_k, gain, batch, lam_init):
    h, n, dv = q.shape
    seq = n // batch
    tile = ATT_TILE
    nq = seq // tile
    assert nq % 2 == 0
    seq_spec = pl.BlockSpec((1, seq, dv), lambda b, hh: (hh, b, 0))
    return pl.pallas_call(
        functools.partial(_attn_kernel, tile=tile, n_q=nq, lam_init=lam_init),
        grid=(batch, h),
        in_specs=[seq_spec, seq_spec, seq_spec,
                  pl.BlockSpec((1, 2, tile, tile), lambda b, hh: (hh, 0, 0, 0)),
                  pl.BlockSpec(lam_q.shape, lambda b, hh: (0, 0)),
                  pl.BlockSpec(lam_k.shape, lambda b, hh: (0, 0)),
                  pl.BlockSpec(gain.shape, lambda b, hh: (0, 0))],
        out_specs=pl.BlockSpec((seq, dv), lambda b, hh: (b, hh)),
        out_shape=jax.ShapeDtypeStruct((n, h * dv), BF16),
        scratch_shapes=[pltpu.VMEM((seq, 2 * dv), BF16)],
        compiler_params=_params("parallel", "parallel"),
        name="attn_prompt",
    )(q, k, v, bias, lam_q, lam_k, gain)


def _sample_attn_core(q, load_k, load_v, load_kn, load_vn, bias_ref, lam, gn_ref, k_buf, v_buf,
                      *, n_pages, page, n_tok_new, lam_init):
    width = q.shape[-1]
    n_rows = 2 * H_ATT * SUBLANES
    qt = jnp.concatenate([q] * (2 * H_ATT), axis=0)
    r = lax.broadcasted_iota(jnp.int32, (n_rows, width), 0)
    c = lax.broadcasted_iota(jnp.int32, (n_rows, width), 1)
    comp = r // (H_ATT * SUBLANES)
    head = (r // SUBLANES) % H_ATT
    own = (c // LANES == head) & ((c // ATT_DQ) % 2 == comp)
    qbd = jnp.where(own, qt, 0.0).astype(BF16)

    n_past = n_pages * page
    new_rows = 2 * SUBLANES
    for load, load_new, buf in ((load_k, load_kn, k_buf), (load_v, load_vn, v_buf)):
        for p in range(n_pages):
            for h in range(H_ATT):
                buf[p * page:(p + 1) * page, h * LANES:(h + 1) * LANES] = load(p, h).astype(BF16)
        new = jnp.concatenate([load_new(h) for h in range(H_ATT)], axis=1)
        new = jnp.concatenate([new, jnp.zeros((new_rows - n_tok_new, width), F32)], axis=0)
        buf[n_past:n_past + new_rows, :] = new.astype(BF16)
        buf[n_past + new_rows:, :] = jnp.zeros((page - new_rows, width), BF16)

    s = _dot_nt(qbd, k_buf[...])
    parts = [s[:, :n_past - page], s[:, n_past - page:] + bias_ref[...]]
    m = functools.reduce(jnp.maximum, [jnp.max(x, axis=-1, keepdims=True) for x in parts])
    p = jnp.concatenate([jnp.exp(x - m) for x in parts], axis=1)
    inv = 1.0 / jnp.sum(p, axis=-1, keepdims=True)
    half = n_rows // 2
    a = (p[:half] * inv[:half] - lam * (p[half:] * inv[half:])).astype(BF16)
    out = _dot(a, v_buf[...])
    heads = [_rms(out[h * SUBLANES:(h + 1) * SUBLANES, h * LANES:(h + 1) * LANES], gn_ref[...])
             * (1.0 - lam_init) for h in range(H_ATT)]
    return jnp.concatenate(heads, axis=1)


def _ffn_attn_kernel(pt_ref, x_ref, g_ref, wgu_ref, wd_ref, q_ref, kn_ref, vn_ref, bias_ref, lq_ref, lk_ref,
                     gn_ref, ck_hbm, cv_hbm, o_ref, ao_ref, u_ref, k_ring, v_ring, k_buf, v_buf, sem,
                     *, n_chunks, gain_rows, n_pages, page, first_page, n_seq, lam_init):
    _, pre_row, post_row = gain_rows
    i = pl.program_id(0)
    seqs = q_ref.shape[0]
    page_rows = page * H_ATT
    new_rows = kn_ref.shape[0] // seqs
    n_tok_new = new_rows // H_ATT

    def page_copies(g, slot):
        copies = []
        for p in range(n_pages):
            src = pl.ds(pl.multiple_of((first_page + pt_ref[g, p]) * page_rows, page_rows), page_rows)
            dst = pl.ds(p * page_rows, page_rows)
            copies.append(pltpu.make_async_copy(ck_hbm.at[src, :], k_ring.at[slot, dst, :], sem.at[slot, 0]))
            copies.append(pltpu.make_async_copy(cv_hbm.at[src, :], v_ring.at[slot, dst, :], sem.at[slot, 1]))
        return copies

    @pl.when(i == 0)
    def _():
        for cp in page_copies(0, 0):
            cp.start()

    x = x_ref[...]
    h = _rms(x, g_ref[pre_row:pre_row + 1, :]).astype(BF16)
    lam = _lambda(lq_ref, lk_ref, lam_init)
    chunk_bounds = [(n_chunks * j) // seqs for j in range(seqs + 1)]
    for j in range(seqs):
        g = i * seqs + j
        slot = j % 2

        for cp in page_copies(jnp.minimum(g + 1, n_seq - 1), 1 - slot):
            cp.start()

        for c in range(chunk_bounds[j], chunk_bounds[j + 1]):
            gu = _dot(h, wgu_ref[:, 2 * c * FFN_CHUNK:2 * (c + 1) * FFN_CHUNK])
            gate = gu[:, :FFN_CHUNK]
            up = gu[:, FFN_CHUNK:]
            u_ref[:, c * FFN_CHUNK:(c + 1) * FFN_CHUNK] = (gate * jax.nn.sigmoid(gate) * up).astype(BF16)

        for cp in page_copies(g, slot):
            cp.wait()

        def past(ring, p, hh, slot=slot):
            return ring[slot, pl.ds(p * page_rows + hh, page, stride=H_ATT), :]

        def new(ref, hh, j=j):
            return ref[pl.ds(j * new_rows + hh, n_tok_new, stride=H_ATT), :]

        ao_ref[j] = _sample_attn_core(
            q_ref[j], functools.partial(past, k_ring), functools.partial(past, v_ring),
            functools.partial(new, kn_ref), functools.partial(new, vn_ref), bias_ref, lam, gn_ref, k_buf, v_buf,
            n_pages=n_pages, page=page, n_tok_new=n_tok_new, lam_init=lam_init)

    y = _dot(u_ref[...], wd_ref[...])
    o_ref[...] = x + 0.5 * _rms(y, g_ref[post_row:post_row + 1, :])

    @pl.when(i == pl.num_programs(0) - 1)
    def _():
        for cp in page_copies(n_seq - 1, seqs % 2):
            cp.wait()


def _ffn_with_sample_attention(x, gains, gain_rows, wgu, wd, which, page_table, q, k_new, v_new, cache_k, cache_v,
                               page, first_page, bias, lam_q, lam_k, gain, lam_init):
    n, d = x.shape
    f = wd.shape[1]
    tm = TOKEN_TILE
    steps = n // tm
    nb, rows, width = q.shape
    n_pages = page_table.shape[1]
    n_heads = width // LANES
    seqs = nb // steps
    assert nb == seqs * steps and seqs % 2 == 0
    ring = pltpu.VMEM((2, n_pages * page * n_heads, LANES), F32)
    stage = pltpu.VMEM(((n_pages + 1) * page, width), BF16)
    full = lambda a: pl.BlockSpec(a.shape, lambda i, pt: (0,) * a.ndim)
    seq_spec = pl.BlockSpec((seqs, rows, width), lambda i, pt: (i, 0, 0))
    new_spec = pl.BlockSpec((seqs * rows * n_heads, LANES), lambda i, pt: (i, 0))
    row_spec = pl.BlockSpec((tm, d), lambda i, pt: (i, 0))
    grid_spec = pltpu.PrefetchScalarGridSpec(
        num_scalar_prefetch=1,
        grid=(steps,),
        in_specs=[row_spec, _resident(gains), _resident(wgu, which), _resident(wd, which),
                  seq_spec, new_spec, new_spec, full(bias), full(lam_q), full(lam_k), full(gain),
                  pl.BlockSpec(memory_space=pl.ANY), pl.BlockSpec(memory_space=pl.ANY)],
        out_specs=[row_spec, seq_spec],
        scratch_shapes=[pltpu.VMEM((tm, f), BF16), ring, ring, stage, stage, pltpu.SemaphoreType.DMA((2, 2))],
    )
    return pl.pallas_call(
        functools.partial(_ffn_attn_kernel, n_chunks=f // FFN_CHUNK, gain_rows=gain_rows, n_pages=n_pages,
                          page=page, first_page=first_page, n_seq=nb, lam_init=lam_init),
        grid_spec=grid_spec,
        out_shape=[jax.ShapeDtypeStruct((n, d), F32), jax.ShapeDtypeStruct((nb, rows, width), F32)],
        compiler_params=pltpu.CompilerParams(dimension_semantics=("arbitrary",),
                                             vmem_limit_bytes=FUSED_VMEM_LIMIT_BYTES),
        name="ffn_attn_sample",
    )(page_table, x, gains, wgu, wd, q, k_new, v_new, bias, lam_q, lam_k, gain, cache_k, cache_v)


def _log_sigmoid(x):
    return jnp.minimum(x, 0.0) - jnp.log1p(jnp.exp(-jnp.abs(x)))


def _mlstm_kernel(*refs, chunk, n_chunks, n_seq, valid, head_major, has_state, unroll):
    refs = list(refs)
    q_ref, k_ref, v_ref, g_ref, gt_ref, om_ref, brow_ref, bcol_ref, gn_ref = refs[:9]
    refs = refs[9:]
    if has_state:
        c0_ref, n0_ref, m0_ref = refs[:3]
        refs = refs[3:]
    hm_ref, c_ref, n_ref, m_ref = refs
    L = chunk

    if has_state:
        c_ref[...] = c0_ref[...]
        n_ref[...] = n0_ref[...]
        m_ref[...] = m0_ref[...]
    else:
        c_ref[...] = jnp.zeros(c_ref.shape, F32)
        n_ref[...] = jnp.zeros(n_ref.shape, F32)
        m_ref[...] = jnp.zeros(m_ref.shape, F32)

    t_i = lax.broadcasted_iota(jnp.int32, (L, L), 0)
    s_i = lax.broadcasted_iota(jnp.int32, (L, L), 1)
    causal = s_i <= t_i
    lower = causal.astype(F32)
    upper = (t_i <= s_i).astype(F32)

    def load(ref, seq_i, rows, h):
        if head_major:
            return ref[h, rows, :]
        return ref[seq_i, :, h * LANES:(h + 1) * LANES].astype(BF16)

    def step(seq_i, c):
        row0 = seq_i * (n_chunks * L) + c * L
        rows = pl.ds(row0 if isinstance(row0, int) else pl.multiple_of(row0, L), L)
        g_col = g_ref[rows, :] + brow_ref[...]
        g_row = gt_ref[seq_i * n_chunks + c] + bcol_ref[...]
        lf_col = _log_sigmoid(g_col)
        lf_row = _log_sigmoid(g_row)
        if valid < L:
            tok_c = lax.broadcasted_iota(jnp.int32, g_col.shape, 0)
            tok_r = lax.broadcasted_iota(jnp.int32, g_row.shape, 1)
            g_col = jnp.where(tok_c < valid, g_col, -1e30)
            g_row = jnp.where(tok_r < valid, g_row, -1e30)
            lf_col = jnp.where(tok_c < valid, lf_col, 0.0)
            lf_row = jnp.where(tok_r < valid, lf_row, 0.0)
        b_col = jnp.dot(lower, lf_col, preferred_element_type=F32, precision=lax.Precision.HIGHEST)
        b_row = jnp.dot(lf_row, upper, preferred_element_type=F32, precision=lax.Precision.HIGHEST)
        for h in range(H_ML):
            q = load(q_ref, seq_i, rows, h)
            k = load(k_ref, seq_i, rows, h)
            v = load(v_ref, seq_i, rows, h)
            bc = b_col[:, H_ML + h:H_ML + h + 1]
            ic = g_col[:, h:h + 1]
            br = b_row[H_ML + h:H_ML + h + 1, :]
            ir = g_row[h:h + 1, :]
            log_d = jnp.where(causal, bc - br + ir, -jnp.inf)
            a_loc = jnp.max(log_d, axis=-1, keepdims=True)
            s = _dot_nt(q, k) * jnp.exp(log_d - a_loc)
            num_loc = _dot(s.astype(BF16), v)
            den_loc = jnp.sum(s, axis=-1, keepdims=True)
            b_last = bc[L - 1:L, :]
            log_w = b_last - bc + ic
            w_max = jnp.max(log_w, axis=0, keepdims=True)
            ws = jnp.exp(log_w - w_max)
            kv = _dot_tn(k, (ws * v.astype(F32)).astype(BF16))
            ksum = jnp.sum(ws * k.astype(F32), axis=0, keepdims=True)
            cm = c_ref[seq_i, h]
            nr = n_ref[seq_i, h:h + 1, :]
            m = m_ref[seq_i, h:h + 1, 0:1]
            inter = bc + m
            m_t = jnp.maximum(inter, a_loc)
            e_loc = jnp.exp(a_loc - m_t)
            w_inter = jnp.exp(inter - m_t)
            num = num_loc * e_loc + w_inter * _dot(q, cm.astype(BF16))
            qn = jnp.sum(q.astype(F32) * nr, axis=-1, keepdims=True)
            den = den_loc * e_loc + w_inter * qn
            hv = num / jnp.maximum(jnp.abs(den), jnp.exp(-m_t))
            m_new = jnp.maximum(b_last + m, w_max)
            fw = jnp.exp(b_last + m - m_new)
            fu = jnp.exp(w_max - m_new)
            c_ref[seq_i, h] = fw * cm + fu * kv
            n_ref[seq_i, h:h + 1, :] = fw * nr + fu * ksum
            m_ref[seq_i, h:h + 1, :] = jnp.broadcast_to(m_new, (1, LANES))
            gate = jax.nn.sigmoid(om_ref[rows, h * LANES:(h + 1) * LANES])
            hm_ref[rows, h * LANES:(h + 1) * LANES] = (_rms(hv, gn_ref[...]) * gate).astype(hm_ref.dtype)

    for seq_i in range(n_seq):
        if n_chunks == 1:
            step(seq_i, 0)
        else:
            def body(c, carry, seq_i=seq_i):
                step(seq_i, c)
                return carry
            lax.fori_loop(0, n_chunks, body, 0, unroll=unroll)


def _mlstm(q, k, v, gates, gates_t, om, b_row, b_col, gain, state, *, batch, chunk, valid, head_major,
           n_seq=1, unroll=1):
    n, width = om.shape
    seq = n // batch
    n_chunks = seq // chunk
    dh = LANES
    if head_major:
        assert n_seq == 1
        qkv_spec = pl.BlockSpec((H_ML, seq, dh), lambda b: (0, b, 0))
    else:
        qkv_spec = pl.BlockSpec((n_seq, seq, width), lambda b: (b, 0, 0))
    row = lambda w: pl.BlockSpec((n_seq * seq, w), lambda b: (b, 0))
    full = lambda a: pl.BlockSpec(a.shape, lambda b: (0,) * a.ndim)
    c_spec = pl.BlockSpec((n_seq, H_ML, dh, dh), lambda b: (b, 0, 0, 0))
    v_spec = pl.BlockSpec((n_seq, H_ML, dh), lambda b: (b, 0, 0))
    in_specs = [qkv_spec, qkv_spec, qkv_spec, row(LANES),
                pl.BlockSpec((n_seq * n_chunks, 2 * H_ML, chunk), lambda b: (b, 0, 0)),
                row(width), full(b_row), full(b_col), full(gain)]
    args = [q, k, v, gates, gates_t, om, b_row, b_col, gain]
    if state is not None:
        in_specs += [c_spec, v_spec, v_spec]
        args += list(state)
    return pl.pallas_call(
        functools.partial(_mlstm_kernel, chunk=chunk, n_chunks=n_chunks, n_seq=n_seq, valid=valid,
                          head_major=head_major, has_state=state is not None, unroll=unroll),
        grid=(batch // n_seq,),
        in_specs=in_specs,
        out_specs=[row(width), c_spec, v_spec, v_spec],
        out_shape=[jax.ShapeDtypeStruct((n, width), BF16 if head_major else F32),
                   jax.ShapeDtypeStruct((batch, H_ML, dh, dh), F32),
                   jax.ShapeDtypeStruct((batch, H_ML, dh), F32),
                   jax.ShapeDtypeStruct((batch, H_ML, dh), F32)],
        compiler_params=_params("parallel"),
        name="mlstm_prompt" if head_major else "mlstm_sample",
    )(*args)


def kernel(x_prompt, x_sample, cache_k, cache_v, state_C, state_n, state_m, page_table, rel_bias, norm_gains,
           ffn_w_gate, ffn_w_up, ffn_w_down, w_in, b_gates, lam_q, lam_k, attn_norm, mlstm_norm, w_out):
    batch, seq, d = x_prompt.shape
    dec_batch, dec_seq, _ = x_sample.shape
    depth = w_in.shape[0]
    page = cache_k.shape[2]
    width = H_ATT * LANES
    d_in = w_in.shape[-1]
    n_gate = 2 * H_ML
    assert d_in == 7 * width + n_gate and dec_seq <= SAMPLE_ROWS
    assert seq % ATT_TILE == 0 and seq % MLSTM_CHUNK == 0 and (batch * seq) % TOKEN_TILE == 0

    xp = x_prompt.reshape(batch * seq, d)
    xs = jnp.pad(x_sample, ((0, 0), (0, SAMPLE_ROWS - dec_seq), (0, 0))).reshape(dec_batch * SAMPLE_ROWS, d)

    prompt_bias = _prompt_bias(rel_bias, ATT_TILE)
    sample_bias = _sample_bias(rel_bias, page, dec_seq)

    outs = [[] for _ in range(10)]
    for l in range(depth):
        lam_init = 0.8 - 0.6 * math.exp(-0.3 * l)
        g = norm_gains[l]
        f = ffn_w_gate.shape[-1]
        wgu = _fuse_gate_up(ffn_w_gate.reshape(-1, d, f), ffn_w_up.reshape(-1, d, f), 2 * l, 2, WEIGHT_ROW_TILE)
        wd = _cast_groups(ffn_w_down.reshape(-1, f, d), 2 * l, 2, d, f // 4)
        w_in_p = _cast_groups(w_in, l, 1, d_in - n_gate + LANES, WEIGHT_ROW_TILE)
        wo = _cast_groups(w_out, l, 1, d, WEIGHT_ROW_TILE)
        b_row = jnp.pad(b_gates[l].reshape(1, n_gate), ((0, 0), (0, LANES - n_gate)))
        b_col = b_gates[l].reshape(n_gate, 1)
        gn_att = attn_norm[l].reshape(1, -1)
        gn_ml = mlstm_norm[l].reshape(1, -1)
        ck = cache_k.reshape(-1, LANES)
        cv = cache_v.reshape(-1, LANES)
        first_page = l * cache_k.shape[1]

        xs = _ffn(xs, g, (None, 0, 1), wgu, wd, 0)
        (qa_s, kaf_s, vaf_s, _, _, qm_s, km_s, vm_s, om_s, gt_s, gtt_s) = _proj(xs, g[2:3], w_in_p, False,
                                                                                SAMPLE_ROWS)
        seq3 = lambda a: a.reshape(dec_batch, SAMPLE_ROWS, width)
        xp, mix_a_s = _ffn_with_sample_attention(xp, g, (None, 0, 1), wgu, wd, 0, page_table, seq3(qa_s), kaf_s,
                                                 vaf_s, ck, cv, page, first_page, sample_bias, lam_q[l], lam_k[l],
                                                 gn_att, lam_init)

        (qa, kaf, vaf, ka, va, qm, km, vm, om, gt, gtt) = _proj(xp, g[2:3], w_in_p, True, MLSTM_CHUNK)
        mix_a = _prompt_attention(qa, ka, va, prompt_bias, lam_q[l], lam_k[l], gn_att, batch, lam_init)
        mix_m, c_p, n_p, m_p = _mlstm(qm, km, vm, gt, gtt, om, b_row, b_col, gn_ml, None,
                                      batch=batch, chunk=MLSTM_CHUNK, valid=MLSTM_CHUNK, head_major=True,
                                      unroll=MLSTM_UNROLL)
        xp = _ffn(xp, g, (3, 4, 5), wgu, wd, 1, mix=(mix_a, mix_m, wo))

        m0 = jnp.broadcast_to(state_m[l][:, :, None], (dec_batch, H_ML, LANES))
        mix_m_s, c_s, n_s, m_s = _mlstm(seq3(qm_s), seq3(km_s), seq3(vm_s), gt_s, gtt_s, om_s, b_row, b_col, gn_ml,
                                        (state_C[l], state_n[l], m0),
                                        batch=dec_batch, chunk=SAMPLE_ROWS, valid=dec_seq, head_major=False,
                                        n_seq=math.gcd(dec_batch, MLSTM_SAMPLE_SEQS))
        xs = _ffn(xs, g, (3, 4, 5), wgu, wd, 1, mix=(mix_a_s.reshape(-1, width), mix_m_s, wo))

        new = lambda a: a.reshape(dec_batch, SAMPLE_ROWS, H_ATT, LANES)[:, :dec_seq]
        for lst, val in zip(outs, (kaf.reshape(batch, seq, H_ATT, LANES), vaf.reshape(batch, seq, H_ATT, LANES),
                                   c_p, n_p, m_p[:, :, 0], new(kaf_s), new(vaf_s), c_s, n_s, m_s[:, :, 0])):
            lst.append(val)

    y_prompt = xp.reshape(batch, seq, d)
    y_sample = xs.reshape(dec_batch, SAMPLE_ROWS, d)[:, :dec_seq]
    return (y_prompt, y_sample) + tuple(jnp.stack(o) for o in outs)
```

```python
import functools
import math

import jax
import jax.numpy as jnp
from jax import lax
from jax.experimental import pallas as pl
from jax.experimental.pallas import tpu as pltpu

F32 = jnp.float32
BF16 = jnp.bfloat16

H_ATT = 4
ATT_DQ = 64
H_ML = 4
NUM_BUCKETS = 32
MAX_DISTANCE = 128
EPS = 1e-6

LANES = 128
SUBLANES = 8
VMEM_LIMIT_BYTES = 56 * 1024 * 1024
FUSED_VMEM_LIMIT_BYTES = 60 * 1024 * 1024

TOKEN_TILE = 512
FFN_CHUNK = 256
ATT_TILE = 256
SAMPLE_ROWS = 8
WEIGHT_ROW_TILE = 256
MLSTM_TILE = 256
MLSTM_SAMPLE_SEQS = 16


def _params(*sem):
    return pltpu.CompilerParams(dimension_semantics=sem, vmem_limit_bytes=VMEM_LIMIT_BYTES)


def _rms(x, g):
    return x * lax.rsqrt(jnp.mean(x * x, axis=-1, keepdims=True) + EPS) * g


def _dot(a, b):
    return jnp.dot(a, b, preferred_element_type=F32)


def _dot_nt(a, b):
    return lax.dot_general(a, b, (((1,), (1,)), ((), ())), preferred_element_type=F32)


def _dot_tn(a, b):
    return lax.dot_general(a, b, (((0,), (0,)), ((), ())), preferred_element_type=F32)


def _resident(arr, lead=None):
    if lead is None:
        return pl.BlockSpec(arr.shape, lambda *_: (0,) * arr.ndim, pipeline_mode=pl.Buffered(1))
    return pl.BlockSpec((None,) + arr.shape[1:], lambda *_: (lead,) + (0,) * (arr.ndim - 1),
                        pipeline_mode=pl.Buffered(1))


def _cast_kernel(x_ref, o_ref):
    w = x_ref.shape[-1]
    if o_ref.shape[-1] != w:
        o_ref[...] = jnp.zeros(o_ref.shape, o_ref.dtype)
    o_ref[:, :w] = x_ref[...].astype(o_ref.dtype)


def _cast_groups(x3, first, count, cols_out, row_tile):
    _, rows, cols = x3.shape
    return pl.pallas_call(
        _cast_kernel,
        grid=(count, rows // row_tile),
        in_specs=[pl.BlockSpec((None, row_tile, cols), lambda g, r: (first + g, r, 0))],
        out_specs=pl.BlockSpec((None, row_tile, cols_out), lambda g, r: (g, r, 0)),
        out_shape=jax.ShapeDtypeStruct((count, rows, cols_out), BF16),
        compiler_params=_params("parallel", "parallel"),
        name="cast_weights",
    )(x3)


def _gate_up_kernel(g_ref, u_ref, o_ref):
    for c in range(g_ref.shape[-1] // FFN_CHUNK):
        src = slice(c * FFN_CHUNK, (c + 1) * FFN_CHUNK)
        o_ref[:, 2 * c * FFN_CHUNK:(2 * c + 1) * FFN_CHUNK] = g_ref[:, src].astype(BF16)
        o_ref[:, (2 * c + 1) * FFN_CHUNK:(2 * c + 2) * FFN_CHUNK] = u_ref[:, src].astype(BF16)


def _fuse_gate_up(wg3, wu3, first, count, row_tile):
    _, rows, cols = wg3.shape
    spec = pl.BlockSpec((None, row_tile, cols), lambda g, r: (first + g, r, 0))
    return pl.pallas_call(
        _gate_up_kernel,
        grid=(count, rows // row_tile),
        in_specs=[spec, spec],
        out_specs=pl.BlockSpec((None, row_tile, 2 * cols), lambda g, r: (g, r, 0)),
        out_shape=jax.ShapeDtypeStruct((count, rows, 2 * cols), BF16),
        compiler_params=_params("parallel", "parallel"),
        name="fuse_gate_up",
    )(wg3, wu3)


def _ffn_kernel(*refs, with_mix, n_chunks, gain_rows):
    mix_row, pre_row, post_row = gain_rows
    if with_mix:
        (x_ref, ma_ref, mm_ref, wo_ref, g_ref, wgu_ref, wd_ref, o_ref, u_ref) = refs
        half = ma_ref.shape[-1]
        y = _dot(ma_ref[...].astype(BF16), wo_ref[:half, :]) + _dot(mm_ref[...].astype(BF16), wo_ref[half:, :])
        x = x_ref[...] + _rms(y, g_ref[mix_row:mix_row + 1, :])
    else:
        (x_ref, g_ref, wgu_ref, wd_ref, o_ref, u_ref) = refs
        x = x_ref[...]
    h = _rms(x, g_ref[pre_row:pre_row + 1, :]).astype(BF16)
    for c in range(n_chunks):
        gu = _dot(h, wgu_ref[:, 2 * c * FFN_CHUNK:2 * (c + 1) * FFN_CHUNK])
        gate = gu[:, :FFN_CHUNK]
        up = gu[:, FFN_CHUNK:]
        u_ref[:, c * FFN_CHUNK:(c + 1) * FFN_CHUNK] = (gate * jax.nn.sigmoid(gate) * up).astype(BF16)
    y = _dot(u_ref[...], wd_ref[...])
    o_ref[...] = x + 0.5 * _rms(y, g_ref[post_row:post_row + 1, :])


def _ffn(x, gains, gain_rows, wgu, wd, which, mix=None):
    n, d = x.shape
    f = wd.shape[1]
    n_chunks = f // FFN_CHUNK
    tm = TOKEN_TILE
    row = lambda w: pl.BlockSpec((tm, w), lambda i: (i, 0))
    in_specs = [row(d)]
    args = [x]
    if mix is not None:
        ma, mm, wo = mix
        in_specs += [row(ma.shape[1]), row(mm.shape[1]), _resident(wo, 0)]
        args += [ma, mm, wo]
    in_specs += [_resident(gains), _resident(wgu, which), _resident(wd, which)]
    args += [gains, wgu, wd]
    return pl.pallas_call(
        functools.partial(_ffn_kernel, with_mix=mix is not None, n_chunks=n_chunks, gain_rows=gain_rows),
        grid=(n // tm,),
        in_specs=in_specs,
        out_specs=row(d),
        out_shape=jax.ShapeDtypeStruct((n, d), F32),
        scratch_shapes=[pltpu.VMEM((tm, f), BF16)],
        compiler_params=_params("parallel"),
        name="ffn_mix" if mix is not None else "ffn",
    )(*args)


def _proj_kernel(x_ref, g_ref, w_ref, *outs, head_major, width, gate_chunk):
    (qa_ref, kaf_ref, vaf_ref, ka_ref, va_ref, qm_ref, km_ref, vm_ref, om_ref, gt_ref, gtt_ref) = outs
    xn = _rms(x_ref[...], g_ref[...]).astype(BF16)
    tm = x_ref.shape[0]
    n_heads = width // LANES

    def group(i):
        return _dot(xn, w_ref[:, i * width:(i + 1) * width])

    def put(ref, val):
        if head_major:
            for h in range(n_heads):
                ref[h] = val[:, h * LANES:(h + 1) * LANES].astype(ref.dtype)
        else:
            ref[...] = val.astype(ref.dtype)

    def put_rows(ref, val):
        for h in range(n_heads):
            ref[pl.ds(h, tm, stride=n_heads), :] = val[:, h * LANES:(h + 1) * LANES]

    put(qa_ref, group(0) * (ATT_DQ ** -0.5))
    ka = group(1)
    put_rows(kaf_ref, ka)
    put(ka_ref, ka)
    va = group(2)
    put_rows(vaf_ref, va)
    put(va_ref, va)
    put(qm_ref, group(3))
    put(km_ref, group(4) * (LANES ** -0.5))
    put(vm_ref, group(5))
    om_ref[...] = group(6)
    gates = _dot(xn, w_ref[:, 7 * width:7 * width + LANES])
    gt_ref[...] = gates
    gates_t = gates.T[:2 * H_ML, :]
    for c in range(tm // gate_chunk):
        gtt_ref[c] = gates_t[:, c * gate_chunk:(c + 1) * gate_chunk]


def _proj(x, gain, w_in_p, head_major, gate_chunk):
    n, d = x.shape
    width = H_ATT * LANES
    n_heads = width // LANES
    tm = TOKEN_TILE
    row = lambda w: pl.BlockSpec((tm, w), lambda i: (i, 0))
    if head_major:
        hm_spec = pl.BlockSpec((n_heads, tm, LANES), lambda i: (0, i, 0))
        hm_shape = jax.ShapeDtypeStruct((n_heads, n, LANES), BF16)
    else:
        hm_spec = row(width)
        hm_shape = jax.ShapeDtypeStruct((n, width), F32)
    f32_shape = jax.ShapeDtypeStruct((n, width), F32)
    rows_spec = pl.BlockSpec((tm * n_heads, LANES), lambda i: (i, 0))
    rows_shape = jax.ShapeDtypeStruct((n * n_heads, LANES), F32)
    out_specs = [hm_spec, rows_spec, rows_spec, hm_spec, hm_spec, hm_spec, hm_spec, hm_spec, row(width),
                 row(LANES), pl.BlockSpec((tm // gate_chunk, 2 * H_ML, gate_chunk), lambda i: (i, 0, 0))]
    out_shape = [hm_shape, rows_shape, rows_shape, hm_shape, hm_shape, hm_shape, hm_shape, hm_shape, f32_shape,
                 jax.ShapeDtypeStruct((n, LANES), F32),
                 jax.ShapeDtypeStruct((n // gate_chunk, 2 * H_ML, gate_chunk), F32)]
    return pl.pallas_call(
        functools.partial(_proj_kernel, head_major=head_major, width=width, gate_chunk=gate_chunk),
        grid=(n // tm,),
        in_specs=[row(d), _resident(gain), _resident(w_in_p, 0)],
        out_specs=out_specs,
        out_shape=out_shape,
        compiler_params=_params("parallel"),
        name="proj_prompt" if head_major else "proj_sample",
    )(x, gain, w_in_p)


def _t5_bias(rel, rb_ref, h):
    n = jnp.maximum(rel, 0)
    max_exact = NUM_BUCKETS // 2
    x = (jnp.log(jnp.maximum(n, max_exact).astype(F32) / max_exact)
         / math.log(MAX_DISTANCE / max_exact) * (NUM_BUCKETS - max_exact))
    out = jnp.full(rel.shape, rb_ref[max_exact, h], F32)
    for i in range(1, NUM_BUCKETS - max_exact):
        out = jnp.where(x >= i, rb_ref[max_exact + i, h], out)
    for i in range(max_exact):
        out = jnp.where(n == i, rb_ref[i, h], out)
    return out - rb_ref[NUM_BUCKETS - 1, h]


def _prompt_bias_kernel(rb_ref, o_ref, *, tile):
    r = lax.broadcasted_iota(jnp.int32, (tile, tile), 0)
    c = lax.broadcasted_iota(jnp.int32, (tile, tile), 1)
    for h in range(H_ATT):
        o_ref[h, 0] = jnp.where(c <= r, _t5_bias(r - c, rb_ref, h), -jnp.inf)
        o_ref[h, 1] = _t5_bias(tile + r - c, rb_ref, h)


def _prompt_bias(rel_bias, tile):
    return pl.pallas_call(
        functools.partial(_prompt_bias_kernel, tile=tile),
        in_specs=[pl.BlockSpec(memory_space=pltpu.SMEM)],
        out_shape=jax.ShapeDtypeStruct((H_ATT, 2, tile, tile), F32),
        name="prompt_bias",
    )(rel_bias)


def _sample_bias_kernel(rb_ref, o_ref, *, page, valid):
    qi = lax.broadcasted_iota(jnp.int32, (SUBLANES, 2 * page), 0)
    col = lax.broadcasted_iota(jnp.int32, (SUBLANES, 2 * page), 1)
    kj = col - page
    for h in range(H_ATT):
        past = _t5_bias(page + qi - col, rb_ref, h)
        new = jnp.where((kj <= qi) & (kj < valid), _t5_bias(qi - kj, rb_ref, h), -jnp.inf)
        tile = jnp.where(col < page, past, new)
        for comp in range(2):
            o_ref[(comp * H_ATT + h) * SUBLANES:(comp * H_ATT + h + 1) * SUBLANES, :] = tile


def _sample_bias(rel_bias, page, valid):
    return pl.pallas_call(
        functools.partial(_sample_bias_kernel, page=page, valid=valid),
        in_specs=[pl.BlockSpec(memory_space=pltpu.SMEM)],
        out_shape=jax.ShapeDtypeStruct((2 * H_ATT * SUBLANES, 2 * page), F32),
        name="sample_bias",
    )(rel_bias)


def _lambda(lq_ref, lk_ref, lam_init):
    prod = lq_ref[...] * lk_ref[...]
    return (jnp.exp(jnp.sum(prod[0:1, :], axis=-1, keepdims=True))
            - jnp.exp(jnp.sum(prod[1:2, :], axis=-1, keepdims=True)) + lam_init)


def _attn_kernel(q_ref, k_ref, v_ref, bias_ref, lq_ref, lk_ref, gn_ref, o_ref, vx_scr, *, tile, n_q, lam_init):
    dv = v_ref.shape[-1]
    vx_scr[:, :dv] = v_ref[0]
    ones_lane = lax.broadcasted_iota(jnp.int32, (vx_scr.shape[0], vx_scr.shape[1] - dv), 1)
    vx_scr[:, dv:] = jnp.where(ones_lane == 0, 1.0, 0.0).astype(BF16)

    lane = lax.broadcasted_iota(jnp.int32, (tile, dv), 1)
    zero = jnp.zeros((tile, dv), BF16)
    lam = _lambda(lq_ref, lk_ref, lam_init)
    diag = jnp.concatenate([bias_ref[0, 0]] * 2, axis=0)
    near = jnp.concatenate([bias_ref[0, 1]] * 2, axis=0)

    def logits(n):
        q = q_ref[0, n * tile:(n + 1) * tile, :]
        qq = jnp.concatenate([jnp.where(lane < ATT_DQ, q, zero), jnp.where(lane >= ATT_DQ, q, zero)], axis=0)
        n_keys = (n + 1) * tile
        s = _dot_nt(qq, k_ref[0, :n_keys, :])
        parts = [s[:, n_keys - tile:] + diag]
        if n >= 1:
            parts.insert(0, s[:, n_keys - 2 * tile:n_keys - tile] + near)
        if n >= 2:
            parts.insert(0, s[:, :n_keys - 2 * tile])
        return parts

    def probs(parts):
        m = functools.reduce(jnp.maximum, [jnp.max(x, axis=-1, keepdims=True) for x in parts])
        return jnp.concatenate([jnp.exp(x - m) for x in parts], axis=1).astype(BF16)

    def finish(n, p):
        vx = vx_scr[:(n + 1) * tile, :]
        acc0 = _dot(p[:tile], vx)
        acc1 = _dot(p[tile:], vx)
        out = acc0[:, :dv] / acc0[:, dv:dv + 1] - lam * (acc1[:, :dv] / acc1[:, dv:dv + 1])
        o_ref[n * tile:(n + 1) * tile, :] = (_rms(out, gn_ref[...]) * (1.0 - lam_init)).astype(o_ref.dtype)

    for g in range(n_q // 2):
        pair = (g, n_q - 1 - g)
        p_pair = [probs(logits(n)) for n in pair]
        for n, p in zip(pair, p_pair):
            finish(n, p)


def _prompt_attention(q, k, v, bias, lam_q, lam_k, gain, batch, lam_init):
    h, n, dv = q.shape
    seq = n // batch
    tile = ATT_TILE
    nq = seq // tile
    assert nq % 2 == 0
    seq_spec = pl.BlockSpec((1, seq, dv), lambda b, hh: (hh, b, 0))
    return pl.pallas_call(
        functools.partial(_attn_kernel, tile=tile, n_q=nq, lam_init=lam_init),
        grid=(batch, h),
        in_specs=[seq_spec, seq_spec, seq_spec,
                  pl.BlockSpec((1, 2, tile, tile), lambda b, hh: (hh, 0, 0, 0)),
                  pl.BlockSpec(lam_q.shape, lambda b, hh: (0, 0)),
                  pl.BlockSpec(lam_k.shape, lambda b, hh: (0, 0)),
                  pl.BlockSpec(gain.shape, lambda b, hh: (0, 0))],
        out_specs=pl.BlockSpec((seq, dv), lambda b, hh: (b, hh)),
        out_shape=jax.ShapeDtypeStruct((n, h * dv), BF16),
        scratch_shapes=[pltpu.VMEM((seq, 2 * dv), BF16)],
        compiler_params=_params("parallel", "parallel"),
        name="attn_prompt",
    )(q, k, v, bias, lam_q, lam_k, gain)


def _sample_attn_core(q, load_k, load_v, load_kn, load_vn, bias_ref, lam, gn_ref, k_buf, v_buf,
                      *, n_pages, page, n_tok_new, lam_init):
    width = q.shape[-1]
    n_rows = 2 * H_ATT * SUBLANES
    qt = jnp.concatenate([q] * (2 * H_ATT), axis=0)
    r = lax.broadcasted_iota(jnp.int32, (n_rows, width), 0)
    c = lax.broadcasted_iota(jnp.int32, (n_rows, width), 1)
    comp = r // (H_ATT * SUBLANES)
    head = (r // SUBLANES) % H_ATT
    own = (c // LANES == head) & ((c // ATT_DQ) % 2 == comp)
    qbd = jnp.where(own, qt, 0.0).astype(BF16)

    n_past = n_pages * page
    new_rows = 2 * SUBLANES
    for load, load_new, buf in ((load_k, load_kn, k_buf), (load_v, load_vn, v_buf)):
        for p in range(n_pages):
            for h in range(H_ATT):
                buf[p * page:(p + 1) * page, h * LANES:(h + 1) * LANES] = load(p, h).astype(BF16)
        new = jnp.concatenate([load_new(h) for h in range(H_ATT)], axis=1)
        new = jnp.concatenate([new, jnp.zeros((new_rows - n_tok_new, width), F32)], axis=0)
        buf[n_past:n_past + new_rows, :] = new.astype(BF16)
        buf[n_past + new_rows:, :] = jnp.zeros((page - new_rows, width), BF16)

    s = _dot_nt(qbd, k_buf[...])
    parts = [s[:, :n_past - page], s[:, n_past - page:] + bias_ref[...]]
    m = functools.reduce(jnp.maximum, [jnp.max(x, axis=-1, keepdims=True) for x in parts])
    p = jnp.concatenate([jnp.exp(x - m) for x in parts], axis=1)
    inv = 1.0 / jnp.sum(p, axis=-1, keepdims=True)
    half = n_rows // 2
    a = (p[:half] * inv[:half] - lam * (p[half:] * inv[half:])).astype(BF16)
    out = _dot(a, v_buf[...])
    heads = [_rms(out[h * SUBLANES:(h + 1) * SUBLANES, h * LANES:(h + 1) * LANES], gn_ref[...])
             * (1.0 - lam_init) for h in range(H_ATT)]
    return jnp.concatenate(heads, axis=1)


def _ffn_attn_kernel(pt_ref, x_ref, g_ref, wgu_ref, wd_ref, q_ref, kn_ref, vn_ref, bias_ref, lq_ref, lk_ref,
                     gn_ref, ck_hbm, cv_hbm, o_ref, ao_ref, u_ref, k_ring, v_ring, k_buf, v_buf, sem,
                     *, n_chunks, gain_rows, n_pages, page, first_page, n_seq, lam_init):
    _, pre_row, post_row = gain_rows
    i = pl.program_id(0)
    seqs = q_ref.shape[0]
    page_rows = page * H_ATT
    new_rows = kn_ref.shape[0] // seqs
    n_tok_new = new_rows // H_ATT

    def page_copies(g, slot):
        copies = []
        for p in range(n_pages):
            src = pl.ds(pl.multiple_of((first_page + pt_ref[g, p]) * page_rows, page_rows), page_rows)
            dst = pl.ds(p * page_rows, page_rows)
            copies.append(pltpu.make_async_copy(ck_hbm.at[src, :], k_ring.at[slot, dst, :], sem.at[slot, 0]))
            copies.append(pltpu.make_async_copy(cv_hbm.at[src, :], v_ring.at[slot, dst, :], sem.at[slot, 1]))
        return copies

    @pl.when(i == 0)
    def _():
        for cp in page_copies(0, 0):
            cp.start()

    x = x_ref[...]
    h = _rms(x, g_ref[pre_row:pre_row + 1, :]).astype(BF16)
    lam = _lambda(lq_ref, lk_ref, lam_init)
    chunk_bounds = [(n_chunks * j) // seqs for j in range(seqs + 1)]
    for j in range(seqs):
        g = i * seqs + j
        slot = j % 2

        for cp in page_copies(jnp.minimum(g + 1, n_seq - 1), 1 - slot):
            cp.start()

        for c in range(chunk_bounds[j], chunk_bounds[j + 1]):
            gu = _dot(h, wgu_ref[:, 2 * c * FFN_CHUNK:2 * (c + 1) * FFN_CHUNK])
            gate = gu[:, :FFN_CHUNK]
            up = gu[:, FFN_CHUNK:]
            u_ref[:, c * FFN_CHUNK:(c + 1) * FFN_CHUNK] = (gate * jax.nn.sigmoid(gate) * up).astype(BF16)

        for cp in page_copies(g, slot):
            cp.wait()

        def past(ring, p, hh, slot=slot):
            return ring[slot, pl.ds(p * page_rows + hh, page, stride=H_ATT), :]

        def new(ref, hh, j=j):
            return ref[pl.ds(j * new_rows + hh, n_tok_new, stride=H_ATT), :]

        ao_ref[j] = _sample_attn_core(
            q_ref[j], functools.partial(past, k_ring), functools.partial(past, v_ring),
            functools.partial(new, kn_ref), functools.partial(new, vn_ref), bias_ref, lam, gn_ref, k_buf, v_buf,
            n_pages=n_pages, page=page, n_tok_new=n_tok_new, lam_init=lam_init)

    y = _dot(u_ref[...], wd_ref[...])
    o_ref[...] = x + 0.5 * _rms(y, g_ref[post_row:post_row + 1, :])

    @pl.when(i == pl.num_programs(0) - 1)
    def _():
        for cp in page_copies(n_seq - 1, seqs % 2):
            cp.wait()


def _ffn_with_sample_attention(x, gains, gain_rows, wgu, wd, which, page_table, q, k_new, v_new, cache_k, cache_v,
                               page, first_page, bias, lam_q, lam_k, gain, lam_init):
    n, d = x.shape
    f = wd.shape[1]
    tm = TOKEN_TILE
    steps = n // tm
    nb, rows, width = q.shape
    n_pages = page_table.shape[1]
    n_heads = width // LANES
    seqs = nb // steps
    assert nb == seqs * steps and seqs % 2 == 0
    ring = pltpu.VMEM((2, n_pages * page * n_heads, LANES), F32)
    stage = pltpu.VMEM(((n_pages + 1) * page, width), BF16)
    full = lambda a: pl.BlockSpec(a.shape, lambda i, pt: (0,) * a.ndim)
    seq_spec = pl.BlockSpec((seqs, rows, width), lambda i, pt: (i, 0, 0))
    new_spec = pl.BlockSpec((seqs * rows * n_heads, LANES), lambda i, pt: (i, 0))
    row_spec = pl.BlockSpec((tm, d), lambda i, pt: (i, 0))
    grid_spec = pltpu.PrefetchScalarGridSpec(
        num_scalar_prefetch=1,
        grid=(steps,),
        in_specs=[row_spec, _resident(gains), _resident(wgu, which), _resident(wd, which),
                  seq_spec, new_spec, new_spec, full(bias), full(lam_q), full(lam_k), full(gain),
                  pl.BlockSpec(memory_space=pl.ANY), pl.BlockSpec(memory_space=pl.ANY)],
        out_specs=[row_spec, seq_spec],
        scratch_shapes=[pltpu.VMEM((tm, f), BF16), ring, ring, stage, stage, pltpu.SemaphoreType.DMA((2, 2))],
    )
    return pl.pallas_call(
        functools.partial(_ffn_attn_kernel, n_chunks=f // FFN_CHUNK, gain_rows=gain_rows, n_pages=n_pages,
                          page=page, first_page=first_page, n_seq=nb, lam_init=lam_init),
        grid_spec=grid_spec,
        out_shape=[jax.ShapeDtypeStruct((n, d), F32), jax.ShapeDtypeStruct((nb, rows, width), F32)],
        compiler_params=pltpu.CompilerParams(dimension_semantics=("arbitrary",),
                                             vmem_limit_bytes=FUSED_VMEM_LIMIT_BYTES),
        name="ffn_attn_sample",
    )(page_table, x, gains, wgu, wd, q, k_new, v_new, bias, lam_q, lam_k, gain, cache_k, cache_v)


def _log_sigmoid(x):
    return jnp.minimum(x, 0.0) - jnp.log1p(jnp.exp(-jnp.abs(x)))


def _mlstm_step_kernel(q_ref, k_ref, v_ref, g_ref, gt_ref, om_ref, brow_ref, bcol_ref, gn_ref,
                       c0_ref, n0_ref, m0_ref, hm_ref, c_ref, n_ref, m_ref, *, rows_per_seq, valid):
    n_seq = c0_ref.shape[0]
    P = rows_per_seq
    R = n_seq * P
    r_i = lax.broadcasted_iota(jnp.int32, (R, R), 0)
    c_i = lax.broadcasted_iota(jnp.int32, (R, R), 1)
    same = (r_i // P) == (c_i // P)
    causal = same & (c_i <= r_i)
    hi = lax.Precision.HIGHEST

    g_col = g_ref[...] + brow_ref[...]
    g_row = gt_ref[0] + bcol_ref[...]
    lf_col = _log_sigmoid(g_col)
    lf_row = _log_sigmoid(g_row)
    tok_c = lax.broadcasted_iota(jnp.int32, g_col.shape, 0) % P
    tok_r = lax.broadcasted_iota(jnp.int32, g_row.shape, 1) % P
    g_col = jnp.where(tok_c < valid, g_col, -1e30)
    g_row = jnp.where(tok_r < valid, g_row, -1e30)
    lf_col = jnp.where(tok_c < valid, lf_col, 0.0)
    lf_row = jnp.where(tok_r < valid, lf_row, 0.0)
    same_f = same.astype(F32)
    b_col = jnp.dot(causal.astype(F32), lf_col, preferred_element_type=F32, precision=hi)
    b_row = jnp.dot(lf_row, (same & (r_i <= c_i)).astype(F32), preferred_element_type=F32, precision=hi)
    bl_col = jnp.dot(same_f, lf_col, preferred_element_type=F32, precision=hi)
    bl_row = jnp.dot(lf_row, same_f, preferred_element_type=F32, precision=hi)

    for h in range(H_ML):
        cols = slice(h * LANES, (h + 1) * LANES)
        q = q_ref[:, cols]
        k = k_ref[:, cols]
        v = v_ref[:, cols]
        bc = b_col[:, H_ML + h:H_ML + h + 1]
        ic = g_col[:, h:h + 1]
        br = b_row[H_ML + h:H_ML + h + 1, :]
        ir = g_row[h:h + 1, :]
        blc = bl_col[:, H_ML + h:H_ML + h + 1]
        blr = bl_row[H_ML + h:H_ML + h + 1, :]
        m = m0_ref[:, h:h + 1]

        log_d = jnp.where(causal, bc - br + ir, -jnp.inf)
        inter = bc + m
        m_t = jnp.maximum(inter, jnp.max(log_d, axis=-1, keepdims=True))
        s = _dot_nt(q.astype(BF16), k.astype(BF16)) * jnp.exp(log_d - m_t)
        w_inter = jnp.exp(inter - m_t)
        q_c = jnp.concatenate([_dot(q[j * P:(j + 1) * P].astype(BF16), c0_ref[j, h].astype(BF16))
                               for j in range(n_seq)], axis=0)
        n_rows = jnp.concatenate([jnp.broadcast_to(n0_ref[j, h:h + 1, :], (P, LANES)) for j in range(n_seq)],
                                 axis=0)
        num = _dot(s.astype(BF16), v.astype(BF16)) + w_inter * q_c
        den = jnp.sum(s, axis=-1, keepdims=True) + w_inter * jnp.sum(q * n_rows, axis=-1, keepdims=True)
        hv = num / jnp.maximum(jnp.abs(den), jnp.exp(-m_t))
        hm_ref[:, cols] = _rms(hv, gn_ref[...]) * jax.nn.sigmoid(om_ref[:, cols])

        w_max = jnp.max(jnp.where(same, blr - br + ir, -jnp.inf), axis=-1, keepdims=True)
        m_new = jnp.maximum(blc + m, w_max)
        ws = jnp.exp(blc - bc + ic - m_new)
        fw = jnp.exp(blc + m - m_new)
        wv = ws * v
        wk = ws * k
        for j in range(n_seq):
            rows = slice(j * P, (j + 1) * P)
            fw_j = fw[j * P:j * P + 1, :]
            c_ref[j, h] = fw_j * c0_ref[j, h] + _dot_tn(k[rows].astype(BF16), wv[rows].astype(BF16))
            n_ref[j, h:h + 1, :] = fw_j * n0_ref[j, h:h + 1, :] + jnp.sum(wk[rows], axis=0, keepdims=True)
            m_ref[j, h:h + 1, :] = jnp.broadcast_to(m_new[j * P:j * P + 1, :], (1, LANES))


def _mlstm_step(q, k, v, gates, gates_t, om, b_row, b_col, gain, state_c, state_n, m_rows, *, rows_per_seq,
                valid, n_seq):
    n, width = om.shape
    batch = state_c.shape[0]
    dh = LANES
    rows = n_seq * rows_per_seq
    row = lambda w: pl.BlockSpec((rows, w), lambda b: (b, 0))
    full = lambda a: pl.BlockSpec(a.shape, lambda b: (0,) * a.ndim)
    c_spec = pl.BlockSpec((n_seq, H_ML, dh, dh), lambda b: (b, 0, 0, 0))
    v_spec = pl.BlockSpec((n_seq, H_ML, dh), lambda b: (b, 0, 0))
    return pl.pallas_call(
        functools.partial(_mlstm_step_kernel, rows_per_seq=rows_per_seq, valid=valid),
        grid=(batch // n_seq,),
        in_specs=[row(width), row(width), row(width), row(LANES),
                  pl.BlockSpec((1, 2 * H_ML, rows), lambda b: (b, 0, 0)),
                  row(width), full(b_row), full(b_col), full(gain), c_spec, v_spec, row(LANES)],
        out_specs=[row(width), c_spec, v_spec, v_spec],
        out_shape=[jax.ShapeDtypeStruct((n, width), F32),
                   jax.ShapeDtypeStruct((batch, H_ML, dh, dh), F32),
                   jax.ShapeDtypeStruct((batch, H_ML, dh), F32),
                   jax.ShapeDtypeStruct((batch, H_ML, dh), F32)],
        compiler_params=_params("parallel"),
        name="mlstm_sample",
    )(q, k, v, gates, gates_t, om, b_row, b_col, gain, state_c, state_n, m_rows)


def _mlstm_seq_kernel(q_ref, k_ref, v_ref, g_ref, gt_ref, om_ref, brow_ref, bcol_ref, gn_ref,
                      hm_ref, c_ref, n_ref, m_ref, fcol_scr, vx_scr, *, tile):
    seq = g_ref.shape[0]
    n_t = seq // tile
    dv = v_ref.shape[-1]
    t_i = lax.broadcasted_iota(jnp.int32, (tile, tile), 0)
    s_i = lax.broadcasted_iota(jnp.int32, (tile, tile), 1)
    causal = s_i <= t_i
    lower = causal.astype(F32)
    upper = (t_i <= s_i).astype(F32)

    off_col = jnp.zeros((1, LANES), F32)
    off_row = jnp.zeros((2 * H_ML, 1), F32)
    u_tiles = []
    for j in range(n_t):
        rows = slice(j * tile, (j + 1) * tile)
        lf_col = _log_sigmoid(g_ref[rows, :] + brow_ref[...])
        f_col = jnp.dot(lower, lf_col, preferred_element_type=F32, precision=lax.Precision.HIGHEST) + off_col
        fcol_scr[rows, :] = f_col
        off_col = f_col[tile - 1:tile, :]
        g_row = gt_ref[j] + bcol_ref[...]
        f_row = jnp.dot(_log_sigmoid(g_row), upper, preferred_element_type=F32,
                        precision=lax.Precision.HIGHEST) + off_row
        off_row = f_row[:, tile - 1:tile]
        u_tiles.append(g_row[:H_ML, :] - f_row[H_ML:, :])

    ones_lane = lax.broadcasted_iota(jnp.int32, (seq, vx_scr.shape[2] - dv), 1)
    for h in range(H_ML):
        vx_scr[h, :, :dv] = v_ref[h]
        vx_scr[h, :, dv:] = jnp.where(ones_lane == 0, 1.0, 0.0).astype(BF16)

    for h in range(H_ML):
        u_row = jnp.concatenate([u[h:h + 1, :] for u in u_tiles], axis=1)
        tile_max = [jnp.max(u[h:h + 1, :], axis=-1, keepdims=True) for u in u_tiles]
        before = [None]
        for j in range(n_t):
            before.append(tile_max[j] if before[-1] is None else jnp.maximum(before[-1], tile_max[j]))

        def weights(n):
            n_keys = (n + 1) * tile
            s = _dot_nt(q_ref[h, n * tile:(n + 1) * tile, :], k_ref[h, :n_keys, :])
            u_diag = jnp.where(causal, u_row[:, n_keys - tile:n_keys], -jnp.inf)
            r = jnp.max(u_diag, axis=-1, keepdims=True)
            if before[n] is not None:
                r = jnp.maximum(r, before[n])
            r0 = jnp.maximum(r, 0.0)
            parts = [jnp.exp(u_diag - r0)]
            if n >= 1:
                parts.insert(0, jnp.exp(u_row[:, :n_keys - tile] - r0))
            return (s * jnp.concatenate(parts, axis=1)).astype(BF16), r0

        def finish(n, p, r0):
            rows = slice(n * tile, (n + 1) * tile)
            acc = _dot(p, vx_scr[h, :(n + 1) * tile, :])
            m_t = fcol_scr[rows, H_ML + h:H_ML + h + 1] + r0
            hv = acc[:, :dv] / jnp.maximum(jnp.abs(acc[:, dv:dv + 1]), jnp.exp(-m_t))
            gate = jax.nn.sigmoid(om_ref[rows, h * LANES:(h + 1) * LANES])
            hm_ref[rows, h * LANES:(h + 1) * LANES] = (_rms(hv, gn_ref[...]) * gate).astype(hm_ref.dtype)

        for g in range(n_t // 2):
            pair = (g, n_t - 1 - g)
            done = [weights(n) for n in pair]
            for n, (p, r0) in zip(pair, done):
                finish(n, p, r0)

        r0_last = jnp.maximum(before[n_t], 0.0)
        u_col = (g_ref[:, h:h + 1] + brow_ref[:, h:h + 1]) - fcol_scr[:, H_ML + h:H_ML + h + 1]
        w_col = jnp.exp(u_col - r0_last)
        kf = k_ref[h]
        c_ref[0, h] = _dot_tn(kf, (w_col * v_ref[h].astype(F32)).astype(BF16))
        n_ref[0, h:h + 1, :] = jnp.sum(w_col * kf.astype(F32), axis=0, keepdims=True)
        m_last = off_col[:, H_ML + h:H_ML + h + 1] + r0_last
        m_ref[0, h:h + 1, :] = jnp.broadcast_to(m_last, (1, LANES))


def _mlstm_seq(q, k, v, gates, gates_t, om, b_row, b_col, gain, *, batch, tile):
    n, width = om.shape
    seq = n // batch
    n_t = seq // tile
    assert n_t % 2 == 0
    dh = LANES
    qkv_spec = pl.BlockSpec((H_ML, seq, dh), lambda b: (0, b, 0))
    row = lambda w: pl.BlockSpec((seq, w), lambda b: (b, 0))
    full = lambda a: pl.BlockSpec(a.shape, lambda b: (0,) * a.ndim)
    c_spec = pl.BlockSpec((1, H_ML, dh, dh), lambda b: (b, 0, 0, 0))
    v_spec = pl.BlockSpec((1, H_ML, dh), lambda b: (b, 0, 0))
    return pl.pallas_call(
        functools.partial(_mlstm_seq_kernel, tile=tile),
        grid=(batch,),
        in_specs=[qkv_spec, qkv_spec, qkv_spec, row(LANES),
                  pl.BlockSpec((n_t, 2 * H_ML, tile), lambda b: (b, 0, 0)),
                  row(width), full(b_row), full(b_col), full(gain)],
        out_specs=[row(width), c_spec, v_spec, v_spec],
        out_shape=[jax.ShapeDtypeStruct((n, width), BF16),
                   jax.ShapeDtypeStruct((batch, H_ML, dh, dh), F32),
                   jax.ShapeDtypeStruct((batch, H_ML, dh), F32),
                   jax.ShapeDtypeStruct((batch, H_ML, dh), F32)],
        scratch_shapes=[pltpu.VMEM((seq, LANES), F32), pltpu.VMEM((H_ML, seq, 2 * dh), BF16)],
        compiler_params=_params("parallel"),
        name="mlstm_prompt",
    )(q, k, v, gates, gates_t, om, b_row, b_col, gain)


def kernel(x_prompt, x_sample, cache_k, cache_v, state_C, state_n, state_m, page_table, rel_bias, norm_gains,
           ffn_w_gate, ffn_w_up, ffn_w_down, w_in, b_gates, lam_q, lam_k, attn_norm, mlstm_norm, w_out):
    batch, seq, d = x_prompt.shape
    dec_batch, dec_seq, _ = x_sample.shape
    depth = w_in.shape[0]
    page = cache_k.shape[2]
    width = H_ATT * LANES
    d_in = w_in.shape[-1]
    n_gate = 2 * H_ML
    assert d_in == 7 * width + n_gate and dec_seq <= SAMPLE_ROWS
    assert seq % ATT_TILE == 0 and seq % MLSTM_TILE == 0 and (batch * seq) % TOKEN_TILE == 0

    xp = x_prompt.reshape(batch * seq, d)
    xs = jnp.pad(x_sample, ((0, 0), (0, SAMPLE_ROWS - dec_seq), (0, 0))).reshape(dec_batch * SAMPLE_ROWS, d)

    prompt_bias = _prompt_bias(rel_bias, ATT_TILE)
    sample_bias = _sample_bias(rel_bias, page, dec_seq)

    outs = [[] for _ in range(10)]
    for l in range(depth):
        lam_init = 0.8 - 0.6 * math.exp(-0.3 * l)
        g = norm_gains[l]
        f = ffn_w_gate.shape[-1]
        wgu = _fuse_gate_up(ffn_w_gate.reshape(-1, d, f), ffn_w_up.reshape(-1, d, f), 2 * l, 2, WEIGHT_ROW_TILE)
        wd = _cast_groups(ffn_w_down.reshape(-1, f, d), 2 * l, 2, d, f // 4)
        w_in_p = _cast_groups(w_in, l, 1, d_in - n_gate + LANES, WEIGHT_ROW_TILE)
        wo = _cast_groups(w_out, l, 1, d, WEIGHT_ROW_TILE)
        b_row = jnp.pad(b_gates[l].reshape(1, n_gate), ((0, 0), (0, LANES - n_gate)))
        b_col = b_gates[l].reshape(n_gate, 1)
        gn_att = attn_norm[l].reshape(1, -1)
        gn_ml = mlstm_norm[l].reshape(1, -1)
        ck = cache_k.reshape(-1, LANES)
        cv = cache_v.reshape(-1, LANES)
        first_page = l * cache_k.shape[1]

        xs = _ffn(xs, g, (None, 0, 1), wgu, wd, 0)
        sample_seqs = math.gcd(dec_batch, MLSTM_SAMPLE_SEQS)
        (qa_s, kaf_s, vaf_s, _, _, qm_s, km_s, vm_s, om_s, gt_s, gtt_s) = _proj(xs, g[2:3], w_in_p, False,
                                                                                sample_seqs * SAMPLE_ROWS)
        seq3 = lambda a: a.reshape(dec_batch, SAMPLE_ROWS, width)
        xp, mix_a_s = _ffn_with_sample_attention(xp, g, (None, 0, 1), wgu, wd, 0, page_table, seq3(qa_s), kaf_s,
                                                 vaf_s, ck, cv, page, first_page, sample_bias, lam_q[l], lam_k[l],
                                                 gn_att, lam_init)

        (qa, kaf, vaf, ka, va, qm, km, vm, om, gt, gtt) = _proj(xp, g[2:3], w_in_p, True, MLSTM_TILE)
        mix_a = _prompt_attention(qa, ka, va, prompt_bias, lam_q[l], lam_k[l], gn_att, batch, lam_init)
        mix_m, c_p, n_p, m_p = _mlstm_seq(qm, km, vm, gt, gtt, om, b_row, b_col, gn_ml, batch=batch, tile=MLSTM_TILE)
        xp = _ffn(xp, g, (3, 4, 5), wgu, wd, 1, mix=(mix_a, mix_m, wo))

        m_rows = jnp.pad(jnp.repeat(state_m[l], SAMPLE_ROWS, axis=0), ((0, 0), (0, LANES - H_ML)))
        mix_m_s, c_s, n_s, m_s = _mlstm_step(qm_s, km_s, vm_s, gt_s, gtt_s, om_s, b_row, b_col, gn_ml,
                                             state_C[l], state_n[l], m_rows, rows_per_seq=SAMPLE_ROWS,
                                             valid=dec_seq, n_seq=sample_seqs)
        xs = _ffn(xs, g, (3, 4, 5), wgu, wd, 1, mix=(mix_a_s.reshape(-1, width), mix_m_s, wo))

        new = lambda a: a.reshape(dec_batch, SAMPLE_ROWS, H_ATT, LANES)[:, :dec_seq]
        for lst, val in zip(outs, (kaf.reshape(batch, seq, H_ATT, LANES), vaf.reshape(batch, seq, H_ATT, LANES),
                                   c_p, n_p, m_p[:, :, 0], new(kaf_s), new(vaf_s), c_s, n_s, m_s[:, :, 0])):
            lst.append(val)

    y_prompt = xp.reshape(batch, seq, d)
    y_sample = xs.reshape(dec_batch, SAMPLE_ROWS, d)[:, :dec_seq]
    return (y_prompt, y_sample) + tuple(jnp.stack(o) for o in outs)
```

```python
import functools
import math

import jax
import jax.numpy as jnp
from jax import lax
from jax.experimental import pallas as pl
from jax.experimental.pallas import tpu as pltpu

F32 = jnp.float32
BF16 = jnp.bfloat16

H_ATT = 4
ATT_DQ = 64
H_ML = 4
NUM_BUCKETS = 32
MAX_DISTANCE = 128
EPS = 1e-6

LANES = 128
SUBLANES = 8
VMEM_LIMIT_BYTES = 56 * 1024 * 1024
FUSED_VMEM_LIMIT_BYTES = 60 * 1024 * 1024

TOKEN_TILE = 512
FFN_CHUNK = 256
ATT_TILE = 256
ATT_GROUP_PAIRS = 2
SAMPLE_ROWS = 8
WEIGHT_ROW_TILE = 256
MLSTM_TILE = 256
MLSTM_SAMPLE_SEQS = 16


def _params(*sem):
    return pltpu.CompilerParams(dimension_semantics=sem, vmem_limit_bytes=VMEM_LIMIT_BYTES)


def _rms(x, g):
    return x * lax.rsqrt(jnp.mean(x * x, axis=-1, keepdims=True) + EPS) * g


def _dot(a, b):
    return jnp.dot(a, b, preferred_element_type=F32)


def _dot_nt(a, b):
    return lax.dot_general(a, b, (((1,), (1,)), ((), ())), preferred_element_type=F32)


def _dot_tn(a, b):
    return lax.dot_general(a, b, (((0,), (0,)), ((), ())), preferred_element_type=F32)


def _resident(arr, lead=None):
    if lead is None:
        return pl.BlockSpec(arr.shape, lambda *_: (0,) * arr.ndim, pipeline_mode=pl.Buffered(1))
    return pl.BlockSpec((None,) + arr.shape[1:], lambda *_: (lead,) + (0,) * (arr.ndim - 1),
                        pipeline_mode=pl.Buffered(1))


def _cast_kernel(x_ref, o_ref):
    w = x_ref.shape[-1]
    if o_ref.shape[-1] != w:
        o_ref[...] = jnp.zeros(o_ref.shape, o_ref.dtype)
    o_ref[:, :w] = x_ref[...].astype(o_ref.dtype)


def _cast_groups(x3, first, count, cols_out, row_tile):
    _, rows, cols = x3.shape
    return pl.pallas_call(
        _cast_kernel,
        grid=(count, rows // row_tile),
        in_specs=[pl.BlockSpec((None, row_tile, cols), lambda g, r: (first + g, r, 0))],
        out_specs=pl.BlockSpec((None, row_tile, cols_out), lambda g, r: (g, r, 0)),
        out_shape=jax.ShapeDtypeStruct((count, rows, cols_out), BF16),
        compiler_params=_params("parallel", "parallel"),
        name="cast_weights",
    )(x3)


def _gate_up_kernel(g_ref, u_ref, o_ref):
    for c in range(g_ref.shape[-1] // FFN_CHUNK):
        src = slice(c * FFN_CHUNK, (c + 1) * FFN_CHUNK)
        o_ref[:, 2 * c * FFN_CHUNK:(2 * c + 1) * FFN_CHUNK] = g_ref[:, src].astype(BF16)
        o_ref[:, (2 * c + 1) * FFN_CHUNK:(2 * c + 2) * FFN_CHUNK] = u_ref[:, src].astype(BF16)


def _fuse_gate_up(wg3, wu3, first, count, row_tile):
    _, rows, cols = wg3.shape
    spec = pl.BlockSpec((None, row_tile, cols), lambda g, r: (first + g, r, 0))
    return pl.pallas_call(
        _gate_up_kernel,
        grid=(count, rows // row_tile),
        in_specs=[spec, spec],
        out_specs=pl.BlockSpec((None, row_tile, 2 * cols), lambda g, r: (g, r, 0)),
        out_shape=jax.ShapeDtypeStruct((count, rows, 2 * cols), BF16),
        compiler_params=_params("parallel", "parallel"),
        name="fuse_gate_up",
    )(wg3, wu3)


def _ffn_kernel(*refs, with_mix, n_chunks, gain_rows):
    mix_row, pre_row, post_row = gain_rows
    if with_mix:
        (x_ref, ma_ref, mm_ref, wo_ref, g_ref, wgu_ref, wd_ref, o_ref, u_ref) = refs
        half = ma_ref.shape[-1]
        y = _dot(ma_ref[...].astype(BF16), wo_ref[:half, :]) + _dot(mm_ref[...].astype(BF16), wo_ref[half:, :])
        x = x_ref[...] + _rms(y, g_ref[mix_row:mix_row + 1, :])
    else:
        (x_ref, g_ref, wgu_ref, wd_ref, o_ref, u_ref) = refs
        x = x_ref[...]
    h = _rms(x, g_ref[pre_row:pre_row + 1, :]).astype(BF16)
    for c in range(n_chunks):
        gu = _dot(h, wgu_ref[:, 2 * c * FFN_CHUNK:2 * (c + 1) * FFN_CHUNK])
        gate = gu[:, :FFN_CHUNK]
        up = gu[:, FFN_CHUNK:]
        u_ref[:, c * FFN_CHUNK:(c + 1) * FFN_CHUNK] = (gate * jax.nn.sigmoid(gate) * up).astype(BF16)
    y = _dot(u_ref[...], wd_ref[...])
    o_ref[...] = x + 0.5 * _rms(y, g_ref[post_row:post_row + 1, :])


def _ffn(x, gains, gain_rows, wgu, wd, which, mix=None):
    n, d = x.shape
    f = wd.shape[1]
    n_chunks = f // FFN_CHUNK
    tm = TOKEN_TILE
    row = lambda w: pl.BlockSpec((tm, w), lambda i: (i, 0))
    in_specs = [row(d)]
    args = [x]
    if mix is not None:
        ma, mm, wo = mix
        in_specs += [row(ma.shape[1]), row(mm.shape[1]), _resident(wo, 0)]
        args += [ma, mm, wo]
    in_specs += [_resident(gains), _resident(wgu, which), _resident(wd, which)]
    args += [gains, wgu, wd]
    return pl.pallas_call(
        functools.partial(_ffn_kernel, with_mix=mix is not None, n_chunks=n_chunks, gain_rows=gain_rows),
        grid=(n // tm,),
        in_specs=in_specs,
        out_specs=row(d),
        out_shape=jax.ShapeDtypeStruct((n, d), F32),
        scratch_shapes=[pltpu.VMEM((tm, f), BF16)],
        compiler_params=_params("parallel"),
        name="ffn_mix" if mix is not None else "ffn",
    )(*args)


def _proj_kernel(x_ref, g_ref, w_ref, *outs, head_major, width, gate_chunk):
    (qa_ref, kaf_ref, vaf_ref, ka_ref, va_ref, qm_ref, km_ref, vm_ref, om_ref, gt_ref, gtt_ref) = outs
    xn = _rms(x_ref[...], g_ref[...]).astype(BF16)
    tm = x_ref.shape[0]
    n_heads = width // LANES

    def group(i):
        return _dot(xn, w_ref[:, i * width:(i + 1) * width])

    def put(ref, val):
        if head_major:
            for h in range(n_heads):
                ref[h] = val[:, h * LANES:(h + 1) * LANES].astype(ref.dtype)
        else:
            ref[...] = val.astype(ref.dtype)

    def put_rows(ref, val):
        for h in range(n_heads):
            ref[pl.ds(h, tm, stride=n_heads), :] = val[:, h * LANES:(h + 1) * LANES]

    put(qa_ref, group(0) * (ATT_DQ ** -0.5))
    ka = group(1)
    put_rows(kaf_ref, ka)
    put(ka_ref, ka)
    va = group(2)
    put_rows(vaf_ref, va)
    put(va_ref, va)
    put(qm_ref, group(3))
    put(km_ref, group(4) * (LANES ** -0.5))
    put(vm_ref, group(5))
    om_ref[...] = group(6)
    gates = _dot(xn, w_ref[:, 7 * width:7 * width + LANES])
    gt_ref[...] = gates
    gates_t = gates.T[:2 * H_ML, :]
    for c in range(tm // gate_chunk):
        gtt_ref[c] = gates_t[:, c * gate_chunk:(c + 1) * gate_chunk]


def _proj(x, gain, w_in_p, head_major, gate_chunk):
    n, d = x.shape
    width = H_ATT * LANES
    n_heads = width // LANES
    tm = TOKEN_TILE
    row = lambda w: pl.BlockSpec((tm, w), lambda i: (i, 0))
    if head_major:
        hm_spec = pl.BlockSpec((n_heads, tm, LANES), lambda i: (0, i, 0))
        hm_shape = jax.ShapeDtypeStruct((n_heads, n, LANES), BF16)
    else:
        hm_spec = row(width)
        hm_shape = jax.ShapeDtypeStruct((n, width), F32)
    f32_shape = jax.ShapeDtypeStruct((n, width), F32)
    rows_spec = pl.BlockSpec((tm * n_heads, LANES), lambda i: (i, 0))
    rows_shape = jax.ShapeDtypeStruct((n * n_heads, LANES), F32)
    out_specs = [hm_spec, rows_spec, rows_spec, hm_spec, hm_spec, hm_spec, hm_spec, hm_spec, row(width),
                 row(LANES), pl.BlockSpec((tm // gate_chunk, 2 * H_ML, gate_chunk), lambda i: (i, 0, 0))]
    out_shape = [hm_shape, rows_shape, rows_shape, hm_shape, hm_shape, hm_shape, hm_shape, hm_shape, f32_shape,
                 jax.ShapeDtypeStruct((n, LANES), F32),
                 jax.ShapeDtypeStruct((n // gate_chunk, 2 * H_ML, gate_chunk), F32)]
    return pl.pallas_call(
        functools.partial(_proj_kernel, head_major=head_major, width=width, gate_chunk=gate_chunk),
        grid=(n // tm,),
        in_specs=[row(d), _resident(gain), _resident(w_in_p, 0)],
        out_specs=out_specs,
        out_shape=out_shape,
        compiler_params=_params("parallel"),
        name="proj_prompt" if head_major else "proj_sample",
    )(x, gain, w_in_p)


def _t5_bias(rel, rb_ref, h):
    n = jnp.maximum(rel, 0)
    max_exact = NUM_BUCKETS // 2
    x = (jnp.log(jnp.maximum(n, max_exact).astype(F32) / max_exact)
         / math.log(MAX_DISTANCE / max_exact) * (NUM_BUCKETS - max_exact))
    out = jnp.full(rel.shape, rb_ref[max_exact, h], F32)
    for i in range(1, NUM_BUCKETS - max_exact):
        out = jnp.where(x >= i, rb_ref[max_exact + i, h], out)
    for i in range(max_exact):
        out = jnp.where(n == i, rb_ref[i, h], out)
    return out - rb_ref[NUM_BUCKETS - 1, h]


def _prompt_bias_kernel(rb_ref, o_ref, *, tile):
    r = lax.broadcasted_iota(jnp.int32, (tile, tile), 0)
    c = lax.broadcasted_iota(jnp.int32, (tile, tile), 1)
    for h in range(H_ATT):
        o_ref[h, 0] = jnp.where(c <= r, _t5_bias(r - c, rb_ref, h), -jnp.inf)
        o_ref[h, 1] = _t5_bias(tile + r - c, rb_ref, h)


def _prompt_bias(rel_bias, tile):
    return pl.pallas_call(
        functools.partial(_prompt_bias_kernel, tile=tile),
        in_specs=[pl.BlockSpec(memory_space=pltpu.SMEM)],
        out_shape=jax.ShapeDtypeStruct((H_ATT, 2, tile, tile), F32),
        name="prompt_bias",
    )(rel_bias)


def _sample_bias_kernel(rb_ref, o_ref, *, page, valid):
    qi = lax.broadcasted_iota(jnp.int32, (SUBLANES, 2 * page), 0)
    col = lax.broadcasted_iota(jnp.int32, (SUBLANES, 2 * page), 1)
    kj = col - page
    for h in range(H_ATT):
        past = _t5_bias(page + qi - col, rb_ref, h)
        new = jnp.where((kj <= qi) & (kj < valid), _t5_bias(qi - kj, rb_ref, h), -jnp.inf)
        tile = jnp.where(col < page, past, new)
        for comp in range(2):
            o_ref[(comp * H_ATT + h) * SUBLANES:(comp * H_ATT + h + 1) * SUBLANES, :] = tile


def _sample_bias(rel_bias, page, valid):
    return pl.pallas_call(
        functools.partial(_sample_bias_kernel, page=page, valid=valid),
        in_specs=[pl.BlockSpec(memory_space=pltpu.SMEM)],
        out_shape=jax.ShapeDtypeStruct((2 * H_ATT * SUBLANES, 2 * page), F32),
        name="sample_bias",
    )(rel_bias)


def _lambda(lq_ref, lk_ref, lam_init):
    prod = lq_ref[...] * lk_ref[...]
    return (jnp.exp(jnp.sum(prod[0:1, :], axis=-1, keepdims=True))
            - jnp.exp(jnp.sum(prod[1:2, :], axis=-1, keepdims=True)) + lam_init)


def _attn_kernel(q_ref, k_ref, v_ref, bias_ref, lq_ref, lk_ref, gn_ref, o_ref, vx_scr, *, tile, n_q, lam_init):
    dv = v_ref.shape[-1]
    vx_scr[:, :dv] = v_ref[0]
    ones_lane = lax.broadcasted_iota(jnp.int32, (vx_scr.shape[0], vx_scr.shape[1] - dv), 1)
    vx_scr[:, dv:] = jnp.where(ones_lane == 0, 1.0, 0.0).astype(BF16)

    lane = lax.broadcasted_iota(jnp.int32, (tile, dv), 1)
    zero = jnp.zeros((tile, dv), BF16)
    lam = _lambda(lq_ref, lk_ref, lam_init)
    diag = jnp.concatenate([bias_ref[0, 0]] * 2, axis=0)
    near = jnp.concatenate([bias_ref[0, 1]] * 2, axis=0)

    def logits(n):
        q = q_ref[0, n * tile:(n + 1) * tile, :]
        qq = jnp.concatenate([jnp.where(lane < ATT_DQ, q, zero), jnp.where(lane >= ATT_DQ, q, zero)], axis=0)
        n_keys = (n + 1) * tile
        s = _dot_nt(qq, k_ref[0, :n_keys, :])
        parts = [s[:, n_keys - tile:] + diag]
        if n >= 1:
            parts.insert(0, s[:, n_keys - 2 * tile:n_keys - tile] + near)
        if n >= 2:
            parts.insert(0, s[:, :n_keys - 2 * tile])
        return parts

    def probs(parts):
        m = functools.reduce(jnp.maximum, [jnp.max(x, axis=-1, keepdims=True) for x in parts])
        return jnp.concatenate([jnp.exp(x - m) for x in parts], axis=1).astype(BF16)

    def finish(n, p):
        vx = vx_scr[:(n + 1) * tile, :]
        acc0 = _dot(p[:tile], vx)
        acc1 = _dot(p[tile:], vx)
        out = acc0[:, :dv] / acc0[:, dv:dv + 1] - lam * (acc1[:, :dv] / acc1[:, dv:dv + 1])
        o_ref[n * tile:(n + 1) * tile, :] = (_rms(out, gn_ref[...]) * (1.0 - lam_init)).astype(o_ref.dtype)

    for g in range(0, n_q // 2, ATT_GROUP_PAIRS):
        group = [n for gg in range(g, min(g + ATT_GROUP_PAIRS, n_q // 2)) for n in (gg, n_q - 1 - gg)]
        p_group = [probs(logits(n)) for n in group]
        for n, p in zip(group, p_group):
            finish(n, p)


def _prompt_attention(q, k, v, bias, lam_q, lam_k, gain, batch, lam_init):
    h, n, dv = q.shape
    seq = n // batch
    tile = ATT_TILE
    nq = seq // tile
    assert nq % 2 == 0
    seq_spec = pl.BlockSpec((1, seq, dv), lambda b, hh: (hh, b, 0))
    return pl.pallas_call(
        functools.partial(_attn_kernel, tile=tile, n_q=nq, lam_init=lam_init),
        grid=(batch, h),
        in_specs=[seq_spec, seq_spec, seq_spec,
                  pl.BlockSpec((1, 2, tile, tile), lambda b, hh: (hh, 0, 0, 0)),
                  pl.BlockSpec(lam_q.shape, lambda b, hh: (0, 0)),
                  pl.BlockSpec(lam_k.shape, lambda b, hh: (0, 0)),
                  pl.BlockSpec(gain.shape, lambda b, hh: (0, 0))],
        out_specs=pl.BlockSpec((seq, dv), lambda b, hh: (b, hh)),
        out_shape=jax.ShapeDtypeStruct((n, h * dv), BF16),
        scratch_shapes=[pltpu.VMEM((seq, 2 * dv), BF16)],
        compiler_params=_params("parallel", "parallel"),
        name="attn_prompt",
    )(q, k, v, bias, lam_q, lam_k, gain)


def _sample_attn_core(q, load_k, load_v, load_kn, load_vn, bias_ref, lam, gn_ref, k_buf, v_buf,
                      *, n_pages, page, n_tok_new, lam_init):
    width = q.shape[-1]
    n_rows = 2 * H_ATT * SUBLANES
    qt = jnp.concatenate([q] * (2 * H_ATT), axis=0)
    r = lax.broadcasted_iota(jnp.int32, (n_rows, width), 0)
    c = lax.broadcasted_iota(jnp.int32, (n_rows, width), 1)
    comp = r // (H_ATT * SUBLANES)
    head = (r // SUBLANES) % H_ATT
    own = (c // LANES == head) & ((c // ATT_DQ) % 2 == comp)
    qbd = jnp.where(own, qt, 0.0).astype(BF16)

    n_past = n_pages * page
    new_rows = 2 * SUBLANES
    for load, load_new, buf in ((load_k, load_kn, k_buf), (load_v, load_vn, v_buf)):
        for p in range(n_pages):
            for h in range(H_ATT):
                buf[p * page:(p + 1) * page, h * LANES:(h + 1) * LANES] = load(p, h).astype(BF16)
        new = jnp.concatenate([load_new(h) for h in range(H_ATT)], axis=1)
        new = jnp.concatenate([new, jnp.zeros((new_rows - n_tok_new, width), F32)], axis=0)
        buf[n_past:n_past + new_rows, :] = new.astype(BF16)
        buf[n_past + new_rows:, :] = jnp.zeros((page - new_rows, width), BF16)

    s = _dot_nt(qbd, k_buf[...])
    parts = [s[:, :n_past - page], s[:, n_past - page:] + bias_ref[...]]
    m = functools.reduce(jnp.maximum, [jnp.max(x, axis=-1, keepdims=True) for x in parts])
    p = jnp.concatenate([jnp.exp(x - m) for x in parts], axis=1)
    inv = 1.0 / jnp.sum(p, axis=-1, keepdims=True)
    half = n_rows // 2
    a = (p[:half] * inv[:half] - lam * (p[half:] * inv[half:])).astype(BF16)
    out = _dot(a, v_buf[...])
    heads = [_rms(out[h * SUBLANES:(h + 1) * SUBLANES, h * LANES:(h + 1) * LANES], gn_ref[...])
             * (1.0 - lam_init) for h in range(H_ATT)]
    return jnp.concatenate(heads, axis=1)


def _ffn_attn_kernel(pt_ref, x_ref, g_ref, wgu_ref, wd_ref, q_ref, kn_ref, vn_ref, bias_ref, lq_ref, lk_ref,
                     gn_ref, ck_hbm, cv_hbm, o_ref, ao_ref, u_ref, k_ring, v_ring, k_buf, v_buf, sem,
                     *, n_chunks, gain_rows, n_pages, page, first_page, n_seq, lam_init):
    _, pre_row, post_row = gain_rows
    i = pl.program_id(0)
    seqs = q_ref.shape[0]
    page_rows = page * H_ATT
    new_rows = kn_ref.shape[0] // seqs
    n_tok_new = new_rows // H_ATT

    def page_copies(g, slot):
        copies = []
        for p in range(n_pages):
            src = pl.ds(pl.multiple_of((first_page + pt_ref[g, p]) * page_rows, page_rows), page_rows)
            dst = pl.ds(p * page_rows, page_rows)
            copies.append(pltpu.make_async_copy(ck_hbm.at[src, :], k_ring.at[slot, dst, :], sem.at[slot, 0]))
            copies.append(pltpu.make_async_copy(cv_hbm.at[src, :], v_ring.at[slot, dst, :], sem.at[slot, 1]))
        return copies

    @pl.when(i == 0)
    def _():
        for cp in page_copies(0, 0):
            cp.start()

    x = x_ref[...]
    h = _rms(x, g_ref[pre_row:pre_row + 1, :]).astype(BF16)
    lam = _lambda(lq_ref, lk_ref, lam_init)
    chunk_bounds = [(n_chunks * j) // seqs for j in range(seqs + 1)]
    for j in range(seqs):
        g = i * seqs + j
        slot = j % 2

        for cp in page_copies(jnp.minimum(g + 1, n_seq - 1), 1 - slot):
            cp.start()

        for c in range(chunk_bounds[j], chunk_bounds[j + 1]):
            gu = _dot(h, wgu_ref[:, 2 * c * FFN_CHUNK:2 * (c + 1) * FFN_CHUNK])
            gate = gu[:, :FFN_CHUNK]
            up = gu[:, FFN_CHUNK:]
            u_ref[:, c * FFN_CHUNK:(c + 1) * FFN_CHUNK] = (gate * jax.nn.sigmoid(gate) * up).astype(BF16)

        for cp in page_copies(g, slot):
            cp.wait()

        def past(ring, p, hh, slot=slot):
            return ring[slot, pl.ds(p * page_rows + hh, page, stride=H_ATT), :]

        def new(ref, hh, j=j):
            return ref[pl.ds(j * new_rows + hh, n_tok_new, stride=H_ATT), :]

        ao_ref[j] = _sample_attn_core(
            q_ref[j], functools.partial(past, k_ring), functools.partial(past, v_ring),
            functools.partial(new, kn_ref), functools.partial(new, vn_ref), bias_ref, lam, gn_ref, k_buf, v_buf,
            n_pages=n_pages, page=page, n_tok_new=n_tok_new, lam_init=lam_init)

    y = _dot(u_ref[...], wd_ref[...])
    o_ref[...] = x + 0.5 * _rms(y, g_ref[post_row:post_row + 1, :])

    @pl.when(i == pl.num_programs(0) - 1)
    def _():
        for cp in page_copies(n_seq - 1, seqs % 2):
            cp.wait()


def _ffn_with_sample_attention(x, gains, gain_rows, wgu, wd, which, page_table, q, k_new, v_new, cache_k, cache_v,
                               page, first_page, bias, lam_q, lam_k, gain, lam_init):
    n, d = x.shape
    f = wd.shape[1]
    tm = TOKEN_TILE
    steps = n // tm
    nb, rows, width = q.shape
    n_pages = page_table.shape[1]
    n_heads = width // LANES
    seqs = nb // steps
    assert nb == seqs * steps and seqs % 2 == 0
    ring = pltpu.VMEM((2, n_pages * page * n_heads, LANES), F32)
    stage = pltpu.VMEM(((n_pages + 1) * page, width), BF16)
    full = lambda a: pl.BlockSpec(a.shape, lambda i, pt: (0,) * a.ndim)
    seq_spec = pl.BlockSpec((seqs, rows, width), lambda i, pt: (i, 0, 0))
    new_spec = pl.BlockSpec((seqs * rows * n_heads, LANES), lambda i, pt: (i, 0))
    row_spec = pl.BlockSpec((tm, d), lambda i, pt: (i, 0))
    grid_spec = pltpu.PrefetchScalarGridSpec(
        num_scalar_prefetch=1,
        grid=(steps,),
        in_specs=[row_spec, _resident(gains), _resident(wgu, which), _resident(wd, which),
                  seq_spec, new_spec, new_spec, full(bias), full(lam_q), full(lam_k), full(gain),
                  pl.BlockSpec(memory_space=pl.ANY), pl.BlockSpec(memory_space=pl.ANY)],
        out_specs=[row_spec, seq_spec],
        scratch_shapes=[pltpu.VMEM((tm, f), BF16), ring, ring, stage, stage, pltpu.SemaphoreType.DMA((2, 2))],
    )
    return pl.pallas_call(
        functools.partial(_ffn_attn_kernel, n_chunks=f // FFN_CHUNK, gain_rows=gain_rows, n_pages=n_pages,
                          page=page, first_page=first_page, n_seq=nb, lam_init=lam_init),
        grid_spec=grid_spec,
        out_shape=[jax.ShapeDtypeStruct((n, d), F32), jax.ShapeDtypeStruct((nb, rows, width), F32)],
        compiler_params=pltpu.CompilerParams(dimension_semantics=("arbitrary",),
                                             vmem_limit_bytes=FUSED_VMEM_LIMIT_BYTES),
        name="ffn_attn_sample",
    )(page_table, x, gains, wgu, wd, q, k_new, v_new, bias, lam_q, lam_k, gain, cache_k, cache_v)


def _log_sigmoid(x):
    return jnp.minimum(x, 0.0) - jnp.log1p(jnp.exp(-jnp.abs(x)))


def _mlstm_step_kernel(q_ref, k_ref, v_ref, g_ref, gt_ref, om_ref, brow_ref, bcol_ref, gn_ref,
                       c0_ref, n0_ref, m0_ref, hm_ref, c_ref, n_ref, m_ref, *, rows_per_seq, valid):
    n_seq = c0_ref.shape[0]
    P = rows_per_seq
    R = n_seq * P
    r_i = lax.broadcasted_iota(jnp.int32, (R, R), 0)
    c_i = lax.broadcasted_iota(jnp.int32, (R, R), 1)
    same = (r_i // P) == (c_i // P)
    causal = same & (c_i <= r_i)
    hi = lax.Precision.HIGHEST

    g_col = g_ref[...] + brow_ref[...]
    g_row = gt_ref[0] + bcol_ref[...]
    lf_col = _log_sigmoid(g_col)
    lf_row = _log_sigmoid(g_row)
    tok_c = lax.broadcasted_iota(jnp.int32, g_col.shape, 0) % P
    tok_r = lax.broadcasted_iota(jnp.int32, g_row.shape, 1) % P
    g_col = jnp.where(tok_c < valid, g_col, -1e30)
    g_row = jnp.where(tok_r < valid, g_row, -1e30)
    lf_col = jnp.where(tok_c < valid, lf_col, 0.0)
    lf_row = jnp.where(tok_r < valid, lf_row, 0.0)
    same_f = same.astype(F32)
    b_col = jnp.dot(causal.astype(F32), lf_col, preferred_element_type=F32, precision=hi)
    b_row = jnp.dot(lf_row, (same & (r_i <= c_i)).astype(F32), preferred_element_type=F32, precision=hi)
    bl_col = jnp.dot(same_f, lf_col, preferred_element_type=F32, precision=hi)
    bl_row = jnp.dot(lf_row, same_f, preferred_element_type=F32, precision=hi)

    for h in range(H_ML):
        cols = slice(h * LANES, (h + 1) * LANES)
        q = q_ref[:, cols]
        k = k_ref[:, cols]
        v = v_ref[:, cols]
        bc = b_col[:, H_ML + h:H_ML + h + 1]
        ic = g_col[:, h:h + 1]
        br = b_row[H_ML + h:H_ML + h + 1, :]
        ir = g_row[h:h + 1, :]
        blc = bl_col[:, H_ML + h:H_ML + h + 1]
        blr = bl_row[H_ML + h:H_ML + h + 1, :]
        m = m0_ref[:, h:h + 1]

        log_d = jnp.where(causal, bc - br + ir, -jnp.inf)
        inter = bc + m
        m_t = jnp.maximum(inter, jnp.max(log_d, axis=-1, keepdims=True))
        s = _dot_nt(q.astype(BF16), k.astype(BF16)) * jnp.exp(log_d - m_t)
        w_inter = jnp.exp(inter - m_t)
        q_c = jnp.concatenate([_dot(q[j * P:(j + 1) * P].astype(BF16), c0_ref[j, h].astype(BF16))
                               for j in range(n_seq)], axis=0)
        n_rows = jnp.concatenate([jnp.broadcast_to(n0_ref[j, h:h + 1, :], (P, LANES)) for j in range(n_seq)],
                                 axis=0)
        num = _dot(s.astype(BF16), v.astype(BF16)) + w_inter * q_c
        den = jnp.sum(s, axis=-1, keepdims=True) + w_inter * jnp.sum(q * n_rows, axis=-1, keepdims=True)
        hv = num / jnp.maximum(jnp.abs(den), jnp.exp(-m_t))
        hm_ref[:, cols] = _rms(hv, gn_ref[...]) * jax.nn.sigmoid(om_ref[:, cols])

        w_max = jnp.max(jnp.where(same, blr - br + ir, -jnp.inf), axis=-1, keepdims=True)
        m_new = jnp.maximum(blc + m, w_max)
        ws = jnp.exp(blc - bc + ic - m_new)
        fw = jnp.exp(blc + m - m_new)
        wv = ws * v
        wk = ws * k
        for j in range(n_seq):
            rows = slice(j * P, (j + 1) * P)
            fw_j = fw[j * P:j * P + 1, :]
            c_ref[j, h] = fw_j * c0_ref[j, h] + _dot_tn(k[rows].astype(BF16), wv[rows].astype(BF16))
            n_ref[j, h:h + 1, :] = fw_j * n0_ref[j, h:h + 1, :] + jnp.sum(wk[rows], axis=0, keepdims=True)
            m_ref[j, h:h + 1, :] = jnp.broadcast_to(m_new[j * P:j * P + 1, :], (1, LANES))


def _mlstm_step(q, k, v, gates, gates_t, om, b_row, b_col, gain, state_c, state_n, m_rows, *, rows_per_seq,
                valid, n_seq):
    n, width = om.shape
    batch = state_c.shape[0]
    dh = LANES
    rows = n_seq * rows_per_seq
    row = lambda w: pl.BlockSpec((rows, w), lambda b: (b, 0))
    full = lambda a: pl.BlockSpec(a.shape, lambda b: (0,) * a.ndim)
    c_spec = pl.BlockSpec((n_seq, H_ML, dh, dh), lambda b: (b, 0, 0, 0))
    v_spec = pl.BlockSpec((n_seq, H_ML, dh), lambda b: (b, 0, 0))
    return pl.pallas_call(
        functools.partial(_mlstm_step_kernel, rows_per_seq=rows_per_seq, valid=valid),
        grid=(batch // n_seq,),
        in_specs=[row(width), row(width), row(width), row(LANES),
                  pl.BlockSpec((1, 2 * H_ML, rows), lambda b: (b, 0, 0)),
                  row(width), full(b_row), full(b_col), full(gain), c_spec, v_spec, row(LANES)],
        out_specs=[row(width), c_spec, v_spec, v_spec],
        out_shape=[jax.ShapeDtypeStruct((n, width), F32),
                   jax.ShapeDtypeStruct((batch, H_ML, dh, dh), F32),
                   jax.ShapeDtypeStruct((batch, H_ML, dh), F32),
                   jax.ShapeDtypeStruct((batch, H_ML, dh), F32)],
        compiler_params=_params("parallel"),
        name="mlstm_sample",
    )(q, k, v, gates, gates_t, om, b_row, b_col, gain, state_c, state_n, m_rows)


def _mlstm_seq_kernel(q_ref, k_ref, v_ref, g_ref, gt_ref, om_ref, brow_ref, bcol_ref, gn_ref,
                      hm_ref, c_ref, n_ref, m_ref, fcol_scr, vx_scr, *, tile):
    seq = g_ref.shape[0]
    n_t = seq // tile
    dv = v_ref.shape[-1]
    t_i = lax.broadcasted_iota(jnp.int32, (tile, tile), 0)
    s_i = lax.broadcasted_iota(jnp.int32, (tile, tile), 1)
    causal = s_i <= t_i
    lower = causal.astype(F32)
    upper = (t_i <= s_i).astype(F32)

    off_col = jnp.zeros((1, LANES), F32)
    off_row = jnp.zeros((2 * H_ML, 1), F32)
    u_tiles = []
    for j in range(n_t):
        rows = slice(j * tile, (j + 1) * tile)
        lf_col = _log_sigmoid(g_ref[rows, :] + brow_ref[...])
        f_col = jnp.dot(lower, lf_col, preferred_element_type=F32, precision=lax.Precision.HIGHEST) + off_col
        fcol_scr[rows, :] = f_col
        off_col = f_col[tile - 1:tile, :]
        g_row = gt_ref[j] + bcol_ref[...]
        f_row = jnp.dot(_log_sigmoid(g_row), upper, preferred_element_type=F32,
                        precision=lax.Precision.HIGHEST) + off_row
        off_row = f_row[:, tile - 1:tile]
        u_tiles.append(g_row[:H_ML, :] - f_row[H_ML:, :])

    ones_lane = lax.broadcasted_iota(jnp.int32, (seq, vx_scr.shape[2] - dv), 1)
    for h in range(H_ML):
        vx_scr[h, :, :dv] = v_ref[h]
        vx_scr[h, :, dv:] = jnp.where(ones_lane == 0, 1.0, 0.0).astype(BF16)

    for h in range(H_ML):
        u_row = jnp.concatenate([u[h:h + 1, :] for u in u_tiles], axis=1)
        tile_max = [jnp.max(u[h:h + 1, :], axis=-1, keepdims=True) for u in u_tiles]
        before = [None]
        for j in range(n_t):
            before.append(tile_max[j] if before[-1] is None else jnp.maximum(before[-1], tile_max[j]))

        def weights(n):
            n_keys = (n + 1) * tile
            s = _dot_nt(q_ref[h, n * tile:(n + 1) * tile, :], k_ref[h, :n_keys, :])
            u_diag = jnp.where(causal, u_row[:, n_keys - tile:n_keys], -jnp.inf)
            r = jnp.max(u_diag, axis=-1, keepdims=True)
            if before[n] is not None:
                r = jnp.maximum(r, before[n])
            r0 = jnp.maximum(r, 0.0)
            parts = [jnp.exp(u_diag - r0)]
            if n >= 1:
                parts.insert(0, jnp.exp(u_row[:, :n_keys - tile] - r0))
            return (s * jnp.concatenate(parts, axis=1)).astype(BF16), r0

        def finish(n, p, r0):
            rows = slice(n * tile, (n + 1) * tile)
            acc = _dot(p, vx_scr[h, :(n + 1) * tile, :])
            m_t = fcol_scr[rows, H_ML + h:H_ML + h + 1] + r0
            hv = acc[:, :dv] / jnp.maximum(jnp.abs(acc[:, dv:dv + 1]), jnp.exp(-m_t))
            gate = jax.nn.sigmoid(om_ref[rows, h * LANES:(h + 1) * LANES])
            hm_ref[rows, h * LANES:(h + 1) * LANES] = (_rms(hv, gn_ref[...]) * gate).astype(hm_ref.dtype)

        for g in range(n_t // 2):
            pair = (g, n_t - 1 - g)
            done = [weights(n) for n in pair]
            for n, (p, r0) in zip(pair, done):
                finish(n, p, r0)

        r0_last = jnp.maximum(before[n_t], 0.0)
        u_col = (g_ref[:, h:h + 1] + brow_ref[:, h:h + 1]) - fcol_scr[:, H_ML + h:H_ML + h + 1]
        w_col = jnp.exp(u_col - r0_last)
        kf = k_ref[h]
        c_ref[0, h] = _dot_tn(kf, (w_col * v_ref[h].astype(F32)).astype(BF16))
        n_ref[0, h:h + 1, :] = jnp.sum(w_col * kf.astype(F32), axis=0, keepdims=True)
        m_last = off_col[:, H_ML + h:H_ML + h + 1] + r0_last
        m_ref[0, h:h + 1, :] = jnp.broadcast_to(m_last, (1, LANES))


def _mlstm_seq(q, k, v, gates, gates_t, om, b_row, b_col, gain, *, batch, tile):
    n, width = om.shape
    seq = n // batch
    n_t = seq // tile
    assert n_t % 2 == 0
    dh = LANES
    qkv_spec = pl.BlockSpec((H_ML, seq, dh), lambda b: (0, b, 0))
    row = lambda w: pl.BlockSpec((seq, w), lambda b: (b, 0))
    full = lambda a: pl.BlockSpec(a.shape, lambda b: (0,) * a.ndim)
    c_spec = pl.BlockSpec((1, H_ML, dh, dh), lambda b: (b, 0, 0, 0))
    v_spec = pl.BlockSpec((1, H_ML, dh), lambda b: (b, 0, 0))
    return pl.pallas_call(
        functools.partial(_mlstm_seq_kernel, tile=tile),
        grid=(batch,),
        in_specs=[qkv_spec, qkv_spec, qkv_spec, row(LANES),
                  pl.BlockSpec((n_t, 2 * H_ML, tile), lambda b: (b, 0, 0)),
                  row(width), full(b_row), full(b_col), full(gain)],
        out_specs=[row(width), c_spec, v_spec, v_spec],
        out_shape=[jax.ShapeDtypeStruct((n, width), BF16),
                   jax.ShapeDtypeStruct((batch, H_ML, dh, dh), F32),
                   jax.ShapeDtypeStruct((batch, H_ML, dh), F32),
                   jax.ShapeDtypeStruct((batch, H_ML, dh), F32)],
        scratch_shapes=[pltpu.VMEM((seq, LANES), F32), pltpu.VMEM((H_ML, seq, 2 * dh), BF16)],
        compiler_params=_params("parallel"),
        name="mlstm_prompt",
    )(q, k, v, gates, gates_t, om, b_row, b_col, gain)


def kernel(x_prompt, x_sample, cache_k, cache_v, state_C, state_n, state_m, page_table, rel_bias, norm_gains,
           ffn_w_gate, ffn_w_up, ffn_w_down, w_in, b_gates, lam_q, lam_k, attn_norm, mlstm_norm, w_out):
    batch, seq, d = x_prompt.shape
    dec_batch, dec_seq, _ = x_sample.shape
    depth = w_in.shape[0]
    page = cache_k.shape[2]
    width = H_ATT * LANES
    d_in = w_in.shape[-1]
    n_gate = 2 * H_ML
    assert d_in == 7 * width + n_gate and dec_seq <= SAMPLE_ROWS
    assert seq % ATT_TILE == 0 and seq % MLSTM_TILE == 0 and (batch * seq) % TOKEN_TILE == 0

    xp = x_prompt.reshape(batch * seq, d)
    xs = jnp.pad(x_sample, ((0, 0), (0, SAMPLE_ROWS - dec_seq), (0, 0))).reshape(dec_batch * SAMPLE_ROWS, d)

    prompt_bias = _prompt_bias(rel_bias, ATT_TILE)
    sample_bias = _sample_bias(rel_bias, page, dec_seq)

    outs = [[] for _ in range(10)]
    for l in range(depth):
        lam_init = 0.8 - 0.6 * math.exp(-0.3 * l)
        g = norm_gains[l]
        f = ffn_w_gate.shape[-1]
        wgu = _fuse_gate_up(ffn_w_gate.reshape(-1, d, f), ffn_w_up.reshape(-1, d, f), 2 * l, 2, WEIGHT_ROW_TILE)
        wd = _cast_groups(ffn_w_down.reshape(-1, f, d), 2 * l, 2, d, f // 4)
        w_in_p = _cast_groups(w_in, l, 1, d_in - n_gate + LANES, WEIGHT_ROW_TILE)
        wo = _cast_groups(w_out, l, 1, d, WEIGHT_ROW_TILE)
        b_row = jnp.pad(b_gates[l].reshape(1, n_gate), ((0, 0), (0, LANES - n_gate)))
        b_col = b_gates[l].reshape(n_gate, 1)
        gn_att = attn_norm[l].reshape(1, -1)
        gn_ml = mlstm_norm[l].reshape(1, -1)
        ck = cache_k.reshape(-1, LANES)
        cv = cache_v.reshape(-1, LANES)
        first_page = l * cache_k.shape[1]

        xs = _ffn(xs, g, (None, 0, 1), wgu, wd, 0)
        sample_seqs = math.gcd(dec_batch, MLSTM_SAMPLE_SEQS)
        (qa_s, kaf_s, vaf_s, _, _, qm_s, km_s, vm_s, om_s, gt_s, gtt_s) = _proj(xs, g[2:3], w_in_p, False,
                                                                                sample_seqs * SAMPLE_ROWS)
        seq3 = lambda a: a.reshape(dec_batch, SAMPLE_ROWS, width)
        xp, mix_a_s = _ffn_with_sample_attention(xp, g, (None, 0, 1), wgu, wd, 0, page_table, seq3(qa_s), kaf_s,
                                                 vaf_s, ck, cv, page, first_page, sample_bias, lam_q[l], lam_k[l],
                                                 gn_att, lam_init)

        (qa, kaf, vaf, ka, va, qm, km, vm, om, gt, gtt) = _proj(xp, g[2:3], w_in_p, True, MLSTM_TILE)
        mix_a = _prompt_attention(qa, ka, va, prompt_bias, lam_q[l], lam_k[l], gn_att, batch, lam_init)
        mix_m, c_p, n_p, m_p = _mlstm_seq(qm, km, vm, gt, gtt, om, b_row, b_col, gn_ml, batch=batch, tile=MLSTM_TILE)
        xp = _ffn(xp, g, (3, 4, 5), wgu, wd, 1, mix=(mix_a, mix_m, wo))

        m_rows = jnp.pad(jnp.repeat(state_m[l], SAMPLE_ROWS, axis=0), ((0, 0), (0, LANES - H_ML)))
        mix_m_s, c_s, n_s, m_s = _mlstm_step(qm_s, km_s, vm_s, gt_s, gtt_s, om_s, b_row, b_col, gn_ml,
                                             state_C[l], state_n[l], m_rows, rows_per_seq=SAMPLE_ROWS,
                                             valid=dec_seq, n_seq=sample_seqs)
        xs = _ffn(xs, g, (3, 4, 5), wgu, wd, 1, mix=(mix_a_s.reshape(-1, width), mix_m_s, wo))

        new = lambda a: a.reshape(dec_batch, SAMPLE_ROWS, H_ATT, LANES)[:, :dec_seq]
        for lst, val in zip(outs, (kaf.reshape(batch, seq, H_ATT, LANES), vaf.reshape(batch, seq, H_ATT, LANES),
                                   c_p, n_p, m_p[:, :, 0], new(kaf_s), new(vaf_s), c_s, n_s, m_s[:, :, 0])):
            lst.append(val)

    y_prompt = xp.reshape(batch, seq, d)
    y_sample = xs.reshape(dec_batch, SAMPLE_ROWS, d)[:, :dec_seq]
    return (y_prompt, y_sample) + tuple(jnp.stack(o) for o in outs)
```

```python
import functools
import math

import jax
import jax.numpy as jnp
from jax import lax
from jax.experimental import pallas as pl
from jax.experimental.pallas import tpu as pltpu

F32 = jnp.float32
BF16 = jnp.bfloat16

H_ATT = 4
ATT_DQ = 64
H_ML = 4
NUM_BUCKETS = 32
MAX_DISTANCE = 128
EPS = 1e-6
LOG2E = math.log2(math.e)

LANES = 128
SUBLANES = 8
VMEM_LIMIT_BYTES = 56 * 1024 * 1024
FUSED_VMEM_LIMIT_BYTES = 60 * 1024 * 1024

TOKEN_TILE = 512
FFN_CHUNK = 256
ATT_TILE = 256
ATT_GROUP_PAIRS = 2
SAMPLE_ROWS = 8
WEIGHT_ROW_TILE = 256
MLSTM_TILE = 256
MLSTM_SAMPLE_SEQS = 16


def _params(*sem):
    return pltpu.CompilerParams(dimension_semantics=sem, vmem_limit_bytes=VMEM_LIMIT_BYTES)


def _rms(x, g):
    return x * lax.rsqrt(jnp.mean(x * x, axis=-1, keepdims=True) + EPS) * g


def _dot(a, b):
    return jnp.dot(a, b, preferred_element_type=F32)


def _dot_nt(a, b):
    return lax.dot_general(a, b, (((1,), (1,)), ((), ())), preferred_element_type=F32)


def _dot_tn(a, b):
    return lax.dot_general(a, b, (((0,), (0,)), ((), ())), preferred_element_type=F32)


def _resident(arr, lead=None):
    if lead is None:
        return pl.BlockSpec(arr.shape, lambda *_: (0,) * arr.ndim, pipeline_mode=pl.Buffered(1))
    return pl.BlockSpec((None,) + arr.shape[1:], lambda *_: (lead,) + (0,) * (arr.ndim - 1),
                        pipeline_mode=pl.Buffered(1))


def _cast_kernel(x_ref, o_ref):
    w = x_ref.shape[-1]
    if o_ref.shape[-1] != w:
        o_ref[...] = jnp.zeros(o_ref.shape, o_ref.dtype)
    o_ref[:, :w] = x_ref[...].astype(o_ref.dtype)


def _cast_groups(x3, first, count, cols_out, row_tile):
    _, rows, cols = x3.shape
    return pl.pallas_call(
        _cast_kernel,
        grid=(count, rows // row_tile),
        in_specs=[pl.BlockSpec((None, row_tile, cols), lambda g, r: (first + g, r, 0))],
        out_specs=pl.BlockSpec((None, row_tile, cols_out), lambda g, r: (g, r, 0)),
        out_shape=jax.ShapeDtypeStruct((count, rows, cols_out), BF16),
        compiler_params=_params("parallel", "parallel"),
        name="cast_weights",
    )(x3)


def _gate_up_kernel(g_ref, u_ref, o_ref):
    for c in range(g_ref.shape[-1] // FFN_CHUNK):
        src = slice(c * FFN_CHUNK, (c + 1) * FFN_CHUNK)
        o_ref[:, 2 * c * FFN_CHUNK:(2 * c + 1) * FFN_CHUNK] = g_ref[:, src].astype(BF16)
        o_ref[:, (2 * c + 1) * FFN_CHUNK:(2 * c + 2) * FFN_CHUNK] = u_ref[:, src].astype(BF16)


def _fuse_gate_up(wg3, wu3, first, count, row_tile):
    _, rows, cols = wg3.shape
    spec = pl.BlockSpec((None, row_tile, cols), lambda g, r: (first + g, r, 0))
    return pl.pallas_call(
        _gate_up_kernel,
        grid=(count, rows // row_tile),
        in_specs=[spec, spec],
        out_specs=pl.BlockSpec((None, row_tile, 2 * cols), lambda g, r: (g, r, 0)),
        out_shape=jax.ShapeDtypeStruct((count, rows, 2 * cols), BF16),
        compiler_params=_params("parallel", "parallel"),
        name="fuse_gate_up",
    )(wg3, wu3)


def _ffn_kernel(*refs, with_mix, n_chunks, gain_rows):
    mix_row, pre_row, post_row = gain_rows
    if with_mix:
        (x_ref, ma_ref, mm_ref, wo_ref, g_ref, wgu_ref, wd_ref, o_ref, u_ref) = refs
        half = ma_ref.shape[-1]
        y = _dot(ma_ref[...].astype(BF16), wo_ref[:half, :]) + _dot(mm_ref[...].astype(BF16), wo_ref[half:, :])
        x = x_ref[...] + _rms(y, g_ref[mix_row:mix_row + 1, :])
    else:
        (x_ref, g_ref, wgu_ref, wd_ref, o_ref, u_ref) = refs
        x = x_ref[...]
    h = _rms(x, g_ref[pre_row:pre_row + 1, :]).astype(BF16)
    for c in range(n_chunks):
        gu = _dot(h, wgu_ref[:, 2 * c * FFN_CHUNK:2 * (c + 1) * FFN_CHUNK])
        gate = gu[:, :FFN_CHUNK]
        up = gu[:, FFN_CHUNK:]
        u_ref[:, c * FFN_CHUNK:(c + 1) * FFN_CHUNK] = (gate * jax.nn.sigmoid(gate) * up).astype(BF16)
    y = _dot(u_ref[...], wd_ref[...])
    o_ref[...] = x + 0.5 * _rms(y, g_ref[post_row:post_row + 1, :])


def _ffn(x, gains, gain_rows, wgu, wd, mix=None):
    n, d = x.shape
    f = wd.shape[0]
    n_chunks = f // FFN_CHUNK
    tm = TOKEN_TILE
    row = lambda w: pl.BlockSpec((tm, w), lambda i: (i, 0))
    in_specs = [row(d)]
    args = [x]
    if mix is not None:
        ma, mm, wo = mix
        in_specs += [row(ma.shape[1]), row(mm.shape[1]), _resident(wo)]
        args += [ma, mm, wo]
    in_specs += [_resident(gains), _resident(wgu), _resident(wd)]
    args += [gains, wgu, wd]
    return pl.pallas_call(
        functools.partial(_ffn_kernel, with_mix=mix is not None, n_chunks=n_chunks, gain_rows=gain_rows),
        grid=(n // tm,),
        in_specs=in_specs,
        out_specs=row(d),
        out_shape=jax.ShapeDtypeStruct((n, d), F32),
        scratch_shapes=[pltpu.VMEM((tm, f), BF16)],
        compiler_params=_params("parallel"),
        name="ffn_mix" if mix is not None else "ffn",
    )(*args)


def _proj_kernel(x_ref, g_ref, w_ref, *rest, head_major, width, gate_chunk, n_later):
    later_weights, outs = rest[:n_later], rest[n_later:]
    (qa_ref, kaf_ref, vaf_ref, ka_ref, va_ref, qm_ref, km_ref, vm_ref, om_ref, gt_ref, gtt_ref) = outs[:11]
    wgu_o, wd_o, wo_o = outs[11:] if later_weights else (None, None, None)
    xn = _rms(x_ref[...], g_ref[...]).astype(BF16)
    tm = x_ref.shape[0]
    n_heads = width // LANES

    def group(i):
        return _dot(xn, w_ref[:, i * width:(i + 1) * width])

    def put(ref, val):
        if head_major:
            for h in range(n_heads):
                ref[h] = val[:, h * LANES:(h + 1) * LANES].astype(ref.dtype)
        else:
            ref[...] = val.astype(ref.dtype)

    def put_rows(ref, val):
        for h in range(n_heads):
            ref[pl.ds(h, tm, stride=n_heads), :] = val[:, h * LANES:(h + 1) * LANES]

    put(qa_ref, group(0) * (ATT_DQ ** -0.5 * LOG2E))
    ka = group(1)
    put_rows(kaf_ref, ka)
    put(ka_ref, ka)
    va = group(2)
    put_rows(vaf_ref, va)
    put(va_ref, va)
    put(qm_ref, group(3))
    put(km_ref, group(4) * (LANES ** -0.5))
    put(vm_ref, group(5))
    om_ref[...] = group(6)
    gates = _dot(xn, w_ref[:, 7 * width:7 * width + LANES])
    gt_ref[...] = gates
    gates_t = gates.T[:2 * H_ML, :]
    for c in range(tm // gate_chunk):
        gtt_ref[c] = gates_t[:, c * gate_chunk:(c + 1) * gate_chunk]

    if later_weights:
        wg_ref, wu_ref, wd_ref, wo_ref = later_weights
        _gate_up_kernel(wg_ref, wu_ref, wgu_o)
        wd_o[...] = wd_ref[...].astype(BF16)
        wo_o[...] = wo_ref[...].astype(BF16)


def _rows_per_step(total, steps):
    share = 1
    while steps % share or total % (steps // share) or (total // (steps // share)) % (2 * SUBLANES):
        share += 1
    return total // (steps // share), share


def _proj(x, gain, w_in_p, head_major, gate_chunk, later=None):
    n, d = x.shape
    width = H_ATT * LANES
    n_heads = width // LANES
    tm = TOKEN_TILE
    row = lambda w: pl.BlockSpec((tm, w), lambda i: (i, 0))
    if head_major:
        hm_spec = pl.BlockSpec((n_heads, tm, LANES), lambda i: (0, i, 0))
        hm_shape = jax.ShapeDtypeStruct((n_heads, n, LANES), BF16)
    else:
        hm_spec = row(width)
        hm_shape = jax.ShapeDtypeStruct((n, width), F32)
    f32_shape = jax.ShapeDtypeStruct((n, width), F32)
    rows_spec = pl.BlockSpec((tm * n_heads, LANES), lambda i: (i, 0))
    rows_shape = jax.ShapeDtypeStruct((n * n_heads, LANES), F32)
    out_specs = [hm_spec, rows_spec, rows_spec, hm_spec, hm_spec, hm_spec, hm_spec, hm_spec, row(width),
                 row(LANES), pl.BlockSpec((tm // gate_chunk, 2 * H_ML, gate_chunk), lambda i: (i, 0, 0))]
    out_shape = [hm_shape, rows_shape, rows_shape, hm_shape, hm_shape, hm_shape, hm_shape, hm_shape, f32_shape,
                 jax.ShapeDtypeStruct((n, LANES), F32),
                 jax.ShapeDtypeStruct((n // gate_chunk, 2 * H_ML, gate_chunk), F32)]
    in_specs = [row(d), _resident(gain), _resident(w_in_p, 0)]
    args = [x, gain, w_in_p]
    if later is not None:
        wg3, wu3, wd3, group, wo3, layer = later
        steps = n // tm
        f = wg3.shape[-1]
        r_gu, s_gu = _rows_per_step(d, steps)
        r_d, s_d = _rows_per_step(f, steps)
        gu_spec = pl.BlockSpec((None, r_gu, f), lambda i: (group, i // s_gu, 0))
        in_specs += [gu_spec, gu_spec, pl.BlockSpec((None, r_d, d), lambda i: (group, i // s_d, 0)),
                     pl.BlockSpec((None, r_gu, d), lambda i: (layer, i // s_gu, 0))]
        args += [wg3, wu3, wd3, wo3]
        out_specs += [pl.BlockSpec((r_gu, 2 * f), lambda i: (i // s_gu, 0)),
                      pl.BlockSpec((r_d, d), lambda i: (i // s_d, 0)),
                      pl.BlockSpec((r_gu, d), lambda i: (i // s_gu, 0))]
        out_shape += [jax.ShapeDtypeStruct((d, 2 * f), BF16), jax.ShapeDtypeStruct((f, d), BF16),
                      jax.ShapeDtypeStruct((d, d), BF16)]
    return pl.pallas_call(
        functools.partial(_proj_kernel, head_major=head_major, width=width, gate_chunk=gate_chunk,
                          n_later=0 if later is None else 4),
        grid=(n // tm,),
        in_specs=in_specs,
        out_specs=out_specs,
        out_shape=out_shape,
        compiler_params=_params("arbitrary" if later is not None else "parallel"),
        name="proj_prompt" if head_major else "proj_sample",
    )(*args)


def _t5_bias(rel, rb_ref, h):
    n = jnp.maximum(rel, 0)
    max_exact = NUM_BUCKETS // 2
    x = (jnp.log(jnp.maximum(n, max_exact).astype(F32) / max_exact)
         / math.log(MAX_DISTANCE / max_exact) * (NUM_BUCKETS - max_exact))
    out = jnp.full(rel.shape, rb_ref[max_exact, h], F32)
    for i in range(1, NUM_BUCKETS - max_exact):
        out = jnp.where(x >= i, rb_ref[max_exact + i, h], out)
    for i in range(max_exact):
        out = jnp.where(n == i, rb_ref[i, h], out)
    return (out - rb_ref[NUM_BUCKETS - 1, h]) * LOG2E


def _prompt_bias_kernel(rb_ref, o_ref, *, tile):
    r = lax.broadcasted_iota(jnp.int32, (tile, tile), 0)
    c = lax.broadcasted_iota(jnp.int32, (tile, tile), 1)
    for h in range(H_ATT):
        o_ref[h, 0] = jnp.where(c <= r, _t5_bias(r - c, rb_ref, h), -jnp.inf)
        o_ref[h, 1] = _t5_bias(tile + r - c, rb_ref, h)


def _prompt_bias(rel_bias, tile):
    return pl.pallas_call(
        functools.partial(_prompt_bias_kernel, tile=tile),
        in_specs=[pl.BlockSpec(memory_space=pltpu.SMEM)],
        out_shape=jax.ShapeDtypeStruct((H_ATT, 2, tile, tile), F32),
        name="prompt_bias",
    )(rel_bias)


def _sample_bias_kernel(rb_ref, o_ref, *, page, valid):
    qi = lax.broadcasted_iota(jnp.int32, (SUBLANES, 2 * page), 0)
    col = lax.broadcasted_iota(jnp.int32, (SUBLANES, 2 * page), 1)
    kj = col - page
    for h in range(H_ATT):
        past = _t5_bias(page + qi - col, rb_ref, h)
        new = jnp.where((kj <= qi) & (kj < valid), _t5_bias(qi - kj, rb_ref, h), -jnp.inf)
        tile = jnp.where(col < page, past, new)
        for comp in range(2):
            o_ref[(comp * H_ATT + h) * SUBLANES:(comp * H_ATT + h + 1) * SUBLANES, :] = tile


def _sample_bias(rel_bias, page, valid):
    return pl.pallas_call(
        functools.partial(_sample_bias_kernel, page=page, valid=valid),
        in_specs=[pl.BlockSpec(memory_space=pltpu.SMEM)],
        out_shape=jax.ShapeDtypeStruct((2 * H_ATT * SUBLANES, 2 * page), F32),
        name="sample_bias",
    )(rel_bias)


def _lambda(lq_ref, lk_ref, lam_init):
    prod = lq_ref[...] * lk_ref[...]
    return (jnp.exp(jnp.sum(prod[0:1, :], axis=-1, keepdims=True))
            - jnp.exp(jnp.sum(prod[1:2, :], axis=-1, keepdims=True)) + lam_init)


def _attn_kernel(q_ref, k_ref, v_ref, bias_ref, lq_ref, lk_ref, gn_ref, o_ref, vx_scr, *, tile, n_q, lam_init):
    dv = v_ref.shape[-1]
    vx_scr[:, :dv] = v_ref[0]
    ones_lane = lax.broadcasted_iota(jnp.int32, (vx_scr.shape[0], vx_scr.shape[1] - dv), 1)
    vx_scr[:, dv:] = jnp.where(ones_lane == 0, 1.0, 0.0).astype(BF16)

    lane = lax.broadcasted_iota(jnp.int32, (tile, dv), 1)
    zero = jnp.zeros((tile, dv), BF16)
    lam = _lambda(lq_ref, lk_ref, lam_init)
    diag = jnp.concatenate([bias_ref[0, 0]] * 2, axis=0)
    near = jnp.concatenate([bias_ref[0, 1]] * 2, axis=0)

    def logits(n):
        q = q_ref[0, n * tile:(n + 1) * tile, :]
        qq = jnp.concatenate([jnp.where(lane < ATT_DQ, q, zero), jnp.where(lane >= ATT_DQ, q, zero)], axis=0)
        n_keys = (n + 1) * tile
        s = _dot_nt(qq, k_ref[0, :n_keys, :])
        parts = [s[:, n_keys - tile:] + diag]
        if n >= 1:
            parts.insert(0, s[:, n_keys - 2 * tile:n_keys - tile] + near)
        if n >= 2:
            parts.insert(0, s[:, :n_keys - 2 * tile])
        return parts

    def probs(parts):
        m = functools.reduce(jnp.maximum, [jnp.max(x, axis=-1, keepdims=True) for x in parts])
        return jnp.concatenate([jnp.exp2(x - m) for x in parts], axis=1).astype(BF16)

    def finish(n, p):
        vx = vx_scr[:(n + 1) * tile, :]
        acc0 = _dot(p[:tile], vx)
        acc1 = _dot(p[tile:], vx)
        out = acc0[:, :dv] / acc0[:, dv:dv + 1] - lam * (acc1[:, :dv] / acc1[:, dv:dv + 1])
        o_ref[n * tile:(n + 1) * tile, :] = (_rms(out, gn_ref[...]) * (1.0 - lam_init)).astype(o_ref.dtype)

    for g in range(0, n_q // 2, ATT_GROUP_PAIRS):
        group = [n for gg in range(g, min(g + ATT_GROUP_PAIRS, n_q // 2)) for n in (gg, n_q - 1 - gg)]
        p_group = [probs(logits(n)) for n in group]
        for n, p in zip(group, p_group):
            finish(n, p)


def _prompt_attention(q, k, v, bias, lam_q, lam_k, gain, batch, lam_init):
    h, n, dv = q.shape
    seq = n // batch
    tile = ATT_TILE
    nq = seq // tile
    assert nq % 2 == 0
    seq_spec = pl.BlockSpec((1, seq, dv), lambda b, hh: (hh, b, 0))
    return pl.pallas_call(
        functools.partial(_attn_kernel, tile=tile, n_q=nq, lam_init=lam_init),
        grid=(batch, h),
        in_specs=[seq_spec, seq_spec, seq_spec,
                  pl.BlockSpec((1, 2, tile, tile), lambda b, hh: (hh, 0, 0, 0)),
                  pl.BlockSpec(lam_q.shape, lambda b, hh: (0, 0)),
                  pl.BlockSpec(lam_k.shape, lambda b, hh: (0, 0)),
                  pl.BlockSpec(gain.shape, lambda b, hh: (0, 0))],
        out_specs=pl.BlockSpec((seq, dv), lambda b, hh: (b, hh)),
        out_shape=jax.ShapeDtypeStruct((n, h * dv), BF16),
        scratch_shapes=[pltpu.VMEM((seq, 2 * dv), BF16)],
        compiler_params=_params("parallel", "parallel"),
        name="attn_prompt",
    )(q, k, v, bias, lam_q, lam_k, gain)


def _sample_attn_core(q, load_k, load_v, load_kn, load_vn, bias_ref, lam, gn_ref, k_buf, v_buf,
                      *, n_pages, page, n_tok_new, lam_init):
    width = q.shape[-1]
    n_rows = 2 * H_ATT * SUBLANES
    qt = jnp.concatenate([q] * (2 * H_ATT), axis=0)
    r = lax.broadcasted_iota(jnp.int32, (n_rows, width), 0)
    c = lax.broadcasted_iota(jnp.int32, (n_rows, width), 1)
    comp = r // (H_ATT * SUBLANES)
    head = (r // SUBLANES) % H_ATT
    own = (c // LANES == head) & ((c // ATT_DQ) % 2 == comp)
    qbd = jnp.where(own, qt, 0.0).astype(BF16)

    n_past = n_pages * page
    new_rows = 2 * SUBLANES
    for load, load_new, buf in ((load_k, load_kn, k_buf), (load_v, load_vn, v_buf)):
        for p in range(n_pages):
            for h in range(H_ATT):
                buf[p * page:(p + 1) * page, h * LANES:(h + 1) * LANES] = load(p, h).astype(BF16)
        new = jnp.concatenate([load_new(h) for h in range(H_ATT)], axis=1)
        new = jnp.concatenate([new, jnp.zeros((new_rows - n_tok_new, width), F32)], axis=0)
        buf[n_past:n_past + new_rows, :] = new.astype(BF16)
        buf[n_past + new_rows:, :] = jnp.zeros((page - new_rows, width), BF16)

    s = _dot_nt(qbd, k_buf[...])
    parts = [s[:, :n_past - page], s[:, n_past - page:] + bias_ref[...]]
    m = functools.reduce(jnp.maximum, [jnp.max(x, axis=-1, keepdims=True) for x in parts])
    p = jnp.concatenate([jnp.exp2(x - m) for x in parts], axis=1)
    inv = 1.0 / jnp.sum(p, axis=-1, keepdims=True)
    half = n_rows // 2
    a = (p[:half] * inv[:half] - lam * (p[half:] * inv[half:])).astype(BF16)
    out = _dot(a, v_buf[...])
    heads = [_rms(out[h * SUBLANES:(h + 1) * SUBLANES, h * LANES:(h + 1) * LANES], gn_ref[...])
             * (1.0 - lam_init) for h in range(H_ATT)]
    return jnp.concatenate(heads, axis=1)


def _ffn_attn_kernel(pt_ref, x_ref, g_ref, wgu_ref, wd_ref, q_ref, kn_ref, vn_ref, bias_ref, lq_ref, lk_ref,
                     gn_ref, ck_hbm, cv_hbm, o_ref, ao_ref, u_ref, k_ring, v_ring, k_buf, v_buf, sem,
                     *, n_chunks, gain_rows, n_pages, page, first_page, n_seq, lam_init):
    _, pre_row, post_row = gain_rows
    i = pl.program_id(0)
    seqs = q_ref.shape[0]
    page_rows = page * H_ATT
    new_rows = kn_ref.shape[0] // seqs
    n_tok_new = new_rows // H_ATT

    def page_copies(g, slot):
        copies = []
        for p in range(n_pages):
            src = pl.ds(pl.multiple_of((first_page + pt_ref[g, p]) * page_rows, page_rows), page_rows)
            dst = pl.ds(p * page_rows, page_rows)
            copies.append(pltpu.make_async_copy(ck_hbm.at[src, :], k_ring.at[slot, dst, :], sem.at[slot, 0]))
            copies.append(pltpu.make_async_copy(cv_hbm.at[src, :], v_ring.at[slot, dst, :], sem.at[slot, 1]))
        return copies

    @pl.when(i == 0)
    def _():
        for cp in page_copies(0, 0):
            cp.start()

    x = x_ref[...]
    h = _rms(x, g_ref[pre_row:pre_row + 1, :]).astype(BF16)
    lam = _lambda(lq_ref, lk_ref, lam_init)
    chunk_bounds = [(n_chunks * j) // seqs for j in range(seqs + 1)]
    for j in range(seqs):
        g = i * seqs + j
        slot = j % 2

        for cp in page_copies(jnp.minimum(g + 1, n_seq - 1), 1 - slot):
            cp.start()

        for c in range(chunk_bounds[j], chunk_bounds[j + 1]):
            gu = _dot(h, wgu_ref[:, 2 * c * FFN_CHUNK:2 * (c + 1) * FFN_CHUNK])
            gate = gu[:, :FFN_CHUNK]
            up = gu[:, FFN_CHUNK:]
            u_ref[:, c * FFN_CHUNK:(c + 1) * FFN_CHUNK] = (gate * jax.nn.sigmoid(gate) * up).astype(BF16)

        for cp in page_copies(g, slot):
            cp.wait()

        def past(ring, p, hh, slot=slot):
            return ring[slot, pl.ds(p * page_rows + hh, page, stride=H_ATT), :]

        def new(ref, hh, j=j):
            return ref[pl.ds(j * new_rows + hh, n_tok_new, stride=H_ATT), :]

        ao_ref[j] = _sample_attn_core(
            q_ref[j], functools.partial(past, k_ring), functools.partial(past, v_ring),
            functools.partial(new, kn_ref), functools.partial(new, vn_ref), bias_ref, lam, gn_ref, k_buf, v_buf,
            n_pages=n_pages, page=page, n_tok_new=n_tok_new, lam_init=lam_init)

    y = _dot(u_ref[...], wd_ref[...])
    o_ref[...] = x + 0.5 * _rms(y, g_ref[post_row:post_row + 1, :])

    @pl.when(i == pl.num_programs(0) - 1)
    def _():
        for cp in page_copies(n_seq - 1, seqs % 2):
            cp.wait()


def _ffn_with_sample_attention(x, gains, gain_rows, wgu, wd, page_table, q, k_new, v_new, cache_k, cache_v,
                               page, first_page, bias, lam_q, lam_k, gain, lam_init):
    n, d = x.shape
    f = wd.shape[0]
    tm = TOKEN_TILE
    steps = n // tm
    nb, rows, width = q.shape
    n_pages = page_table.shape[1]
    n_heads = width // LANES
    seqs = nb // steps
    assert nb == seqs * steps and seqs % 2 == 0
    ring = pltpu.VMEM((2, n_pages * page * n_heads, LANES), F32)
    stage = pltpu.VMEM(((n_pages + 1) * page, width), BF16)
    full = lambda a: pl.BlockSpec(a.shape, lambda i, pt: (0,) * a.ndim)
    seq_spec = pl.BlockSpec((seqs, rows, width), lambda i, pt: (i, 0, 0))
    new_spec = pl.BlockSpec((seqs * rows * n_heads, LANES), lambda i, pt: (i, 0))
    row_spec = pl.BlockSpec((tm, d), lambda i, pt: (i, 0))
    grid_spec = pltpu.PrefetchScalarGridSpec(
        num_scalar_prefetch=1,
        grid=(steps,),
        in_specs=[row_spec, _resident(gains), _resident(wgu), _resident(wd),
                  seq_spec, new_spec, new_spec, full(bias), full(lam_q), full(lam_k), full(gain),
                  pl.BlockSpec(memory_space=pl.ANY), pl.BlockSpec(memory_space=pl.ANY)],
        out_specs=[row_spec, seq_spec],
        scratch_shapes=[pltpu.VMEM((tm, f), BF16), ring, ring, stage, stage, pltpu.SemaphoreType.DMA((2, 2))],
    )
    return pl.pallas_call(
        functools.partial(_ffn_attn_kernel, n_chunks=f // FFN_CHUNK, gain_rows=gain_rows, n_pages=n_pages,
                          page=page, first_page=first_page, n_seq=nb, lam_init=lam_init),
        grid_spec=grid_spec,
        out_shape=[jax.ShapeDtypeStruct((n, d), F32), jax.ShapeDtypeStruct((nb, rows, width), F32)],
        compiler_params=pltpu.CompilerParams(dimension_semantics=("arbitrary",),
                                             vmem_limit_bytes=FUSED_VMEM_LIMIT_BYTES),
        name="ffn_attn_sample",
    )(page_table, x, gains, wgu, wd, q, k_new, v_new, bias, lam_q, lam_k, gain, cache_k, cache_v)


def _log_sigmoid(x):
    return jnp.minimum(x, 0.0) - jnp.log1p(jnp.exp(-jnp.abs(x)))


def _mlstm_step_kernel(q_ref, k_ref, v_ref, g_ref, gt_ref, om_ref, brow_ref, bcol_ref, gn_ref,
                       c0_ref, n0_ref, m0_ref, hm_ref, c_ref, n_ref, m_ref, *, rows_per_seq, valid):
    n_seq = c0_ref.shape[0]
    P = rows_per_seq
    R = n_seq * P
    r_i = lax.broadcasted_iota(jnp.int32, (R, R), 0)
    c_i = lax.broadcasted_iota(jnp.int32, (R, R), 1)
    same = (r_i // P) == (c_i // P)
    causal = same & (c_i <= r_i)
    hi = lax.Precision.HIGHEST

    g_col = g_ref[...] + brow_ref[...]
    g_row = gt_ref[0] + bcol_ref[...]
    lf_col = _log_sigmoid(g_col)
    lf_row = _log_sigmoid(g_row)
    tok_c = lax.broadcasted_iota(jnp.int32, g_col.shape, 0) % P
    tok_r = lax.broadcasted_iota(jnp.int32, g_row.shape, 1) % P
    g_col = jnp.where(tok_c < valid, g_col, -1e30)
    g_row = jnp.where(tok_r < valid, g_row, -1e30)
    lf_col = jnp.where(tok_c < valid, lf_col, 0.0)
    lf_row = jnp.where(tok_r < valid, lf_row, 0.0)
    same_f = same.astype(F32)
    b_col = jnp.dot(causal.astype(F32), lf_col, preferred_element_type=F32, precision=hi)
    b_row = jnp.dot(lf_row, (same & (r_i <= c_i)).astype(F32), preferred_element_type=F32, precision=hi)
    bl_col = jnp.dot(same_f, lf_col, preferred_element_type=F32, precision=hi)
    bl_row = jnp.dot(lf_row, same_f, preferred_element_type=F32, precision=hi)

    for h in range(H_ML):
        cols = slice(h * LANES, (h + 1) * LANES)
        q = q_ref[:, cols]
        k = k_ref[:, cols]
        v = v_ref[:, cols]
        bc = b_col[:, H_ML + h:H_ML + h + 1]
        ic = g_col[:, h:h + 1]
        br = b_row[H_ML + h:H_ML + h + 1, :]
        ir = g_row[h:h + 1, :]
        blc = bl_col[:, H_ML + h:H_ML + h + 1]
        blr = bl_row[H_ML + h:H_ML + h + 1, :]
        m = m0_ref[:, h:h + 1]

        log_d = jnp.where(causal, bc - br + ir, -jnp.inf)
        inter = bc + m
        m_t = jnp.maximum(inter, jnp.max(log_d, axis=-1, keepdims=True))
        s = _dot_nt(q.astype(BF16), k.astype(BF16)) * jnp.exp(log_d - m_t)
        w_inter = jnp.exp(inter - m_t)
        q_c = jnp.concatenate([_dot(q[j * P:(j + 1) * P].astype(BF16), c0_ref[j, h].astype(BF16))
                               for j in range(n_seq)], axis=0)
        n_rows = jnp.concatenate([jnp.broadcast_to(n0_ref[j, h:h + 1, :], (P, LANES)) for j in range(n_seq)],
                                 axis=0)
        num = _dot(s.astype(BF16), v.astype(BF16)) + w_inter * q_c
        den = jnp.sum(s, axis=-1, keepdims=True) + w_inter * jnp.sum(q * n_rows, axis=-1, keepdims=True)
        hv = num / jnp.maximum(jnp.abs(den), jnp.exp(-m_t))
        hm_ref[:, cols] = _rms(hv, gn_ref[...]) * jax.nn.sigmoid(om_ref[:, cols])

        w_max = jnp.max(jnp.where(same, blr - br + ir, -jnp.inf), axis=-1, keepdims=True)
        m_new = jnp.maximum(blc + m, w_max)
        ws = jnp.exp(blc - bc + ic - m_new)
        fw = jnp.exp(blc + m - m_new)
        wv = ws * v
        wk = ws * k
        for j in range(n_seq):
            rows = slice(j * P, (j + 1) * P)
            fw_j = fw[j * P:j * P + 1, :]
            c_ref[j, h] = fw_j * c0_ref[j, h] + _dot_tn(k[rows].astype(BF16), wv[rows].astype(BF16))
            n_ref[j, h:h + 1, :] = fw_j * n0_ref[j, h:h + 1, :] + jnp.sum(wk[rows], axis=0, keepdims=True)
            m_ref[j, h:h + 1, :] = jnp.broadcast_to(m_new[j * P:j * P + 1, :], (1, LANES))


def _mlstm_step(q, k, v, gates, gates_t, om, b_row, b_col, gain, state_c, state_n, m_rows, *, rows_per_seq,
                valid, n_seq):
    n, width = om.shape
    batch = state_c.shape[0]
    dh = LANES
    rows = n_seq * rows_per_seq
    row = lambda w: pl.BlockSpec((rows, w), lambda b: (b, 0))
    full = lambda a: pl.BlockSpec(a.shape, lambda b: (0,) * a.ndim)
    c_spec = pl.BlockSpec((n_seq, H_ML, dh, dh), lambda b: (b, 0, 0, 0))
    v_spec = pl.BlockSpec((n_seq, H_ML, dh), lambda b: (b, 0, 0))
    return pl.pallas_call(
        functools.partial(_mlstm_step_kernel, rows_per_seq=rows_per_seq, valid=valid),
        grid=(batch // n_seq,),
        in_specs=[row(width), row(width), row(width), row(LANES),
                  pl.BlockSpec((1, 2 * H_ML, rows), lambda b: (b, 0, 0)),
                  row(width), full(b_row), full(b_col), full(gain), c_spec, v_spec, row(LANES)],
        out_specs=[row(width), c_spec, v_spec, v_spec],
        out_shape=[jax.ShapeDtypeStruct((n, width), F32),
                   jax.ShapeDtypeStruct((batch, H_ML, dh, dh), F32),
                   jax.ShapeDtypeStruct((batch, H_ML, dh), F32),
                   jax.ShapeDtypeStruct((batch, H_ML, dh), F32)],
        compiler_params=_params("parallel"),
        name="mlstm_sample",
    )(q, k, v, gates, gates_t, om, b_row, b_col, gain, state_c, state_n, m_rows)


def _mlstm_seq_kernel(q_ref, k_ref, v_ref, g_ref, gt_ref, om_ref, brow_ref, bcol_ref, gn_ref,
                      hm_ref, c_ref, n_ref, m_ref, fcol_scr, vx_scr, *, tile):
    seq = g_ref.shape[0]
    n_t = seq // tile
    dv = v_ref.shape[-1]
    t_i = lax.broadcasted_iota(jnp.int32, (tile, tile), 0)
    s_i = lax.broadcasted_iota(jnp.int32, (tile, tile), 1)
    causal = s_i <= t_i
    lower = causal.astype(F32)
    upper = (t_i <= s_i).astype(F32)

    off_col = jnp.zeros((1, LANES), F32)
    off_row = jnp.zeros((2 * H_ML, 1), F32)
    u_tiles = []
    for j in range(n_t):
        rows = slice(j * tile, (j + 1) * tile)
        lf_col = _log_sigmoid(g_ref[rows, :] + brow_ref[...])
        f_col = jnp.dot(lower, lf_col, preferred_element_type=F32, precision=lax.Precision.HIGHEST) + off_col
        fcol_scr[rows, :] = f_col
        off_col = f_col[tile - 1:tile, :]
        g_row = gt_ref[j] + bcol_ref[...]
        f_row = jnp.dot(_log_sigmoid(g_row), upper, preferred_element_type=F32,
                        precision=lax.Precision.HIGHEST) + off_row
        off_row = f_row[:, tile - 1:tile]
        u_tiles.append((g_row[:H_ML, :] - f_row[H_ML:, :]) * LOG2E)

    ones_lane = lax.broadcasted_iota(jnp.int32, (seq, vx_scr.shape[2] - dv), 1)
    for h in range(H_ML):
        vx_scr[h, :, :dv] = v_ref[h]
        vx_scr[h, :, dv:] = jnp.where(ones_lane == 0, 1.0, 0.0).astype(BF16)

    for h in range(H_ML):
        u_row = jnp.concatenate([u[h:h + 1, :] for u in u_tiles], axis=1)
        tile_max = [jnp.max(u[h:h + 1, :], axis=-1, keepdims=True) for u in u_tiles]
        before = [None]
        for j in range(n_t):
            before.append(tile_max[j] if before[-1] is None else jnp.maximum(before[-1], tile_max[j]))

        def weights(n):
            n_keys = (n + 1) * tile
            s = _dot_nt(q_ref[h, n * tile:(n + 1) * tile, :], k_ref[h, :n_keys, :])
            u_diag = jnp.where(causal, u_row[:, n_keys - tile:n_keys], -jnp.inf)
            r = jnp.max(u_diag, axis=-1, keepdims=True)
            if before[n] is not None:
                r = jnp.maximum(r, before[n])
            r0 = jnp.maximum(r, 0.0)
            parts = [jnp.exp2(u_diag - r0)]
            if n >= 1:
                parts.insert(0, jnp.exp2(u_row[:, :n_keys - tile] - r0))
            return (s * jnp.concatenate(parts, axis=1)).astype(BF16), r0 * (1.0 / LOG2E)

        def finish(n, p, r0):
            rows = slice(n * tile, (n + 1) * tile)
            acc = _dot(p, vx_scr[h, :(n + 1) * tile, :])
            m_t = fcol_scr[rows, H_ML + h:H_ML + h + 1] + r0
            hv = acc[:, :dv] / jnp.maximum(jnp.abs(acc[:, dv:dv + 1]), jnp.exp(-m_t))
            gate = jax.nn.sigmoid(om_ref[rows, h * LANES:(h + 1) * LANES])
            hm_ref[rows, h * LANES:(h + 1) * LANES] = (_rms(hv, gn_ref[...]) * gate).astype(hm_ref.dtype)

        for g in range(n_t // 2):
            pair = (g, n_t - 1 - g)
            done = [weights(n) for n in pair]
            for n, (p, r0) in zip(pair, done):
                finish(n, p, r0)

        r0_last = jnp.maximum(before[n_t], 0.0) * (1.0 / LOG2E)
        u_col = (g_ref[:, h:h + 1] + brow_ref[:, h:h + 1]) - fcol_scr[:, H_ML + h:H_ML + h + 1]
        w_col = jnp.exp(u_col - r0_last)
        kf = k_ref[h]
        c_ref[0, h] = _dot_tn(kf, (w_col * v_ref[h].astype(F32)).astype(BF16))
        n_ref[0, h:h + 1, :] = jnp.sum(w_col * kf.astype(F32), axis=0, keepdims=True)
        m_last = off_col[:, H_ML + h:H_ML + h + 1] + r0_last
        m_ref[0, h:h + 1, :] = jnp.broadcast_to(m_last, (1, LANES))


def _mlstm_seq(q, k, v, gates, gates_t, om, b_row, b_col, gain, *, batch, tile):
    n, width = om.shape
    seq = n // batch
    n_t = seq // tile
    assert n_t % 2 == 0
    dh = LANES
    qkv_spec = pl.BlockSpec((H_ML, seq, dh), lambda b: (0, b, 0))
    row = lambda w: pl.BlockSpec((seq, w), lambda b: (b, 0))
    full = lambda a: pl.BlockSpec(a.shape, lambda b: (0,) * a.ndim)
    c_spec = pl.BlockSpec((1, H_ML, dh, dh), lambda b: (b, 0, 0, 0))
    v_spec = pl.BlockSpec((1, H_ML, dh), lambda b: (b, 0, 0))
    return pl.pallas_call(
        functools.partial(_mlstm_seq_kernel, tile=tile),
        grid=(batch,),
        in_specs=[qkv_spec, qkv_spec, qkv_spec, row(LANES),
                  pl.BlockSpec((n_t, 2 * H_ML, tile), lambda b: (b, 0, 0)),
                  row(width), full(b_row), full(b_col), full(gain)],
        out_specs=[row(width), c_spec, v_spec, v_spec],
        out_shape=[jax.ShapeDtypeStruct((n, width), BF16),
                   jax.ShapeDtypeStruct((batch, H_ML, dh, dh), F32),
                   jax.ShapeDtypeStruct((batch, H_ML, dh), F32),
                   jax.ShapeDtypeStruct((batch, H_ML, dh), F32)],
        scratch_shapes=[pltpu.VMEM((seq, LANES), F32), pltpu.VMEM((H_ML, seq, 2 * dh), BF16)],
        compiler_params=_params("parallel"),
        name="mlstm_prompt",
    )(q, k, v, gates, gates_t, om, b_row, b_col, gain)


def kernel(x_prompt, x_sample, cache_k, cache_v, state_C, state_n, state_m, page_table, rel_bias, norm_gains,
           ffn_w_gate, ffn_w_up, ffn_w_down, w_in, b_gates, lam_q, lam_k, attn_norm, mlstm_norm, w_out):
    batch, seq, d = x_prompt.shape
    dec_batch, dec_seq, _ = x_sample.shape
    depth = w_in.shape[0]
    page = cache_k.shape[2]
    width = H_ATT * LANES
    d_in = w_in.shape[-1]
    n_gate = 2 * H_ML
    assert d_in == 7 * width + n_gate and dec_seq <= SAMPLE_ROWS
    assert seq % ATT_TILE == 0 and seq % MLSTM_TILE == 0 and (batch * seq) % TOKEN_TILE == 0

    xp = x_prompt.reshape(batch * seq, d)
    xs = jnp.pad(x_sample, ((0, 0), (0, SAMPLE_ROWS - dec_seq), (0, 0))).reshape(dec_batch * SAMPLE_ROWS, d)

    prompt_bias = _prompt_bias(rel_bias, ATT_TILE)
    sample_bias = _sample_bias(rel_bias, page, dec_seq)

    outs = [[] for _ in range(10)]
    for l in range(depth):
        lam_init = 0.8 - 0.6 * math.exp(-0.3 * l)
        g = norm_gains[l]
        f = ffn_w_gate.shape[-1]
        wg3, wu3, wd3 = ffn_w_gate.reshape(-1, d, f), ffn_w_up.reshape(-1, d, f), ffn_w_down.reshape(-1, f, d)
        wgu1 = _fuse_gate_up(wg3, wu3, 2 * l, 1, WEIGHT_ROW_TILE)[0]
        wd1 = _cast_groups(wd3, 2 * l, 1, d, f // 4)[0]
        w_in_p = _cast_groups(w_in, l, 1, d_in - n_gate + LANES, WEIGHT_ROW_TILE)
        b_row = jnp.pad(b_gates[l].reshape(1, n_gate), ((0, 0), (0, LANES - n_gate)))
        b_col = b_gates[l].reshape(n_gate, 1)
        gn_att = attn_norm[l].reshape(1, -1)
        gn_ml = mlstm_norm[l].reshape(1, -1)
        ck = cache_k.reshape(-1, LANES)
        cv = cache_v.reshape(-1, LANES)
        first_page = l * cache_k.shape[1]

        xs = _ffn(xs, g, (None, 0, 1), wgu1, wd1)
        sample_seqs = math.gcd(dec_batch, MLSTM_SAMPLE_SEQS)
        (qa_s, kaf_s, vaf_s, _, _, qm_s, km_s, vm_s, om_s, gt_s, gtt_s) = _proj(xs, g[2:3], w_in_p, False,
                                                                                sample_seqs * SAMPLE_ROWS)
        seq3 = lambda a: a.reshape(dec_batch, SAMPLE_ROWS, width)
        xp, mix_a_s = _ffn_with_sample_attention(xp, g, (None, 0, 1), wgu1, wd1, page_table, seq3(qa_s), kaf_s,
                                                 vaf_s, ck, cv, page, first_page, sample_bias, lam_q[l], lam_k[l],
                                                 gn_att, lam_init)

        (qa, kaf, vaf, ka, va, qm, km, vm, om, gt, gtt, wgu2, wd2, wo) = _proj(
            xp, g[2:3], w_in_p, True, MLSTM_TILE, later=(wg3, wu3, wd3, 2 * l + 1, w_out, l))
        mix_a = _prompt_attention(qa, ka, va, prompt_bias, lam_q[l], lam_k[l], gn_att, batch, lam_init)
        mix_m, c_p, n_p, m_p = _mlstm_seq(qm, km, vm, gt, gtt, om, b_row, b_col, gn_ml, batch=batch, tile=MLSTM_TILE)
        xp = _ffn(xp, g, (3, 4, 5), wgu2, wd2, mix=(mix_a, mix_m, wo))

        m_rows = jnp.pad(jnp.repeat(state_m[l], SAMPLE_ROWS, axis=0), ((0, 0), (0, LANES - H_ML)))
        mix_m_s, c_s, n_s, m_s = _mlstm_step(qm_s, km_s, vm_s, gt_s, gtt_s, om_s, b_row, b_col, gn_ml,
                                             state_C[l], state_n[l], m_rows, rows_per_seq=SAMPLE_ROWS,
                                             valid=dec_seq, n_seq=sample_seqs)
        xs = _ffn(xs, g, (3, 4, 5), wgu2, wd2, mix=(mix_a_s.reshape(-1, width), mix_m_s, wo))

        new = lambda a: a.reshape(dec_batch, SAMPLE_ROWS, H_ATT, LANES)[:, :dec_seq]
        for lst, val in zip(outs, (kaf.reshape(batch, seq, H_ATT, LANES), vaf.reshape(batch, seq, H_ATT, LANES),
                                   c_p, n_p, m_p[:, :, 0], new(kaf_s), new(vaf_s), c_s, n_s, m_s[:, :, 0])):
            lst.append(val)

    y_prompt = xp.reshape(batch, seq, d)
    y_sample = xs.reshape(dec_batch, SAMPLE_ROWS, d)[:, :dec_seq]
    return (y_prompt, y_sample) + tuple(jnp.stack(o) for o in outs)
```

```python
import functools
import math

import jax
import jax.numpy as jnp
from jax import lax
from jax.experimental import pallas as pl
from jax.experimental.pallas import tpu as pltpu

F32 = jnp.float32
BF16 = jnp.bfloat16

H_ATT = 4
ATT_DQ = 64
H_ML = 4
NUM_BUCKETS = 32
MAX_DISTANCE = 128
EPS = 1e-6
LOG2E = math.log2(math.e)

LANES = 128
SUBLANES = 8
VMEM_LIMIT_BYTES = 56 * 1024 * 1024
FUSED_VMEM_LIMIT_BYTES = 60 * 1024 * 1024
PAGE_RING_SLOTS = 2

TOKEN_TILE = 512
FFN_CHUNK = 256
ATT_TILE = 256
ATT_GROUP_PAIRS = 2
SAMPLE_ROWS = 8
WEIGHT_ROW_TILE = 256
MLSTM_TILE = 256
MLSTM_SAMPLE_SEQS = 16


def _params(*sem):
    return pltpu.CompilerParams(dimension_semantics=sem, vmem_limit_bytes=VMEM_LIMIT_BYTES)


def _rms(x, g):
    return x * lax.rsqrt(jnp.mean(x * x, axis=-1, keepdims=True) + EPS) * g


def _dot(a, b):
    return jnp.dot(a, b, preferred_element_type=F32)


def _dot_nt(a, b):
    return lax.dot_general(a, b, (((1,), (1,)), ((), ())), preferred_element_type=F32)


def _dot_tn(a, b):
    return lax.dot_general(a, b, (((0,), (0,)), ((), ())), preferred_element_type=F32)


def _resident(arr, lead=None):
    if lead is None:
        return pl.BlockSpec(arr.shape, lambda *_: (0,) * arr.ndim, pipeline_mode=pl.Buffered(1))
    return pl.BlockSpec((None,) + arr.shape[1:], lambda *_: (lead,) + (0,) * (arr.ndim - 1),
                        pipeline_mode=pl.Buffered(1))


def _cast_kernel(x_ref, o_ref):
    w = x_ref.shape[-1]
    if o_ref.shape[-1] != w:
        o_ref[...] = jnp.zeros(o_ref.shape, o_ref.dtype)
    o_ref[:, :w] = x_ref[...].astype(o_ref.dtype)


def _cast_groups(x3, first, count, cols_out, row_tile):
    _, rows, cols = x3.shape
    return pl.pallas_call(
        _cast_kernel,
        grid=(count, rows // row_tile),
        in_specs=[pl.BlockSpec((None, row_tile, cols), lambda g, r: (first + g, r, 0))],
        out_specs=pl.BlockSpec((None, row_tile, cols_out), lambda g, r: (g, r, 0)),
        out_shape=jax.ShapeDtypeStruct((count, rows, cols_out), BF16),
        compiler_params=_params("parallel", "parallel"),
        name="cast_weights",
    )(x3)


def _gate_up_kernel(g_ref, u_ref, o_ref):
    for c in range(g_ref.shape[-1] // FFN_CHUNK):
        src = slice(c * FFN_CHUNK, (c + 1) * FFN_CHUNK)
        o_ref[:, 2 * c * FFN_CHUNK:(2 * c + 1) * FFN_CHUNK] = g_ref[:, src].astype(BF16)
        o_ref[:, (2 * c + 1) * FFN_CHUNK:(2 * c + 2) * FFN_CHUNK] = u_ref[:, src].astype(BF16)


def _fuse_gate_up(wg3, wu3, first, count, row_tile):
    _, rows, cols = wg3.shape
    spec = pl.BlockSpec((None, row_tile, cols), lambda g, r: (first + g, r, 0))
    return pl.pallas_call(
        _gate_up_kernel,
        grid=(count, rows // row_tile),
        in_specs=[spec, spec],
        out_specs=pl.BlockSpec((None, row_tile, 2 * cols), lambda g, r: (g, r, 0)),
        out_shape=jax.ShapeDtypeStruct((count, rows, 2 * cols), BF16),
        compiler_params=_params("parallel", "parallel"),
        name="fuse_gate_up",
    )(wg3, wu3)


def _ffn_kernel(*refs, with_mix, n_chunks, gain_rows):
    mix_row, pre_row, post_row = gain_rows
    if with_mix:
        (x_ref, ma_ref, mm_ref, wo_ref, g_ref, wgu_ref, wd_ref, o_ref, u_ref) = refs
        half = ma_ref.shape[-1]
        y = _dot(ma_ref[...].astype(BF16), wo_ref[:half, :]) + _dot(mm_ref[...].astype(BF16), wo_ref[half:, :])
        x = x_ref[...] + _rms(y, g_ref[mix_row:mix_row + 1, :])
    else:
        (x_ref, g_ref, wgu_ref, wd_ref, o_ref, u_ref) = refs
        x = x_ref[...]
    h = _rms(x, g_ref[pre_row:pre_row + 1, :]).astype(BF16)
    for c in range(n_chunks):
        gu = _dot(h, wgu_ref[:, 2 * c * FFN_CHUNK:2 * (c + 1) * FFN_CHUNK])
        gate = gu[:, :FFN_CHUNK]
        up = gu[:, FFN_CHUNK:]
        u_ref[:, c * FFN_CHUNK:(c + 1) * FFN_CHUNK] = (gate * jax.nn.sigmoid(gate) * up).astype(BF16)
    y = _dot(u_ref[...], wd_ref[...])
    o_ref[...] = x + 0.5 * _rms(y, g_ref[post_row:post_row + 1, :])


def _ffn(x, gains, gain_rows, wgu, wd, mix=None):
    n, d = x.shape
    f = wd.shape[0]
    n_chunks = f // FFN_CHUNK
    tm = TOKEN_TILE
    row = lambda w: pl.BlockSpec((tm, w), lambda i: (i, 0))
    in_specs = [row(d)]
    args = [x]
    if mix is not None:
        ma, mm, wo = mix
        in_specs += [row(ma.shape[1]), row(mm.shape[1]), _resident(wo)]
        args += [ma, mm, wo]
    in_specs += [_resident(gains), _resident(wgu), _resident(wd)]
    args += [gains, wgu, wd]
    return pl.pallas_call(
        functools.partial(_ffn_kernel, with_mix=mix is not None, n_chunks=n_chunks, gain_rows=gain_rows),
        grid=(n // tm,),
        in_specs=in_specs,
        out_specs=row(d),
        out_shape=jax.ShapeDtypeStruct((n, d), F32),
        scratch_shapes=[pltpu.VMEM((tm, f), BF16)],
        compiler_params=_params("parallel"),
        name="ffn_mix" if mix is not None else "ffn",
    )(*args)


def _proj_kernel(x_ref, g_ref, w_ref, *rest, head_major, width, gate_chunk, n_later):
    later_weights, outs = rest[:n_later], rest[n_later:]
    (qa_ref, kaf_ref, vaf_ref, ka_ref, va_ref, qm_ref, km_ref, vm_ref, om_ref, gt_ref, gtt_ref) = outs[:11]
    wgu_o, wd_o, wo_o = outs[11:] if later_weights else (None, None, None)
    xn = _rms(x_ref[...], g_ref[...]).astype(BF16)
    tm = x_ref.shape[0]
    n_heads = width // LANES

    def group(i):
        return _dot(xn, w_ref[:, i * width:(i + 1) * width])

    def put(ref, val):
        if head_major:
            for h in range(n_heads):
                ref[h] = val[:, h * LANES:(h + 1) * LANES].astype(ref.dtype)
        else:
            ref[...] = val.astype(ref.dtype)

    def put_rows(ref, val):
        for h in range(n_heads):
            ref[pl.ds(h, tm, stride=n_heads), :] = val[:, h * LANES:(h + 1) * LANES]

    put(qa_ref, group(0) * (ATT_DQ ** -0.5 * LOG2E))
    ka = group(1)
    put_rows(kaf_ref, ka)
    put(ka_ref, ka)
    va = group(2)
    put_rows(vaf_ref, va)
    put(va_ref, va)
    put(qm_ref, group(3))
    put(km_ref, group(4) * (LANES ** -0.5))
    put(vm_ref, group(5))
    om_ref[...] = group(6)
    gates = _dot(xn, w_ref[:, 7 * width:7 * width + LANES])
    gt_ref[...] = gates
    gates_t = gates.T[:2 * H_ML, :]
    for c in range(tm // gate_chunk):
        gtt_ref[c] = gates_t[:, c * gate_chunk:(c + 1) * gate_chunk]

    if later_weights:
        wg_ref, wu_ref, wd_ref, wo_ref = later_weights
        _gate_up_kernel(wg_ref, wu_ref, wgu_o)
        wd_o[...] = wd_ref[...].astype(BF16)
        wo_o[...] = wo_ref[...].astype(BF16)


def _rows_per_step(total, steps):
    share = 1
    while steps % share or total % (steps // share) or (total // (steps // share)) % (2 * SUBLANES):
        share += 1
    return total // (steps // share), share


def _proj(x, gain, w_in_p, head_major, gate_chunk, later=None):
    n, d = x.shape
    width = H_ATT * LANES
    n_heads = width // LANES
    tm = TOKEN_TILE
    row = lambda w: pl.BlockSpec((tm, w), lambda i: (i, 0))
    if head_major:
        hm_spec = pl.BlockSpec((n_heads, tm, LANES), lambda i: (0, i, 0))
        hm_shape = jax.ShapeDtypeStruct((n_heads, n, LANES), BF16)
    else:
        hm_spec = row(width)
        hm_shape = jax.ShapeDtypeStruct((n, width), F32)
    f32_shape = jax.ShapeDtypeStruct((n, width), F32)
    rows_spec = pl.BlockSpec((tm * n_heads, LANES), lambda i: (i, 0))
    rows_shape = jax.ShapeDtypeStruct((n * n_heads, LANES), F32)
    out_specs = [hm_spec, rows_spec, rows_spec, hm_spec, hm_spec, hm_spec, hm_spec, hm_spec, row(width),
                 row(LANES), pl.BlockSpec((tm // gate_chunk, 2 * H_ML, gate_chunk), lambda i: (i, 0, 0))]
    out_shape = [hm_shape, rows_shape, rows_shape, hm_shape, hm_shape, hm_shape, hm_shape, hm_shape, f32_shape,
                 jax.ShapeDtypeStruct((n, LANES), F32),
                 jax.ShapeDtypeStruct((n // gate_chunk, 2 * H_ML, gate_chunk), F32)]
    in_specs = [row(d), _resident(gain), _resident(w_in_p, 0)]
    args = [x, gain, w_in_p]
    if later is not None:
        wg3, wu3, wd3, group, wo3, layer = later
        steps = n // tm
        f = wg3.shape[-1]
        r_gu, s_gu = _rows_per_step(d, steps)
        r_d, s_d = _rows_per_step(f, steps)
        gu_spec = pl.BlockSpec((None, r_gu, f), lambda i: (group, i // s_gu, 0))
        in_specs += [gu_spec, gu_spec, pl.BlockSpec((None, r_d, d), lambda i: (group, i // s_d, 0)),
                     pl.BlockSpec((None, r_gu, d), lambda i: (layer, i // s_gu, 0))]
        args += [wg3, wu3, wd3, wo3]
        out_specs += [pl.BlockSpec((r_gu, 2 * f), lambda i: (i // s_gu, 0)),
                      pl.BlockSpec((r_d, d), lambda i: (i // s_d, 0)),
                      pl.BlockSpec((r_gu, d), lambda i: (i // s_gu, 0))]
        out_shape += [jax.ShapeDtypeStruct((d, 2 * f), BF16), jax.ShapeDtypeStruct((f, d), BF16),
                      jax.ShapeDtypeStruct((d, d), BF16)]
    return pl.pallas_call(
        functools.partial(_proj_kernel, head_major=head_major, width=width, gate_chunk=gate_chunk,
                          n_later=0 if later is None else 4),
        grid=(n // tm,),
        in_specs=in_specs,
        out_specs=out_specs,
        out_shape=out_shape,
        compiler_params=_params("arbitrary" if later is not None else "parallel"),
        name="proj_prompt" if head_major else "proj_sample",
    )(*args)


def _t5_bias(rel, rb_ref, h):
    n = jnp.maximum(rel, 0)
    max_exact = NUM_BUCKETS // 2
    x = (jnp.log(jnp.maximum(n, max_exact).astype(F32) / max_exact)
         / math.log(MAX_DISTANCE / max_exact) * (NUM_BUCKETS - max_exact))
    out = jnp.full(rel.shape, rb_ref[max_exact, h], F32)
    for i in range(1, NUM_BUCKETS - max_exact):
        out = jnp.where(x >= i, rb_ref[max_exact + i, h], out)
    for i in range(max_exact):
        out = jnp.where(n == i, rb_ref[i, h], out)
    return (out - rb_ref[NUM_BUCKETS - 1, h]) * LOG2E


def _prompt_bias_kernel(rb_ref, o_ref, *, tile):
    r = lax.broadcasted_iota(jnp.int32, (tile, tile), 0)
    c = lax.broadcasted_iota(jnp.int32, (tile, tile), 1)
    for h in range(H_ATT):
        o_ref[h, 0] = jnp.where(c <= r, _t5_bias(r - c, rb_ref, h), -jnp.inf)
        o_ref[h, 1] = _t5_bias(tile + r - c, rb_ref, h)


def _prompt_bias(rel_bias, tile):
    return pl.pallas_call(
        functools.partial(_prompt_bias_kernel, tile=tile),
        in_specs=[pl.BlockSpec(memory_space=pltpu.SMEM)],
        out_shape=jax.ShapeDtypeStruct((H_ATT, 2, tile, tile), F32),
        name="prompt_bias",
    )(rel_bias)


def _sample_bias_kernel(rb_ref, o_ref, *, page, valid):
    qi = lax.broadcasted_iota(jnp.int32, (SUBLANES, 2 * page), 0)
    col = lax.broadcasted_iota(jnp.int32, (SUBLANES, 2 * page), 1)
    kj = col - page
    for h in range(H_ATT):
        past = _t5_bias(page + qi - col, rb_ref, h)
        new = jnp.where((kj <= qi) & (kj < valid), _t5_bias(qi - kj, rb_ref, h), -jnp.inf)
        tile = jnp.where(col < page, past, new)
        for comp in range(2):
            o_ref[(comp * H_ATT + h) * SUBLANES:(comp * H_ATT + h + 1) * SUBLANES, :] = tile


def _sample_bias(rel_bias, page, valid):
    return pl.pallas_call(
        functools.partial(_sample_bias_kernel, page=page, valid=valid),
        in_specs=[pl.BlockSpec(memory_space=pltpu.SMEM)],
        out_shape=jax.ShapeDtypeStruct((2 * H_ATT * SUBLANES, 2 * page), F32),
        name="sample_bias",
    )(rel_bias)


def _lambda(lq_ref, lk_ref, lam_init):
    prod = lq_ref[...] * lk_ref[...]
    return (jnp.exp(jnp.sum(prod[0:1, :], axis=-1, keepdims=True))
            - jnp.exp(jnp.sum(prod[1:2, :], axis=-1, keepdims=True)) + lam_init)


def _attn_kernel(q_ref, k_ref, v_ref, bias_ref, lq_ref, lk_ref, gn_ref, o_ref, vx_scr, *, tile, n_q, lam_init):
    dv = v_ref.shape[-1]
    vx_scr[:, :dv] = v_ref[0]
    ones_lane = lax.broadcasted_iota(jnp.int32, (vx_scr.shape[0], vx_scr.shape[1] - dv), 1)
    vx_scr[:, dv:] = jnp.where(ones_lane == 0, 1.0, 0.0).astype(BF16)

    lane = lax.broadcasted_iota(jnp.int32, (tile, dv), 1)
    zero = jnp.zeros((tile, dv), BF16)
    lam = _lambda(lq_ref, lk_ref, lam_init)
    diag = jnp.concatenate([bias_ref[0, 0]] * 2, axis=0)
    near = jnp.concatenate([bias_ref[0, 1]] * 2, axis=0)

    def logits(n):
        q = q_ref[0, n * tile:(n + 1) * tile, :]
        qq = jnp.concatenate([jnp.where(lane < ATT_DQ, q, zero), jnp.where(lane >= ATT_DQ, q, zero)], axis=0)
        n_keys = (n + 1) * tile
        s = _dot_nt(qq, k_ref[0, :n_keys, :])
        parts = [s[:, n_keys - tile:] + diag]
        if n >= 1:
            parts.insert(0, s[:, n_keys - 2 * tile:n_keys - tile] + near)
        if n >= 2:
            parts.insert(0, s[:, :n_keys - 2 * tile])
        return parts

    def probs(parts):
        m = functools.reduce(jnp.maximum, [jnp.max(x, axis=-1, keepdims=True) for x in parts])
        return jnp.concatenate([jnp.exp2(x - m) for x in parts], axis=1).astype(BF16)

    def finish(n, p):
        vx = vx_scr[:(n + 1) * tile, :]
        acc0 = _dot(p[:tile], vx)
        acc1 = _dot(p[tile:], vx)
        out = acc0[:, :dv] / acc0[:, dv:dv + 1] - lam * (acc1[:, :dv] / acc1[:, dv:dv + 1])
        o_ref[n * tile:(n + 1) * tile, :] = (_rms(out, gn_ref[...]) * (1.0 - lam_init)).astype(o_ref.dtype)

    for g in range(0, n_q // 2, ATT_GROUP_PAIRS):
        group = [n for gg in range(g, min(g + ATT_GROUP_PAIRS, n_q // 2)) for n in (gg, n_q - 1 - gg)]
        p_group = [probs(logits(n)) for n in group]
        for n, p in zip(group, p_group):
            finish(n, p)


def _prompt_attention(q, k, v, bias, lam_q, lam_k, gain, batch, lam_init):
    h, n, dv = q.shape
    seq = n // batch
    tile = ATT_TILE
    nq = seq // tile
    assert nq % 2 == 0
    seq_spec = pl.BlockSpec((1, seq, dv), lambda b, hh: (hh, b, 0))
    return pl.pallas_call(
        functools.partial(_attn_kernel, tile=tile, n_q=nq, lam_init=lam_init),
        grid=(batch, h),
        in_specs=[seq_spec, seq_spec, seq_spec,
                  pl.BlockSpec((1, 2, tile, tile), lambda b, hh: (hh, 0, 0, 0)),
                  pl.BlockSpec(lam_q.shape, lambda b, hh: (0, 0)),
                  pl.BlockSpec(lam_k.shape, lambda b, hh: (0, 0)),
                  pl.BlockSpec(gain.shape, lambda b, hh: (0, 0))],
        out_specs=pl.BlockSpec((seq, dv), lambda b, hh: (b, hh)),
        out_shape=jax.ShapeDtypeStruct((n, h * dv), BF16),
        scratch_shapes=[pltpu.VMEM((seq, 2 * dv), BF16)],
        compiler_params=_params("parallel", "parallel"),
        name="attn_prompt",
    )(q, k, v, bias, lam_q, lam_k, gain)


def _sample_attn_core(q, load_k, load_v, load_kn, load_vn, bias_ref, lam, gn_ref, k_buf, v_buf,
                      *, n_pages, page, n_tok_new, lam_init):
    width = q.shape[-1]
    n_rows = 2 * H_ATT * SUBLANES
    qt = jnp.concatenate([q] * (2 * H_ATT), axis=0)
    r = lax.broadcasted_iota(jnp.int32, (n_rows, width), 0)
    c = lax.broadcasted_iota(jnp.int32, (n_rows, width), 1)
    comp = r // (H_ATT * SUBLANES)
    head = (r // SUBLANES) % H_ATT
    own = (c // LANES == head) & ((c // ATT_DQ) % 2 == comp)
    qbd = jnp.where(own, qt, 0.0).astype(BF16)

    n_past = n_pages * page
    new_rows = 2 * SUBLANES
    for load, load_new, buf in ((load_k, load_kn, k_buf), (load_v, load_vn, v_buf)):
        for p in range(n_pages):
            for h in range(H_ATT):
                buf[p * page:(p + 1) * page, h * LANES:(h + 1) * LANES] = load(p, h).astype(BF16)
        new = jnp.concatenate([load_new(h) for h in range(H_ATT)], axis=1)
        new = jnp.concatenate([new, jnp.zeros((new_rows - n_tok_new, width), F32)], axis=0)
        buf[n_past:n_past + new_rows, :] = new.astype(BF16)
        buf[n_past + new_rows:, :] = jnp.zeros((page - new_rows, width), BF16)

    s = _dot_nt(qbd, k_buf[...])
    parts = [s[:, :n_past - page], s[:, n_past - page:] + bias_ref[...]]
    m = functools.reduce(jnp.maximum, [jnp.max(x, axis=-1, keepdims=True) for x in parts])
    p = jnp.concatenate([jnp.exp2(x - m) for x in parts], axis=1)
    inv = 1.0 / jnp.sum(p, axis=-1, keepdims=True)
    half = n_rows // 2
    a = (p[:half] * inv[:half] - lam * (p[half:] * inv[half:])).astype(BF16)
    out = _dot(a, v_buf[...])
    heads = [_rms(out[h * SUBLANES:(h + 1) * SUBLANES, h * LANES:(h + 1) * LANES], gn_ref[...])
             * (1.0 - lam_init) for h in range(H_ATT)]
    return jnp.concatenate(heads, axis=1)


def _ffn_attn_kernel(pt_ref, x_ref, g_ref, wgu_ref, wd_ref, q_ref, kn_ref, vn_ref, bias_ref, lq_ref, lk_ref,
                     gn_ref, ck_hbm, cv_hbm, o_ref, ao_ref, u_ref, k_ring, v_ring, k_buf, v_buf, sem,
                     *, n_chunks, gain_rows, n_pages, page, first_page, n_seq, lam_init):
    _, pre_row, post_row = gain_rows
    i = pl.program_id(0)
    seqs = q_ref.shape[0]
    page_rows = page * H_ATT
    new_rows = kn_ref.shape[0] // seqs
    n_tok_new = new_rows // H_ATT

    def page_copies(g, slot):
        copies = []
        for p in range(n_pages):
            src = pl.ds(pl.multiple_of((first_page + pt_ref[g, p]) * page_rows, page_rows), page_rows)
            dst = pl.ds(p * page_rows, page_rows)
            copies.append(pltpu.make_async_copy(ck_hbm.at[src, :], k_ring.at[slot, dst, :], sem.at[slot, 0]))
            copies.append(pltpu.make_async_copy(cv_hbm.at[src, :], v_ring.at[slot, dst, :], sem.at[slot, 1]))
        return copies

    n_slots = k_ring.shape[0]

    @pl.when(i == 0)
    def _():
        for a in range(n_slots):
            for cp in page_copies(min(a, n_seq - 1), a):
                cp.start()

    x = x_ref[...]
    h = _rms(x, g_ref[pre_row:pre_row + 1, :]).astype(BF16)
    lam = _lambda(lq_ref, lk_ref, lam_init)
    chunk_bounds = [(n_chunks * j) // seqs for j in range(seqs + 1)]
    for j in range(seqs):
        g = i * seqs + j
        slot = lax.rem(g, n_slots)

        for c in range(chunk_bounds[j], chunk_bounds[j + 1]):
            gu = _dot(h, wgu_ref[:, 2 * c * FFN_CHUNK:2 * (c + 1) * FFN_CHUNK])
            gate = gu[:, :FFN_CHUNK]
            up = gu[:, FFN_CHUNK:]
            u_ref[:, c * FFN_CHUNK:(c + 1) * FFN_CHUNK] = (gate * jax.nn.sigmoid(gate) * up).astype(BF16)

        for cp in page_copies(g, slot):
            cp.wait()

        def past(ring, p, hh, slot=slot):
            return ring[slot, pl.ds(p * page_rows + hh, page, stride=H_ATT), :]

        def new(ref, hh, j=j):
            return ref[pl.ds(j * new_rows + hh, n_tok_new, stride=H_ATT), :]

        ao_ref[j] = _sample_attn_core(
            q_ref[j], functools.partial(past, k_ring), functools.partial(past, v_ring),
            functools.partial(new, kn_ref), functools.partial(new, vn_ref), bias_ref, lam, gn_ref, k_buf, v_buf,
            n_pages=n_pages, page=page, n_tok_new=n_tok_new, lam_init=lam_init)

        for cp in page_copies(jnp.minimum(g + n_slots, n_seq - 1), slot):
            cp.start()

    y = _dot(u_ref[...], wd_ref[...])
    o_ref[...] = x + 0.5 * _rms(y, g_ref[post_row:post_row + 1, :])

    @pl.when(i == pl.num_programs(0) - 1)
    def _():
        for a in range(n_slots):
            for cp in page_copies(n_seq - 1, a):
                cp.wait()


def _ffn_with_sample_attention(x, gains, gain_rows, wgu, wd, page_table, q, k_new, v_new, cache_k, cache_v,
                               page, first_page, bias, lam_q, lam_k, gain, lam_init):
    n, d = x.shape
    f = wd.shape[0]
    tm = TOKEN_TILE
    steps = n // tm
    nb, rows, width = q.shape
    n_pages = page_table.shape[1]
    n_heads = width // LANES
    seqs = nb // steps
    assert nb == seqs * steps and nb >= PAGE_RING_SLOTS
    ring = pltpu.VMEM((PAGE_RING_SLOTS, n_pages * page * n_heads, LANES), F32)
    stage = pltpu.VMEM(((n_pages + 1) * page, width), BF16)
    full = lambda a: pl.BlockSpec(a.shape, lambda i, pt: (0,) * a.ndim)
    seq_spec = pl.BlockSpec((seqs, rows, width), lambda i, pt: (i, 0, 0))
    new_spec = pl.BlockSpec((seqs * rows * n_heads, LANES), lambda i, pt: (i, 0))
    row_spec = pl.BlockSpec((tm, d), lambda i, pt: (i, 0))
    grid_spec = pltpu.PrefetchScalarGridSpec(
        num_scalar_prefetch=1,
        grid=(steps,),
        in_specs=[row_spec, _resident(gains), _resident(wgu), _resident(wd),
                  seq_spec, new_spec, new_spec, full(bias), full(lam_q), full(lam_k), full(gain),
                  pl.BlockSpec(memory_space=pl.ANY), pl.BlockSpec(memory_space=pl.ANY)],
        out_specs=[row_spec, seq_spec],
        scratch_shapes=[pltpu.VMEM((tm, f), BF16), ring, ring, stage, stage,
                        pltpu.SemaphoreType.DMA((PAGE_RING_SLOTS, 2))],
    )
    return pl.pallas_call(
        functools.partial(_ffn_attn_kernel, n_chunks=f // FFN_CHUNK, gain_rows=gain_rows, n_pages=n_pages,
                          page=page, first_page=first_page, n_seq=nb, lam_init=lam_init),
        grid_spec=grid_spec,
        out_shape=[jax.ShapeDtypeStruct((n, d), F32), jax.ShapeDtypeStruct((nb, rows, width), F32)],
        compiler_params=pltpu.CompilerParams(dimension_semantics=("arbitrary",),
                                             vmem_limit_bytes=FUSED_VMEM_LIMIT_BYTES),
        name="ffn_attn_sample",
    )(page_table, x, gains, wgu, wd, q, k_new, v_new, bias, lam_q, lam_k, gain, cache_k, cache_v)


def _log_sigmoid(x):
    return jnp.minimum(x, 0.0) - jnp.log1p(jnp.exp(-jnp.abs(x)))


def _mlstm_step_kernel(q_ref, k_ref, v_ref, g_ref, gt_ref, om_ref, brow_ref, bcol_ref, gn_ref,
                       c0_ref, n0_ref, m0_ref, hm_ref, c_ref, n_ref, m_ref, *, rows_per_seq, valid):
    n_seq = c0_ref.shape[0]
    P = rows_per_seq
    R = n_seq * P
    r_i = lax.broadcasted_iota(jnp.int32, (R, R), 0)
    c_i = lax.broadcasted_iota(jnp.int32, (R, R), 1)
    same = (r_i // P) == (c_i // P)
    causal = same & (c_i <= r_i)
    hi = lax.Precision.HIGHEST

    g_col = g_ref[...] + brow_ref[...]
    g_row = gt_ref[0] + bcol_ref[...]
    lf_col = _log_sigmoid(g_col)
    lf_row = _log_sigmoid(g_row)
    tok_c = lax.broadcasted_iota(jnp.int32, g_col.shape, 0) % P
    tok_r = lax.broadcasted_iota(jnp.int32, g_row.shape, 1) % P
    g_col = jnp.where(tok_c < valid, g_col, -1e30)
    g_row = jnp.where(tok_r < valid, g_row, -1e30)
    lf_col = jnp.where(tok_c < valid, lf_col, 0.0)
    lf_row = jnp.where(tok_r < valid, lf_row, 0.0)
    same_f = same.astype(F32)
    b_col = jnp.dot(causal.astype(F32), lf_col, preferred_element_type=F32, precision=hi)
    b_row = jnp.dot(lf_row, (same & (r_i <= c_i)).astype(F32), preferred_element_type=F32, precision=hi)
    bl_col = jnp.dot(same_f, lf_col, preferred_element_type=F32, precision=hi)
    bl_row = jnp.dot(lf_row, same_f, preferred_element_type=F32, precision=hi)

    for h in range(H_ML):
        cols = slice(h * LANES, (h + 1) * LANES)
        q = q_ref[:, cols]
        k = k_ref[:, cols]
        v = v_ref[:, cols]
        bc = b_col[:, H_ML + h:H_ML + h + 1]
        ic = g_col[:, h:h + 1]
        br = b_row[H_ML + h:H_ML + h + 1, :]
        ir = g_row[h:h + 1, :]
        blc = bl_col[:, H_ML + h:H_ML + h + 1]
        blr = bl_row[H_ML + h:H_ML + h + 1, :]
        m = m0_ref[:, h:h + 1]

        log_d = jnp.where(causal, bc - br + ir, -jnp.inf)
        inter = bc + m
        m_t = jnp.maximum(inter, jnp.max(log_d, axis=-1, keepdims=True))
        s = _dot_nt(q.astype(BF16), k.astype(BF16)) * jnp.exp(log_d - m_t)
        w_inter = jnp.exp(inter - m_t)
        q_c = jnp.concatenate([_dot(q[j * P:(j + 1) * P].astype(BF16), c0_ref[j, h].astype(BF16))
                               for j in range(n_seq)], axis=0)
        n_rows = jnp.concatenate([jnp.broadcast_to(n0_ref[j, h:h + 1, :], (P, LANES)) for j in range(n_seq)],
                                 axis=0)
        num = _dot(s.astype(BF16), v.astype(BF16)) + w_inter * q_c
        den = jnp.sum(s, axis=-1, keepdims=True) + w_inter * jnp.sum(q * n_rows, axis=-1, keepdims=True)
        hv = num / jnp.maximum(jnp.abs(den), jnp.exp(-m_t))
        hm_ref[:, cols] = _rms(hv, gn_ref[...]) * jax.nn.sigmoid(om_ref[:, cols])

        w_max = jnp.max(jnp.where(same, blr - br + ir, -jnp.inf), axis=-1, keepdims=True)
        m_new = jnp.maximum(blc + m, w_max)
        ws = jnp.exp(blc - bc + ic - m_new)
        fw = jnp.exp(blc + m - m_new)
        wv = ws * v
        wk = ws * k
        for j in range(n_seq):
            rows = slice(j * P, (j + 1) * P)
            fw_j = fw[j * P:j * P + 1, :]
            c_ref[j, h] = fw_j * c0_ref[j, h] + _dot_tn(k[rows].astype(BF16), wv[rows].astype(BF16))
            n_ref[j, h:h + 1, :] = fw_j * n0_ref[j, h:h + 1, :] + jnp.sum(wk[rows], axis=0, keepdims=True)
            m_ref[j, h:h + 1, :] = jnp.broadcast_to(m_new[j * P:j * P + 1, :], (1, LANES))


def _mlstm_step(q, k, v, gates, gates_t, om, b_row, b_col, gain, state_c, state_n, m_rows, *, rows_per_seq,
                valid, n_seq):
    n, width = om.shape
    batch = state_c.shape[0]
    dh = LANES
    rows = n_seq * rows_per_seq
    row = lambda w: pl.BlockSpec((rows, w), lambda b: (b, 0))
    full = lambda a: pl.BlockSpec(a.shape, lambda b: (0,) * a.ndim)
    c_spec = pl.BlockSpec((n_seq, H_ML, dh, dh), lambda b: (b, 0, 0, 0))
    v_spec = pl.BlockSpec((n_seq, H_ML, dh), lambda b: (b, 0, 0))
    return pl.pallas_call(
        functools.partial(_mlstm_step_kernel, rows_per_seq=rows_per_seq, valid=valid),
        grid=(batch // n_seq,),
        in_specs=[row(width), row(width), row(width), row(LANES),
                  pl.BlockSpec((1, 2 * H_ML, rows), lambda b: (b, 0, 0)),
                  row(width), full(b_row), full(b_col), full(gain), c_spec, v_spec, row(LANES)],
        out_specs=[row(width), c_spec, v_spec, v_spec],
        out_shape=[jax.ShapeDtypeStruct((n, width), F32),
                   jax.ShapeDtypeStruct((batch, H_ML, dh, dh), F32),
                   jax.ShapeDtypeStruct((batch, H_ML, dh), F32),
                   jax.ShapeDtypeStruct((batch, H_ML, dh), F32)],
        compiler_params=_params("parallel"),
        name="mlstm_sample",
    )(q, k, v, gates, gates_t, om, b_row, b_col, gain, state_c, state_n, m_rows)


def _mlstm_seq_kernel(q_ref, k_ref, v_ref, g_ref, gt_ref, om_ref, brow_ref, bcol_ref, gn_ref,
                      hm_ref, c_ref, n_ref, m_ref, fcol_scr, vx_scr, *, tile):
    seq = g_ref.shape[0]
    n_t = seq // tile
    dv = v_ref.shape[-1]
    t_i = lax.broadcasted_iota(jnp.int32, (tile, tile), 0)
    s_i = lax.broadcasted_iota(jnp.int32, (tile, tile), 1)
    causal = s_i <= t_i
    lower = causal.astype(F32)
    upper = (t_i <= s_i).astype(F32)

    off_col = jnp.zeros((1, LANES), F32)
    off_row = jnp.zeros((2 * H_ML, 1), F32)
    u_tiles = []
    for j in range(n_t):
        rows = slice(j * tile, (j + 1) * tile)
        lf_col = _log_sigmoid(g_ref[rows, :] + brow_ref[...])
        f_col = jnp.dot(lower, lf_col, preferred_element_type=F32, precision=lax.Precision.HIGHEST) + off_col
        fcol_scr[rows, :] = f_col
        off_col = f_col[tile - 1:tile, :]
        g_row = gt_ref[j] + bcol_ref[...]
        f_row = jnp.dot(_log_sigmoid(g_row), upper, preferred_element_type=F32,
                        precision=lax.Precision.HIGHEST) + off_row
        off_row = f_row[:, tile - 1:tile]
        u_tiles.append((g_row[:H_ML, :] - f_row[H_ML:, :]) * LOG2E)

    ones_lane = lax.broadcasted_iota(jnp.int32, (seq, vx_scr.shape[2] - dv), 1)
    for h in range(H_ML):
        vx_scr[h, :, :dv] = v_ref[h]
        vx_scr[h, :, dv:] = jnp.where(ones_lane == 0, 1.0, 0.0).astype(BF16)

    for h in range(H_ML):
        u_row = jnp.concatenate([u[h:h + 1, :] for u in u_tiles], axis=1)
        tile_max = [jnp.max(u[h:h + 1, :], axis=-1, keepdims=True) for u in u_tiles]
        before = [None]
        for j in range(n_t):
            before.append(tile_max[j] if before[-1] is None else jnp.maximum(before[-1], tile_max[j]))

        def weights(n):
            n_keys = (n + 1) * tile
            s = _dot_nt(q_ref[h, n * tile:(n + 1) * tile, :], k_ref[h, :n_keys, :])
            u_diag = jnp.where(causal, u_row[:, n_keys - tile:n_keys], -jnp.inf)
            r = jnp.max(u_diag, axis=-1, keepdims=True)
            if before[n] is not None:
                r = jnp.maximum(r, before[n])
            r0 = jnp.maximum(r, 0.0)
            parts = [jnp.exp2(u_diag - r0)]
            if n >= 1:
                parts.insert(0, jnp.exp2(u_row[:, :n_keys - tile] - r0))
            return (s * jnp.concatenate(parts, axis=1)).astype(BF16), r0 * (1.0 / LOG2E)

        def finish(n, p, r0):
            rows = slice(n * tile, (n + 1) * tile)
            acc = _dot(p, vx_scr[h, :(n + 1) * tile, :])
            m_t = fcol_scr[rows, H_ML + h:H_ML + h + 1] + r0
            hv = acc[:, :dv] / jnp.maximum(jnp.abs(acc[:, dv:dv + 1]), jnp.exp(-m_t))
            gate = jax.nn.sigmoid(om_ref[rows, h * LANES:(h + 1) * LANES])
            hm_ref[rows, h * LANES:(h + 1) * LANES] = (_rms(hv, gn_ref[...]) * gate).astype(hm_ref.dtype)

        for g in range(n_t // 2):
            pair = (g, n_t - 1 - g)
            done = [weights(n) for n in pair]
            for n, (p, r0) in zip(pair, done):
                finish(n, p, r0)

        r0_last = jnp.maximum(before[n_t], 0.0) * (1.0 / LOG2E)
        u_col = (g_ref[:, h:h + 1] + brow_ref[:, h:h + 1]) - fcol_scr[:, H_ML + h:H_ML + h + 1]
        w_col = jnp.exp(u_col - r0_last)
        kf = k_ref[h]
        c_ref[0, h] = _dot_tn(kf, (w_col * v_ref[h].astype(F32)).astype(BF16))
        n_ref[0, h:h + 1, :] = jnp.sum(w_col * kf.astype(F32), axis=0, keepdims=True)
        m_last = off_col[:, H_ML + h:H_ML + h + 1] + r0_last
        m_ref[0, h:h + 1, :] = jnp.broadcast_to(m_last, (1, LANES))


def _mlstm_seq(q, k, v, gates, gates_t, om, b_row, b_col, gain, *, batch, tile):
    n, width = om.shape
    seq = n // batch
    n_t = seq // tile
    assert n_t % 2 == 0
    dh = LANES
    qkv_spec = pl.BlockSpec((H_ML, seq, dh), lambda b: (0, b, 0))
    row = lambda w: pl.BlockSpec((seq, w), lambda b: (b, 0))
    full = lambda a: pl.BlockSpec(a.shape, lambda b: (0,) * a.ndim)
    c_spec = pl.BlockSpec((1, H_ML, dh, dh), lambda b: (b, 0, 0, 0))
    v_spec = pl.BlockSpec((1, H_ML, dh), lambda b: (b, 0, 0))
    return pl.pallas_call(
        functools.partial(_mlstm_seq_kernel, tile=tile),
        grid=(batch,),
        in_specs=[qkv_spec, qkv_spec, qkv_spec, row(LANES),
                  pl.BlockSpec((n_t, 2 * H_ML, tile), lambda b: (b, 0, 0)),
                  row(width), full(b_row), full(b_col), full(gain)],
        out_specs=[row(width), c_spec, v_spec, v_spec],
        out_shape=[jax.ShapeDtypeStruct((n, width), BF16),
                   jax.ShapeDtypeStruct((batch, H_ML, dh, dh), F32),
                   jax.ShapeDtypeStruct((batch, H_ML, dh), F32),
                   jax.ShapeDtypeStruct((batch, H_ML, dh), F32)],
        scratch_shapes=[pltpu.VMEM((seq, LANES), F32), pltpu.VMEM((H_ML, seq, 2 * dh), BF16)],
        compiler_params=_params("parallel"),
        name="mlstm_prompt",
    )(q, k, v, gates, gates_t, om, b_row, b_col, gain)


def kernel(x_prompt, x_sample, cache_k, cache_v, state_C, state_n, state_m, page_table, rel_bias, norm_gains,
           ffn_w_gate, ffn_w_up, ffn_w_down, w_in, b_gates, lam_q, lam_k, attn_norm, mlstm_norm, w_out):
    batch, seq, d = x_prompt.shape
    dec_batch, dec_seq, _ = x_sample.shape
    depth = w_in.shape[0]
    page = cache_k.shape[2]
    width = H_ATT * LANES
    d_in = w_in.shape[-1]
    n_gate = 2 * H_ML
    assert d_in == 7 * width + n_gate and dec_seq <= SAMPLE_ROWS
    assert seq % ATT_TILE == 0 and seq % MLSTM_TILE == 0 and (batch * seq) % TOKEN_TILE == 0

    xp = x_prompt.reshape(batch * seq, d)
    xs = jnp.pad(x_sample, ((0, 0), (0, SAMPLE_ROWS - dec_seq), (0, 0))).reshape(dec_batch * SAMPLE_ROWS, d)

    prompt_bias = _prompt_bias(rel_bias, ATT_TILE)
    sample_bias = _sample_bias(rel_bias, page, dec_seq)

    outs = [[] for _ in range(10)]
    for l in range(depth):
        lam_init = 0.8 - 0.6 * math.exp(-0.3 * l)
        g = norm_gains[l]
        f = ffn_w_gate.shape[-1]
        wg3, wu3, wd3 = ffn_w_gate.reshape(-1, d, f), ffn_w_up.reshape(-1, d, f), ffn_w_down.reshape(-1, f, d)
        wgu1 = _fuse_gate_up(wg3, wu3, 2 * l, 1, WEIGHT_ROW_TILE)[0]
        wd1 = _cast_groups(wd3, 2 * l, 1, d, f // 4)[0]
        w_in_p = _cast_groups(w_in, l, 1, d_in - n_gate + LANES, WEIGHT_ROW_TILE)
        b_row = jnp.pad(b_gates[l].reshape(1, n_gate), ((0, 0), (0, LANES - n_gate)))
        b_col = b_gates[l].reshape(n_gate, 1)
        gn_att = attn_norm[l].reshape(1, -1)
        gn_ml = mlstm_norm[l].reshape(1, -1)
        ck = cache_k.reshape(-1, LANES)
        cv = cache_v.reshape(-1, LANES)
        first_page = l * cache_k.shape[1]

        xs = _ffn(xs, g, (None, 0, 1), wgu1, wd1)
        sample_seqs = math.gcd(dec_batch, MLSTM_SAMPLE_SEQS)
        (qa_s, kaf_s, vaf_s, _, _, qm_s, km_s, vm_s, om_s, gt_s, gtt_s) = _proj(xs, g[2:3], w_in_p, False,
                                                                                sample_seqs * SAMPLE_ROWS)
        seq3 = lambda a: a.reshape(dec_batch, SAMPLE_ROWS, width)
        xp, mix_a_s = _ffn_with_sample_attention(xp, g, (None, 0, 1), wgu1, wd1, page_table, seq3(qa_s), kaf_s,
                                                 vaf_s, ck, cv, page, first_page, sample_bias, lam_q[l], lam_k[l],
                                                 gn_att, lam_init)

        (qa, kaf, vaf, ka, va, qm, km, vm, om, gt, gtt, wgu2, wd2, wo) = _proj(
            xp, g[2:3], w_in_p, True, MLSTM_TILE, later=(wg3, wu3, wd3, 2 * l + 1, w_out, l))
        mix_a = _prompt_attention(qa, ka, va, prompt_bias, lam_q[l], lam_k[l], gn_att, batch, lam_init)
        mix_m, c_p, n_p, m_p = _mlstm_seq(qm, km, vm, gt, gtt, om, b_row, b_col, gn_ml, batch=batch, tile=MLSTM_TILE)
        xp = _ffn(xp, g, (3, 4, 5), wgu2, wd2, mix=(mix_a, mix_m, wo))

        m_rows = jnp.pad(jnp.repeat(state_m[l], SAMPLE_ROWS, axis=0), ((0, 0), (0, LANES - H_ML)))
        mix_m_s, c_s, n_s, m_s = _mlstm_step(qm_s, km_s, vm_s, gt_s, gtt_s, om_s, b_row, b_col, gn_ml,
                                             state_C[l], state_n[l], m_rows, rows_per_seq=SAMPLE_ROWS,
                                             valid=dec_seq, n_seq=sample_seqs)
        xs = _ffn(xs, g, (3, 4, 5), wgu2, wd2, mix=(mix_a_s.reshape(-1, width), mix_m_s, wo))

        new = lambda a: a.reshape(dec_batch, SAMPLE_ROWS, H_ATT, LANES)[:, :dec_seq]
        for lst, val in zip(outs, (kaf.reshape(batch, seq, H_ATT, LANES), vaf.reshape(batch, seq, H_ATT, LANES),
                                   c_p, n_p, m_p[:, :, 0], new(kaf_s), new(vaf_s), c_s, n_s, m_s[:, :, 0])):
            lst.append(val)

    y_prompt = xp.reshape(batch, seq, d)
    y_sample = xs.reshape(dec_batch, SAMPLE_ROWS, d)[:, :dec_seq]
    return (y_prompt, y_sample) + tuple(jnp.stack(o) for o in outs)
```

```python
import functools
import math

import jax
import jax.numpy as jnp
from jax import lax
from jax.experimental import pallas as pl
from jax.experimental.pallas import tpu as pltpu

F32 = jnp.float32
BF16 = jnp.bfloat16

H_ATT = 4
ATT_DQ = 64
H_ML = 4
NUM_BUCKETS = 32
MAX_DISTANCE = 128
EPS = 1e-6
LOG2E = math.log2(math.e)

LANES = 128
SUBLANES = 8
VMEM_LIMIT_BYTES = 56 * 1024 * 1024
FUSED_VMEM_LIMIT_BYTES = 60 * 1024 * 1024
PAGE_RING_SLOTS = 2

TOKEN_TILE = 512
FFN_CHUNK = 256
ATT_TILE = 256
ATT_GROUP_PAIRS = 2
SAMPLE_ROWS = 8
WEIGHT_ROW_TILE = 256
MLSTM_TILE = 256
MLSTM_SAMPLE_SEQS = 32


def _params(*sem):
    return pltpu.CompilerParams(dimension_semantics=sem, vmem_limit_bytes=VMEM_LIMIT_BYTES)


def _rms(x, g):
    return x * lax.rsqrt(jnp.mean(x * x, axis=-1, keepdims=True) + EPS) * g


def _dot(a, b):
    return jnp.dot(a, b, preferred_element_type=F32)


def _dot_nt(a, b):
    return lax.dot_general(a, b, (((1,), (1,)), ((), ())), preferred_element_type=F32)


def _dot_tn(a, b):
    return lax.dot_general(a, b, (((0,), (0,)), ((), ())), preferred_element_type=F32)


def _resident(arr, lead=None):
    if lead is None:
        return pl.BlockSpec(arr.shape, lambda *_: (0,) * arr.ndim, pipeline_mode=pl.Buffered(1))
    return pl.BlockSpec((None,) + arr.shape[1:], lambda *_: (lead,) + (0,) * (arr.ndim - 1),
                        pipeline_mode=pl.Buffered(1))


def _cast_kernel(x_ref, o_ref):
    w = x_ref.shape[-1]
    if o_ref.shape[-1] != w:
        o_ref[...] = jnp.zeros(o_ref.shape, o_ref.dtype)
    o_ref[:, :w] = x_ref[...].astype(o_ref.dtype)


def _cast_groups(x3, first, count, cols_out, row_tile):
    _, rows, cols = x3.shape
    return pl.pallas_call(
        _cast_kernel,
        grid=(count, rows // row_tile),
        in_specs=[pl.BlockSpec((None, row_tile, cols), lambda g, r: (first + g, r, 0))],
        out_specs=pl.BlockSpec((None, row_tile, cols_out), lambda g, r: (g, r, 0)),
        out_shape=jax.ShapeDtypeStruct((count, rows, cols_out), BF16),
        compiler_params=_params("parallel", "parallel"),
        name="cast_weights",
    )(x3)


def _gate_up_kernel(g_ref, u_ref, o_ref):
    for c in range(g_ref.shape[-1] // FFN_CHUNK):
        src = slice(c * FFN_CHUNK, (c + 1) * FFN_CHUNK)
        o_ref[:, 2 * c * FFN_CHUNK:(2 * c + 1) * FFN_CHUNK] = g_ref[:, src].astype(BF16)
        o_ref[:, (2 * c + 1) * FFN_CHUNK:(2 * c + 2) * FFN_CHUNK] = u_ref[:, src].astype(BF16)


def _fuse_gate_up(wg3, wu3, first, count, row_tile):
    _, rows, cols = wg3.shape
    spec = pl.BlockSpec((None, row_tile, cols), lambda g, r: (first + g, r, 0))
    return pl.pallas_call(
        _gate_up_kernel,
        grid=(count, rows // row_tile),
        in_specs=[spec, spec],
        out_specs=pl.BlockSpec((None, row_tile, 2 * cols), lambda g, r: (g, r, 0)),
        out_shape=jax.ShapeDtypeStruct((count, rows, 2 * cols), BF16),
        compiler_params=_params("parallel", "parallel"),
        name="fuse_gate_up",
    )(wg3, wu3)


def _ffn_kernel(x_ref, g_ref, wgu_ref, wd_ref, o_ref, u_ref, *, n_chunks, gain_rows):
    _, pre_row, post_row = gain_rows
    x = x_ref[...]
    h = _rms(x, g_ref[pre_row:pre_row + 1, :]).astype(BF16)
    for c in range(n_chunks):
        gu = _dot(h, wgu_ref[:, 2 * c * FFN_CHUNK:2 * (c + 1) * FFN_CHUNK])
        gate = gu[:, :FFN_CHUNK]
        up = gu[:, FFN_CHUNK:]
        u_ref[:, c * FFN_CHUNK:(c + 1) * FFN_CHUNK] = (gate * jax.nn.sigmoid(gate) * up).astype(BF16)
    y = _dot(u_ref[...], wd_ref[...])
    o_ref[...] = x + 0.5 * _rms(y, g_ref[post_row:post_row + 1, :])


def _ffn_two_kernel(xa_ref, maa_ref, mma_ref, xb_ref, mab_ref, mmb_ref, wo_ref, g_ref, wgu_ref, wd_ref,
                    oa_ref, ob_ref, u_ref, *, n_chunks, gain_rows, tiles_a):
    mix_row, pre_row, post_row = gain_rows
    first = pl.program_id(0) < tiles_a

    def run(x_ref, ma_ref, mm_ref, o_ref):
        half = ma_ref.shape[-1]
        y = _dot(ma_ref[...].astype(BF16), wo_ref[:half, :]) + _dot(mm_ref[...].astype(BF16), wo_ref[half:, :])
        x = x_ref[...] + _rms(y, g_ref[mix_row:mix_row + 1, :])
        h = _rms(x, g_ref[pre_row:pre_row + 1, :]).astype(BF16)
        for c in range(n_chunks):
            gu = _dot(h, wgu_ref[:, 2 * c * FFN_CHUNK:2 * (c + 1) * FFN_CHUNK])
            gate = gu[:, :FFN_CHUNK]
            up = gu[:, FFN_CHUNK:]
            u_ref[:, c * FFN_CHUNK:(c + 1) * FFN_CHUNK] = (gate * jax.nn.sigmoid(gate) * up).astype(BF16)
        o_ref[...] = x + 0.5 * _rms(_dot(u_ref[...], wd_ref[...]), g_ref[post_row:post_row + 1, :])

    @pl.when(first)
    def _():
        run(xa_ref, maa_ref, mma_ref, oa_ref)

    @pl.when(jnp.logical_not(first))
    def _():
        run(xb_ref, mab_ref, mmb_ref, ob_ref)


def _ffn_two(xa, mix_a, xb, mix_b, wo, gains, gain_rows, wgu, wd):
    (na, d), nb = xa.shape, xb.shape[0]
    f = wd.shape[0]
    tm = TOKEN_TILE
    tiles_a, tiles_b = na // tm, nb // tm
    spec_a = lambda w: pl.BlockSpec((tm, w), lambda i: (jnp.minimum(i, tiles_a - 1), 0))
    spec_b = lambda w: pl.BlockSpec((tm, w), lambda i: (jnp.maximum(i - tiles_a, 0), 0))
    half = mix_a[0].shape[1]
    return pl.pallas_call(
        functools.partial(_ffn_two_kernel, n_chunks=f // FFN_CHUNK, gain_rows=gain_rows, tiles_a=tiles_a),
        grid=(tiles_a + tiles_b,),
        in_specs=[spec_a(d), spec_a(half), spec_a(half), spec_b(d), spec_b(half), spec_b(half),
                  _resident(wo), _resident(gains), _resident(wgu), _resident(wd)],
        out_specs=[spec_a(d), spec_b(d)],
        out_shape=[jax.ShapeDtypeStruct((na, d), F32), jax.ShapeDtypeStruct((nb, d), F32)],
        scratch_shapes=[pltpu.VMEM((tm, f), BF16)],
        compiler_params=_params("arbitrary"),
        name="ffn_mix",
    )(xa, *mix_a, xb, *mix_b, wo, gains, wgu, wd)


def _ffn(x, gains, gain_rows, wgu, wd):
    n, d = x.shape
    f = wd.shape[0]
    tm = TOKEN_TILE
    row = pl.BlockSpec((tm, d), lambda i: (i, 0))
    return pl.pallas_call(
        functools.partial(_ffn_kernel, n_chunks=f // FFN_CHUNK, gain_rows=gain_rows),
        grid=(n // tm,),
        in_specs=[row, _resident(gains), _resident(wgu), _resident(wd)],
        out_specs=row,
        out_shape=jax.ShapeDtypeStruct((n, d), F32),
        scratch_shapes=[pltpu.VMEM((tm, f), BF16)],
        compiler_params=_params("parallel"),
        name="ffn",
    )(x, gains, wgu, wd)


def _proj_kernel(x_ref, g_ref, w_ref, *rest, head_major, width, gate_chunk, n_later):
    later_weights, outs = rest[:n_later], rest[n_later:]
    (qa_ref, kaf_ref, vaf_ref, ka_ref, va_ref, qm_ref, km_ref, vm_ref, om_ref, gt_ref, gtt_ref) = outs[:11]
    wgu_o, wd_o, wo_o = outs[11:] if later_weights else (None, None, None)
    xn = _rms(x_ref[...], g_ref[...]).astype(BF16)
    tm = x_ref.shape[0]
    n_heads = width // LANES

    def group(i):
        return _dot(xn, w_ref[:, i * width:(i + 1) * width])

    def put(ref, val):
        if head_major:
            for h in range(n_heads):
                ref[h] = val[:, h * LANES:(h + 1) * LANES].astype(ref.dtype)
        else:
            ref[...] = val.astype(ref.dtype)

    def put_rows(ref, val):
        for h in range(n_heads):
            ref[pl.ds(h, tm, stride=n_heads), :] = val[:, h * LANES:(h + 1) * LANES]

    put(qa_ref, group(0) * (ATT_DQ ** -0.5 * LOG2E))
    ka = group(1)
    put_rows(kaf_ref, ka)
    put(ka_ref, ka)
    va = group(2)
    put_rows(vaf_ref, va)
    put(va_ref, va)
    put(qm_ref, group(3))
    put(km_ref, group(4) * (LANES ** -0.5))
    put(vm_ref, group(5))
    om_ref[...] = group(6)
    gates = _dot(xn, w_ref[:, 7 * width:7 * width + LANES])
    gt_ref[...] = gates
    gates_t = gates.T[:2 * H_ML, :]
    for c in range(tm // gate_chunk):
        gtt_ref[c] = gates_t[:, c * gate_chunk:(c + 1) * gate_chunk]

    if later_weights:
        wg_ref, wu_ref, wd_ref, wo_ref = later_weights
        _gate_up_kernel(wg_ref, wu_ref, wgu_o)
        wd_o[...] = wd_ref[...].astype(BF16)
        wo_o[...] = wo_ref[...].astype(BF16)


def _rows_per_step(total, steps):
    share = 1
    while steps % share or total % (steps // share) or (total // (steps // share)) % (2 * SUBLANES):
        share += 1
    return total // (steps // share), share


def _proj(x, gain, w_in_p, head_major, gate_chunk, later=None):
    n, d = x.shape
    width = H_ATT * LANES
    n_heads = width // LANES
    tm = TOKEN_TILE
    row = lambda w: pl.BlockSpec((tm, w), lambda i: (i, 0))
    if head_major:
        hm_spec = pl.BlockSpec((n_heads, tm, LANES), lambda i: (0, i, 0))
        hm_shape = jax.ShapeDtypeStruct((n_heads, n, LANES), BF16)
    else:
        hm_spec = row(width)
        hm_shape = jax.ShapeDtypeStruct((n, width), F32)
    f32_shape = jax.ShapeDtypeStruct((n, width), F32)
    rows_spec = pl.BlockSpec((tm * n_heads, LANES), lambda i: (i, 0))
    rows_shape = jax.ShapeDtypeStruct((n * n_heads, LANES), F32)
    out_specs = [hm_spec, rows_spec, rows_spec, hm_spec, hm_spec, hm_spec, hm_spec, hm_spec, row(width),
                 row(LANES), pl.BlockSpec((tm // gate_chunk, 2 * H_ML, gate_chunk), lambda i: (i, 0, 0))]
    out_shape = [hm_shape, rows_shape, rows_shape, hm_shape, hm_shape, hm_shape, hm_shape, hm_shape, f32_shape,
                 jax.ShapeDtypeStruct((n, LANES), F32),
                 jax.ShapeDtypeStruct((n // gate_chunk, 2 * H_ML, gate_chunk), F32)]
    in_specs = [row(d), _resident(gain), _resident(w_in_p, 0)]
    args = [x, gain, w_in_p]
    if later is not None:
        wg3, wu3, wd3, group, wo3, layer = later
        steps = n // tm
        f = wg3.shape[-1]
        r_gu, s_gu = _rows_per_step(d, steps)
        r_d, s_d = _rows_per_step(f, steps)
        gu_spec = pl.BlockSpec((None, r_gu, f), lambda i: (group, i // s_gu, 0))
        in_specs += [gu_spec, gu_spec, pl.BlockSpec((None, r_d, d), lambda i: (group, i // s_d, 0)),
                     pl.BlockSpec((None, r_gu, d), lambda i: (layer, i // s_gu, 0))]
        args += [wg3, wu3, wd3, wo3]
        out_specs += [pl.BlockSpec((r_gu, 2 * f), lambda i: (i // s_gu, 0)),
                      pl.BlockSpec((r_d, d), lambda i: (i // s_d, 0)),
                      pl.BlockSpec((r_gu, d), lambda i: (i // s_gu, 0))]
        out_shape += [jax.ShapeDtypeStruct((d, 2 * f), BF16), jax.ShapeDtypeStruct((f, d), BF16),
                      jax.ShapeDtypeStruct((d, d), BF16)]
    return pl.pallas_call(
        functools.partial(_proj_kernel, head_major=head_major, width=width, gate_chunk=gate_chunk,
                          n_later=0 if later is None else 4),
        grid=(n // tm,),
        in_specs=in_specs,
        out_specs=out_specs,
        out_shape=out_shape,
        compiler_params=_params("arbitrary" if later is not None else "parallel"),
        name="proj_prompt" if head_major else "proj_sample",
    )(*args)


def _t5_bias(rel, rb_ref, h):
    n = jnp.maximum(rel, 0)
    max_exact = NUM_BUCKETS // 2
    x = (jnp.log(jnp.maximum(n, max_exact).astype(F32) / max_exact)
         / math.log(MAX_DISTANCE / max_exact) * (NUM_BUCKETS - max_exact))
    out = jnp.full(rel.shape, rb_ref[max_exact, h], F32)
    for i in range(1, NUM_BUCKETS - max_exact):
        out = jnp.where(x >= i, rb_ref[max_exact + i, h], out)
    for i in range(max_exact):
        out = jnp.where(n == i, rb_ref[i, h], out)
    return (out - rb_ref[NUM_BUCKETS - 1, h]) * LOG2E


def _prompt_bias_kernel(rb_ref, o_ref, *, tile):
    r = lax.broadcasted_iota(jnp.int32, (tile, tile), 0)
    c = lax.broadcasted_iota(jnp.int32, (tile, tile), 1)
    for h in range(H_ATT):
        o_ref[h, 0] = jnp.where(c <= r, _t5_bias(r - c, rb_ref, h), -jnp.inf)
        o_ref[h, 1] = _t5_bias(tile + r - c, rb_ref, h)


def _prompt_bias(rel_bias, tile):
    return pl.pallas_call(
        functools.partial(_prompt_bias_kernel, tile=tile),
        in_specs=[pl.BlockSpec(memory_space=pltpu.SMEM)],
        out_shape=jax.ShapeDtypeStruct((H_ATT, 2, tile, tile), F32),
        name="prompt_bias",
    )(rel_bias)


def _sample_bias_kernel(rb_ref, o_ref, *, page, valid):
    qi = lax.broadcasted_iota(jnp.int32, (SUBLANES, 2 * page), 0)
    col = lax.broadcasted_iota(jnp.int32, (SUBLANES, 2 * page), 1)
    kj = col - page
    for h in range(H_ATT):
        past = _t5_bias(page + qi - col, rb_ref, h)
        new = jnp.where((kj <= qi) & (kj < valid), _t5_bias(qi - kj, rb_ref, h), -jnp.inf)
        tile = jnp.where(col < page, past, new)
        for comp in range(2):
            o_ref[(comp * H_ATT + h) * SUBLANES:(comp * H_ATT + h + 1) * SUBLANES, :] = tile


def _sample_bias(rel_bias, page, valid):
    return pl.pallas_call(
        functools.partial(_sample_bias_kernel, page=page, valid=valid),
        in_specs=[pl.BlockSpec(memory_space=pltpu.SMEM)],
        out_shape=jax.ShapeDtypeStruct((2 * H_ATT * SUBLANES, 2 * page), F32),
        name="sample_bias",
    )(rel_bias)


def _lambda(lq_ref, lk_ref, lam_init):
    prod = lq_ref[...] * lk_ref[...]
    return (jnp.exp(jnp.sum(prod[0:1, :], axis=-1, keepdims=True))
            - jnp.exp(jnp.sum(prod[1:2, :], axis=-1, keepdims=True)) + lam_init)


def _attn_kernel(q_ref, k_ref, v_ref, bias_ref, lq_ref, lk_ref, gn_ref, o_ref, vx_scr, *, tile, n_q, lam_init):
    dv = v_ref.shape[-1]
    vx_scr[:, :dv] = v_ref[0]
    ones_lane = lax.broadcasted_iota(jnp.int32, (vx_scr.shape[0], vx_scr.shape[1] - dv), 1)
    vx_scr[:, dv:] = jnp.where(ones_lane == 0, 1.0, 0.0).astype(BF16)

    lane = lax.broadcasted_iota(jnp.int32, (tile, dv), 1)
    zero = jnp.zeros((tile, dv), BF16)
    lam = _lambda(lq_ref, lk_ref, lam_init)
    diag = jnp.concatenate([bias_ref[0, 0]] * 2, axis=0)
    near = jnp.concatenate([bias_ref[0, 1]] * 2, axis=0)

    def logits(n):
        q = q_ref[0, n * tile:(n + 1) * tile, :]
        qq = jnp.concatenate([jnp.where(lane < ATT_DQ, q, zero), jnp.where(lane >= ATT_DQ, q, zero)], axis=0)
        n_keys = (n + 1) * tile
        s = _dot_nt(qq, k_ref[0, :n_keys, :])
        parts = [s[:, n_keys - tile:] + diag]
        if n >= 1:
            parts.insert(0, s[:, n_keys - 2 * tile:n_keys - tile] + near)
        if n >= 2:
            parts.insert(0, s[:, :n_keys - 2 * tile])
        return parts

    def probs(parts):
        m = functools.reduce(jnp.maximum, [jnp.max(x, axis=-1, keepdims=True) for x in parts])
        return jnp.concatenate([jnp.exp2(x - m) for x in parts], axis=1).astype(BF16)

    def finish(n, p):
        vx = vx_scr[:(n + 1) * tile, :]
        acc0 = _dot(p[:tile], vx)
        acc1 = _dot(p[tile:], vx)
        out = acc0[:, :dv] / acc0[:, dv:dv + 1] - lam * (acc1[:, :dv] / acc1[:, dv:dv + 1])
        o_ref[n * tile:(n + 1) * tile, :] = (_rms(out, gn_ref[...]) * (1.0 - lam_init)).astype(o_ref.dtype)

    for g in range(0, n_q // 2, ATT_GROUP_PAIRS):
        group = [n for gg in range(g, min(g + ATT_GROUP_PAIRS, n_q // 2)) for n in (gg, n_q - 1 - gg)]
        p_group = [probs(logits(n)) for n in group]
        for n, p in zip(group, p_group):
            finish(n, p)


def _prompt_attention(q, k, v, bias, lam_q, lam_k, gain, batch, lam_init):
    h, n, dv = q.shape
    seq = n // batch
    tile = ATT_TILE
    nq = seq // tile
    assert nq % 2 == 0
    seq_spec = pl.BlockSpec((1, seq, dv), lambda b, hh: (hh, b, 0))
    return pl.pallas_call(
        functools.partial(_attn_kernel, tile=tile, n_q=nq, lam_init=lam_init),
        grid=(batch, h),
        in_specs=[seq_spec, seq_spec, seq_spec,
                  pl.BlockSpec((1, 2, tile, tile), lambda b, hh: (hh, 0, 0, 0)),
                  pl.BlockSpec(lam_q.shape, lambda b, hh: (0, 0)),
                  pl.BlockSpec(lam_k.shape, lambda b, hh: (0, 0)),
                  pl.BlockSpec(gain.shape, lambda b, hh: (0, 0))],
        out_specs=pl.BlockSpec((seq, dv), lambda b, hh: (b, hh)),
        out_shape=jax.ShapeDtypeStruct((n, h * dv), BF16),
        scratch_shapes=[pltpu.VMEM((seq, 2 * dv), BF16)],
        compiler_params=_params("parallel", "parallel"),
        name="attn_prompt",
    )(q, k, v, bias, lam_q, lam_k, gain)


def _sample_attn_core(q, load_k, load_v, load_kn, load_vn, bias_ref, lam, gn_ref, k_buf, v_buf,
                      *, n_pages, page, n_tok_new, lam_init):
    width = q.shape[-1]
    n_rows = 2 * H_ATT * SUBLANES
    qt = jnp.concatenate([q] * (2 * H_ATT), axis=0)
    r = lax.broadcasted_iota(jnp.int32, (n_rows, width), 0)
    c = lax.broadcasted_iota(jnp.int32, (n_rows, width), 1)
    comp = r // (H_ATT * SUBLANES)
    head = (r // SUBLANES) % H_ATT
    own = (c // LANES == head) & ((c // ATT_DQ) % 2 == comp)
    qbd = jnp.where(own, qt, 0.0).astype(BF16)

    n_past = n_pages * page
    new_rows = 2 * SUBLANES
    for load, load_new, buf in ((load_k, load_kn, k_buf), (load_v, load_vn, v_buf)):
        for p in range(n_pages):
            for h in range(H_ATT):
                buf[p * page:(p + 1) * page, h * LANES:(h + 1) * LANES] = load(p, h).astype(BF16)
        new = jnp.concatenate([load_new(h) for h in range(H_ATT)], axis=1)
        new = jnp.concatenate([new, jnp.zeros((new_rows - n_tok_new, width), F32)], axis=0)
        buf[n_past:n_past + new_rows, :] = new.astype(BF16)
        buf[n_past + new_rows:, :] = jnp.zeros((page - new_rows, width), BF16)

    s = _dot_nt(qbd, k_buf[...])
    parts = [s[:, :n_past - page], s[:, n_past - page:] + bias_ref[...]]
    m = functools.reduce(jnp.maximum, [jnp.max(x, axis=-1, keepdims=True) for x in parts])
    p = jnp.concatenate([jnp.exp2(x - m) for x in parts], axis=1)
    inv = 1.0 / jnp.sum(p, axis=-1, keepdims=True)
    half = n_rows // 2
    a = (p[:half] * inv[:half] - lam * (p[half:] * inv[half:])).astype(BF16)
    out = _dot(a, v_buf[...])
    heads = [_rms(out[h * SUBLANES:(h + 1) * SUBLANES, h * LANES:(h + 1) * LANES], gn_ref[...])
             * (1.0 - lam_init) for h in range(H_ATT)]
    return jnp.concatenate(heads, axis=1)


def _ffn_attn_kernel(pt_ref, x_ref, g_ref, wgu_ref, wd_ref, q_ref, kn_ref, vn_ref, bias_ref, lq_ref, lk_ref,
                     gn_ref, ck_hbm, cv_hbm, o_ref, ao_ref, u_ref, k_ring, v_ring, k_buf, v_buf, sem,
                     *, n_chunks, gain_rows, n_pages, page, first_page, n_seq, lam_init):
    _, pre_row, post_row = gain_rows
    i = pl.program_id(0)
    seqs = q_ref.shape[0]
    page_rows = page * H_ATT
    new_rows = kn_ref.shape[0] // seqs
    n_tok_new = new_rows // H_ATT

    def page_copies(g, slot):
        copies = []
        for p in range(n_pages):
            src = pl.ds(pl.multiple_of((first_page + pt_ref[g, p]) * page_rows, page_rows), page_rows)
            dst = pl.ds(p * page_rows, page_rows)
            copies.append(pltpu.make_async_copy(ck_hbm.at[src, :], k_ring.at[slot, dst, :], sem.at[slot, 0]))
            copies.append(pltpu.make_async_copy(cv_hbm.at[src, :], v_ring.at[slot, dst, :], sem.at[slot, 1]))
        return copies

    n_slots = k_ring.shape[0]

    @pl.when(i == 0)
    def _():
        for a in range(n_slots):
            for cp in page_copies(min(a, n_seq - 1), a):
                cp.start()

    x = x_ref[...]
    h = _rms(x, g_ref[pre_row:pre_row + 1, :]).astype(BF16)
    lam = _lambda(lq_ref, lk_ref, lam_init)
    chunk_bounds = [(n_chunks * j) // seqs for j in range(seqs + 1)]
    for j in range(seqs):
        g = i * seqs + j
        slot = lax.rem(g, n_slots)

        for c in range(chunk_bounds[j], chunk_bounds[j + 1]):
            gu = _dot(h, wgu_ref[:, 2 * c * FFN_CHUNK:2 * (c + 1) * FFN_CHUNK])
            gate = gu[:, :FFN_CHUNK]
            up = gu[:, FFN_CHUNK:]
            u_ref[:, c * FFN_CHUNK:(c + 1) * FFN_CHUNK] = (gate * jax.nn.sigmoid(gate) * up).astype(BF16)

        for cp in page_copies(g, slot):
            cp.wait()

        def past(ring, p, hh, slot=slot):
            return ring[slot, pl.ds(p * page_rows + hh, page, stride=H_ATT), :]

        def new(ref, hh, j=j):
            return ref[pl.ds(j * new_rows + hh, n_tok_new, stride=H_ATT), :]

        ao_ref[j] = _sample_attn_core(
            q_ref[j], functools.partial(past, k_ring), functools.partial(past, v_ring),
            functools.partial(new, kn_ref), functools.partial(new, vn_ref), bias_ref, lam, gn_ref, k_buf, v_buf,
            n_pages=n_pages, page=page, n_tok_new=n_tok_new, lam_init=lam_init)

        for cp in page_copies(jnp.minimum(g + n_slots, n_seq - 1), slot):
            cp.start()

    y = _dot(u_ref[...], wd_ref[...])
    o_ref[...] = x + 0.5 * _rms(y, g_ref[post_row:post_row + 1, :])

    @pl.when(i == pl.num_programs(0) - 1)
    def _():
        for a in range(n_slots):
            for cp in page_copies(n_seq - 1, a):
                cp.wait()


def _ffn_with_sample_attention(x, gains, gain_rows, wgu, wd, page_table, q, k_new, v_new, cache_k, cache_v,
                               page, first_page, bias, lam_q, lam_k, gain, lam_init):
    n, d = x.shape
    f = wd.shape[0]
    tm = TOKEN_TILE
    steps = n // tm
    nb, rows, width = q.shape
    n_pages = page_table.shape[1]
    n_heads = width // LANES
    seqs = nb // steps
    assert nb == seqs * steps and nb >= PAGE_RING_SLOTS
    ring = pltpu.VMEM((PAGE_RING_SLOTS, n_pages * page * n_heads, LANES), F32)
    stage = pltpu.VMEM(((n_pages + 1) * page, width), BF16)
    full = lambda a: pl.BlockSpec(a.shape, lambda i, pt: (0,) * a.ndim)
    seq_spec = pl.BlockSpec((seqs, rows, width), lambda i, pt: (i, 0, 0))
    new_spec = pl.BlockSpec((seqs * rows * n_heads, LANES), lambda i, pt: (i, 0))
    row_spec = pl.BlockSpec((tm, d), lambda i, pt: (i, 0))
    grid_spec = pltpu.PrefetchScalarGridSpec(
        num_scalar_prefetch=1,
        grid=(steps,),
        in_specs=[row_spec, _resident(gains), _resident(wgu), _resident(wd),
                  seq_spec, new_spec, new_spec, full(bias), full(lam_q), full(lam_k), full(gain),
                  pl.BlockSpec(memory_space=pl.ANY), pl.BlockSpec(memory_space=pl.ANY)],
        out_specs=[row_spec, seq_spec],
        scratch_shapes=[pltpu.VMEM((tm, f), BF16), ring, ring, stage, stage,
                        pltpu.SemaphoreType.DMA((PAGE_RING_SLOTS, 2))],
    )
    return pl.pallas_call(
        functools.partial(_ffn_attn_kernel, n_chunks=f // FFN_CHUNK, gain_rows=gain_rows, n_pages=n_pages,
                          page=page, first_page=first_page, n_seq=nb, lam_init=lam_init),
        grid_spec=grid_spec,
        out_shape=[jax.ShapeDtypeStruct((n, d), F32), jax.ShapeDtypeStruct((nb, rows, width), F32)],
        compiler_params=pltpu.CompilerParams(dimension_semantics=("arbitrary",),
                                             vmem_limit_bytes=FUSED_VMEM_LIMIT_BYTES),
        name="ffn_attn_sample",
    )(page_table, x, gains, wgu, wd, q, k_new, v_new, bias, lam_q, lam_k, gain, cache_k, cache_v)


def _log_sigmoid(x):
    return jnp.minimum(x, 0.0) - jnp.log1p(jnp.exp(-jnp.abs(x)))


def _mlstm_step_kernel(q_ref, k_ref, v_ref, g_ref, gt_ref, om_ref, brow_ref, bcol_ref, gn_ref,
                       c0_ref, n0_ref, m0_ref, hm_ref, c_ref, n_ref, m_ref, *, rows_per_seq, valid):
    n_seq = c0_ref.shape[0]
    P = rows_per_seq
    R = n_seq * P
    r_i = lax.broadcasted_iota(jnp.int32, (R, R), 0)
    c_i = lax.broadcasted_iota(jnp.int32, (R, R), 1)
    same = (r_i // P) == (c_i // P)
    causal = same & (c_i <= r_i)
    hi = lax.Precision.HIGHEST

    g_col = g_ref[...] + brow_ref[...]
    g_row = gt_ref[0] + bcol_ref[...]
    lf_col = _log_sigmoid(g_col)
    lf_row = _log_sigmoid(g_row)
    tok_c = lax.broadcasted_iota(jnp.int32, g_col.shape, 0) % P
    tok_r = lax.broadcasted_iota(jnp.int32, g_row.shape, 1) % P
    g_col = jnp.where(tok_c < valid, g_col, -1e30)
    g_row = jnp.where(tok_r < valid, g_row, -1e30)
    lf_col = jnp.where(tok_c < valid, lf_col, 0.0)
    lf_row = jnp.where(tok_r < valid, lf_row, 0.0)
    same_f = same.astype(F32)
    b_col = jnp.dot(causal.astype(F32), lf_col, preferred_element_type=F32, precision=hi)
    b_row = jnp.dot(lf_row, (same & (r_i <= c_i)).astype(F32), preferred_element_type=F32, precision=hi)
    bl_col = jnp.dot(same_f, lf_col, preferred_element_type=F32, precision=hi)
    bl_row = jnp.dot(lf_row, same_f, preferred_element_type=F32, precision=hi)

    for h in range(H_ML):
        cols = slice(h * LANES, (h + 1) * LANES)
        q = q_ref[:, cols]
        k = k_ref[:, cols]
        v = v_ref[:, cols]
        bc = b_col[:, H_ML + h:H_ML + h + 1]
        ic = g_col[:, h:h + 1]
        br = b_row[H_ML + h:H_ML + h + 1, :]
        ir = g_row[h:h + 1, :]
        blc = bl_col[:, H_ML + h:H_ML + h + 1]
        blr = bl_row[H_ML + h:H_ML + h + 1, :]
        m = m0_ref[:, h:h + 1]

        log_d = jnp.where(causal, bc - br + ir, -jnp.inf)
        inter = bc + m
        m_t = jnp.maximum(inter, jnp.max(log_d, axis=-1, keepdims=True))
        s = _dot_nt(q.astype(BF16), k.astype(BF16)) * jnp.exp(log_d - m_t)
        w_inter = jnp.exp(inter - m_t)
        q_c = jnp.concatenate([_dot(q[j * P:(j + 1) * P].astype(BF16), c0_ref[j, h].astype(BF16))
                               for j in range(n_seq)], axis=0)
        n_rows = jnp.concatenate([jnp.broadcast_to(n0_ref[j, h:h + 1, :], (P, LANES)) for j in range(n_seq)],
                                 axis=0)
        num = _dot(s.astype(BF16), v.astype(BF16)) + w_inter * q_c
        den = jnp.sum(s, axis=-1, keepdims=True) + w_inter * jnp.sum(q * n_rows, axis=-1, keepdims=True)
        hv = num / jnp.maximum(jnp.abs(den), jnp.exp(-m_t))
        hm_ref[:, cols] = _rms(hv, gn_ref[...]) * jax.nn.sigmoid(om_ref[:, cols])

        w_max = jnp.max(jnp.where(same, blr - br + ir, -jnp.inf), axis=-1, keepdims=True)
        m_new = jnp.maximum(blc + m, w_max)
        ws = jnp.exp(blc - bc + ic - m_new)
        fw = jnp.exp(blc + m - m_new)
        wv = ws * v
        wk = ws * k
        for j in range(n_seq):
            rows = slice(j * P, (j + 1) * P)
            fw_j = fw[j * P:j * P + 1, :]
            c_ref[j, h] = fw_j * c0_ref[j, h] + _dot_tn(k[rows].astype(BF16), wv[rows].astype(BF16))
            n_ref[j, h:h + 1, :] = fw_j * n0_ref[j, h:h + 1, :] + jnp.sum(wk[rows], axis=0, keepdims=True)
            m_ref[j, h:h + 1, :] = jnp.broadcast_to(m_new[j * P:j * P + 1, :], (1, LANES))


def _mlstm_step(q, k, v, gates, gates_t, om, b_row, b_col, gain, state_c, state_n, m_rows, *, rows_per_seq,
                valid, n_seq):
    n, width = om.shape
    batch = state_c.shape[0]
    dh = LANES
    rows = n_seq * rows_per_seq
    row = lambda w: pl.BlockSpec((rows, w), lambda b: (b, 0))
    full = lambda a: pl.BlockSpec(a.shape, lambda b: (0,) * a.ndim)
    c_spec = pl.BlockSpec((n_seq, H_ML, dh, dh), lambda b: (b, 0, 0, 0))
    v_spec = pl.BlockSpec((n_seq, H_ML, dh), lambda b: (b, 0, 0))
    return pl.pallas_call(
        functools.partial(_mlstm_step_kernel, rows_per_seq=rows_per_seq, valid=valid),
        grid=(batch // n_seq,),
        in_specs=[row(width), row(width), row(width), row(LANES),
                  pl.BlockSpec((1, 2 * H_ML, rows), lambda b: (b, 0, 0)),
                  row(width), full(b_row), full(b_col), full(gain), c_spec, v_spec, row(LANES)],
        out_specs=[row(width), c_spec, v_spec, v_spec],
        out_shape=[jax.ShapeDtypeStruct((n, width), F32),
                   jax.ShapeDtypeStruct((batch, H_ML, dh, dh), F32),
                   jax.ShapeDtypeStruct((batch, H_ML, dh), F32),
                   jax.ShapeDtypeStruct((batch, H_ML, dh), F32)],
        compiler_params=_params("parallel"),
        name="mlstm_sample",
    )(q, k, v, gates, gates_t, om, b_row, b_col, gain, state_c, state_n, m_rows)


def _mlstm_seq_kernel(q_ref, k_ref, v_ref, g_ref, gt_ref, om_ref, brow_ref, bcol_ref, gn_ref,
                      hm_ref, c_ref, n_ref, m_ref, fcol_scr, vx_scr, *, tile):
    seq = g_ref.shape[0]
    n_t = seq // tile
    dv = v_ref.shape[-1]
    t_i = lax.broadcasted_iota(jnp.int32, (tile, tile), 0)
    s_i = lax.broadcasted_iota(jnp.int32, (tile, tile), 1)
    causal = s_i <= t_i
    lower = causal.astype(F32)
    upper = (t_i <= s_i).astype(F32)

    off_col = jnp.zeros((1, LANES), F32)
    off_row = jnp.zeros((2 * H_ML, 1), F32)
    u_tiles = []
    for j in range(n_t):
        rows = slice(j * tile, (j + 1) * tile)
        lf_col = _log_sigmoid(g_ref[rows, :] + brow_ref[...])
        f_col = jnp.dot(lower, lf_col, preferred_element_type=F32, precision=lax.Precision.HIGHEST) + off_col
        fcol_scr[rows, :] = f_col
        off_col = f_col[tile - 1:tile, :]
        g_row = gt_ref[j] + bcol_ref[...]
        f_row = jnp.dot(_log_sigmoid(g_row), upper, preferred_element_type=F32,
                        precision=lax.Precision.HIGHEST) + off_row
        off_row = f_row[:, tile - 1:tile]
        u_tiles.append((g_row[:H_ML, :] - f_row[H_ML:, :]) * LOG2E)

    ones_lane = lax.broadcasted_iota(jnp.int32, (seq, vx_scr.shape[2] - dv), 1)
    for h in range(H_ML):
        vx_scr[h, :, :dv] = v_ref[h]
        vx_scr[h, :, dv:] = jnp.where(ones_lane == 0, 1.0, 0.0).astype(BF16)

    for h in range(H_ML):
        u_row = jnp.concatenate([u[h:h + 1, :] for u in u_tiles], axis=1)
        tile_max = [jnp.max(u[h:h + 1, :], axis=-1, keepdims=True) for u in u_tiles]
        before = [None]
        for j in range(n_t):
            before.append(tile_max[j] if before[-1] is None else jnp.maximum(before[-1], tile_max[j]))

        def weights(n):
            n_keys = (n + 1) * tile
            s = _dot_nt(q_ref[h, n * tile:(n + 1) * tile, :], k_ref[h, :n_keys, :])
            u_diag = jnp.where(causal, u_row[:, n_keys - tile:n_keys], -jnp.inf)
            r = jnp.max(u_diag, axis=-1, keepdims=True)
            if before[n] is not None:
                r = jnp.maximum(r, before[n])
            r0 = jnp.maximum(r, 0.0)
            parts = [jnp.exp2(u_diag - r0)]
            if n >= 1:
                parts.insert(0, jnp.exp2(u_row[:, :n_keys - tile] - r0))
            return (s * jnp.concatenate(parts, axis=1)).astype(BF16), r0 * (1.0 / LOG2E)

        def finish(n, p, r0):
            rows = slice(n * tile, (n + 1) * tile)
            acc = _dot(p, vx_scr[h, :(n + 1) * tile, :])
            m_t = fcol_scr[rows, H_ML + h:H_ML + h + 1] + r0
            hv = acc[:, :dv] / jnp.maximum(jnp.abs(acc[:, dv:dv + 1]), jnp.exp(-m_t))
            gate = jax.nn.sigmoid(om_ref[rows, h * LANES:(h + 1) * LANES])
            hm_ref[rows, h * LANES:(h + 1) * LANES] = (_rms(hv, gn_ref[...]) * gate).astype(hm_ref.dtype)

        for g in range(n_t // 2):
            pair = (g, n_t - 1 - g)
            done = [weights(n) for n in pair]
            for n, (p, r0) in zip(pair, done):
                finish(n, p, r0)

        r0_last = jnp.maximum(before[n_t], 0.0) * (1.0 / LOG2E)
        u_col = (g_ref[:, h:h + 1] + brow_ref[:, h:h + 1]) - fcol_scr[:, H_ML + h:H_ML + h + 1]
        w_col = jnp.exp(u_col - r0_last)
        kf = k_ref[h]
        c_ref[0, h] = _dot_tn(kf, (w_col * v_ref[h].astype(F32)).astype(BF16))
        n_ref[0, h:h + 1, :] = jnp.sum(w_col * kf.astype(F32), axis=0, keepdims=True)
        m_last = off_col[:, H_ML + h:H_ML + h + 1] + r0_last
        m_ref[0, h:h + 1, :] = jnp.broadcast_to(m_last, (1, LANES))


def _mlstm_seq(q, k, v, gates, gates_t, om, b_row, b_col, gain, *, batch, tile):
    n, width = om.shape
    seq = n // batch
    n_t = seq // tile
    assert n_t % 2 == 0
    dh = LANES
    qkv_spec = pl.BlockSpec((H_ML, seq, dh), lambda b: (0, b, 0))
    row = lambda w: pl.BlockSpec((seq, w), lambda b: (b, 0))
    full = lambda a: pl.BlockSpec(a.shape, lambda b: (0,) * a.ndim)
    c_spec = pl.BlockSpec((1, H_ML, dh, dh), lambda b: (b, 0, 0, 0))
    v_spec = pl.BlockSpec((1, H_ML, dh), lambda b: (b, 0, 0))
    return pl.pallas_call(
        functools.partial(_mlstm_seq_kernel, tile=tile),
        grid=(batch,),
        in_specs=[qkv_spec, qkv_spec, qkv_spec, row(LANES),
                  pl.BlockSpec((n_t, 2 * H_ML, tile), lambda b: (b, 0, 0)),
                  row(width), full(b_row), full(b_col), full(gain)],
        out_specs=[row(width), c_spec, v_spec, v_spec],
        out_shape=[jax.ShapeDtypeStruct((n, width), BF16),
                   jax.ShapeDtypeStruct((batch, H_ML, dh, dh), F32),
                   jax.ShapeDtypeStruct((batch, H_ML, dh), F32),
                   jax.ShapeDtypeStruct((batch, H_ML, dh), F32)],
        scratch_shapes=[pltpu.VMEM((seq, LANES), F32), pltpu.VMEM((H_ML, seq, 2 * dh), BF16)],
        compiler_params=_params("parallel"),
        name="mlstm_prompt",
    )(q, k, v, gates, gates_t, om, b_row, b_col, gain)


def kernel(x_prompt, x_sample, cache_k, cache_v, state_C, state_n, state_m, page_table, rel_bias, norm_gains,
           ffn_w_gate, ffn_w_up, ffn_w_down, w_in, b_gates, lam_q, lam_k, attn_norm, mlstm_norm, w_out):
    batch, seq, d = x_prompt.shape
    dec_batch, dec_seq, _ = x_sample.shape
    depth = w_in.shape[0]
    page = cache_k.shape[2]
    width = H_ATT * LANES
    d_in = w_in.shape[-1]
    n_gate = 2 * H_ML
    assert d_in == 7 * width + n_gate and dec_seq <= SAMPLE_ROWS
    assert seq % ATT_TILE == 0 and seq % MLSTM_TILE == 0 and (batch * seq) % TOKEN_TILE == 0

    xp = x_prompt.reshape(batch * seq, d)
    xs = jnp.pad(x_sample, ((0, 0), (0, SAMPLE_ROWS - dec_seq), (0, 0))).reshape(dec_batch * SAMPLE_ROWS, d)

    prompt_bias = _prompt_bias(rel_bias, ATT_TILE)
    sample_bias = _sample_bias(rel_bias, page, dec_seq)

    outs = [[] for _ in range(10)]
    for l in range(depth):
        lam_init = 0.8 - 0.6 * math.exp(-0.3 * l)
        g = norm_gains[l]
        f = ffn_w_gate.shape[-1]
        wg3, wu3, wd3 = ffn_w_gate.reshape(-1, d, f), ffn_w_up.reshape(-1, d, f), ffn_w_down.reshape(-1, f, d)
        wgu1 = _fuse_gate_up(wg3, wu3, 2 * l, 1, WEIGHT_ROW_TILE)[0]
        wd1 = _cast_groups(wd3, 2 * l, 1, d, f // 4)[0]
        w_in_p = _cast_groups(w_in, l, 1, d_in - n_gate + LANES, WEIGHT_ROW_TILE)
        b_row = jnp.pad(b_gates[l].reshape(1, n_gate), ((0, 0), (0, LANES - n_gate)))
        b_col = b_gates[l].reshape(n_gate, 1)
        gn_att = attn_norm[l].reshape(1, -1)
        gn_ml = mlstm_norm[l].reshape(1, -1)
        ck = cache_k.reshape(-1, LANES)
        cv = cache_v.reshape(-1, LANES)
        first_page = l * cache_k.shape[1]

        xs = _ffn(xs, g, (None, 0, 1), wgu1, wd1)
        sample_seqs = math.gcd(dec_batch, MLSTM_SAMPLE_SEQS)
        (qa_s, kaf_s, vaf_s, _, _, qm_s, km_s, vm_s, om_s, gt_s, gtt_s) = _proj(xs, g[2:3], w_in_p, False,
                                                                                sample_seqs * SAMPLE_ROWS)
        seq3 = lambda a: a.reshape(dec_batch, SAMPLE_ROWS, width)
        xp, mix_a_s = _ffn_with_sample_attention(xp, g, (None, 0, 1), wgu1, wd1, page_table, seq3(qa_s), kaf_s,
                                                 vaf_s, ck, cv, page, first_page, sample_bias, lam_q[l], lam_k[l],
                                                 gn_att, lam_init)

        (qa, kaf, vaf, ka, va, qm, km, vm, om, gt, gtt, wgu2, wd2, wo) = _proj(
            xp, g[2:3], w_in_p, True, MLSTM_TILE, later=(wg3, wu3, wd3, 2 * l + 1, w_out, l))
        mix_a = _prompt_attention(qa, ka, va, prompt_bias, lam_q[l], lam_k[l], gn_att, batch, lam_init)
        mix_m, c_p, n_p, m_p = _mlstm_seq(qm, km, vm, gt, gtt, om, b_row, b_col, gn_ml, batch=batch, tile=MLSTM_TILE)

        m_rows = jnp.pad(jnp.repeat(state_m[l], SAMPLE_ROWS, axis=0), ((0, 0), (0, LANES - H_ML)))
        mix_m_s, c_s, n_s, m_s = _mlstm_step(qm_s, km_s, vm_s, gt_s, gtt_s, om_s, b_row, b_col, gn_ml,
                                             state_C[l], state_n[l], m_rows, rows_per_seq=SAMPLE_ROWS,
                                             valid=dec_seq, n_seq=sample_seqs)

        xp, xs = _ffn_two(xp, (mix_a, mix_m), xs, (mix_a_s.reshape(-1, width), mix_m_s), wo, g, (3, 4, 5),
                          wgu2, wd2)

        new = lambda a: a.reshape(dec_batch, SAMPLE_ROWS, H_ATT, LANES)[:, :dec_seq]
        for lst, val in zip(outs, (kaf.reshape(batch, seq, H_ATT, LANES), vaf.reshape(batch, seq, H_ATT, LANES),
                                   c_p, n_p, m_p[:, :, 0], new(kaf_s), new(vaf_s), c_s, n_s, m_s[:, :, 0])):
            lst.append(val)

    y_prompt = xp.reshape(batch, seq, d)
    y_sample = xs.reshape(dec_batch, SAMPLE_ROWS, d)[:, :dec_seq]
    return (y_prompt, y_sample) + tuple(jnp.stack(o) for o in outs)
```

```python
import functools
import math

import jax
import jax.numpy as jnp
from jax import lax
from jax.experimental import pallas as pl
from jax.experimental.pallas import tpu as pltpu

F32 = jnp.float32
BF16 = jnp.bfloat16

H_ATT = 4
ATT_DQ = 64
H_ML = 4
NUM_BUCKETS = 32
MAX_DISTANCE = 128
EPS = 1e-6
LOG2E = math.log2(math.e)

LANES = 128
SUBLANES = 8
VMEM_LIMIT_BYTES = 56 * 1024 * 1024
FUSED_VMEM_LIMIT_BYTES = 60 * 1024 * 1024
PAGE_RING_SLOTS = 2

TOKEN_TILE = 512
FFN_CHUNK = 256
ATT_TILE = 256
ATT_GROUP_PAIRS = 2
SAMPLE_ROWS = 8
WEIGHT_ROW_TILE = 256
MLSTM_TILE = 256
MLSTM_SAMPLE_SEQS = 32


def _params(*sem):
    return pltpu.CompilerParams(dimension_semantics=sem, vmem_limit_bytes=VMEM_LIMIT_BYTES)


def _rms(x, g):
    return x * lax.rsqrt(jnp.mean(x * x, axis=-1, keepdims=True) + EPS) * g


def _dot(a, b):
    return jnp.dot(a, b, preferred_element_type=F32)


def _dot_nt(a, b):
    return lax.dot_general(a, b, (((1,), (1,)), ((), ())), preferred_element_type=F32)


def _dot_tn(a, b):
    return lax.dot_general(a, b, (((0,), (0,)), ((), ())), preferred_element_type=F32)


def _resident(arr, lead=None):
    if lead is None:
        return pl.BlockSpec(arr.shape, lambda *_: (0,) * arr.ndim, pipeline_mode=pl.Buffered(1))
    return pl.BlockSpec((None,) + arr.shape[1:], lambda *_: (lead,) + (0,) * (arr.ndim - 1),
                        pipeline_mode=pl.Buffered(1))


def _cast_kernel(x_ref, o_ref):
    w = x_ref.shape[-1]
    if o_ref.shape[-1] != w:
        o_ref[...] = jnp.zeros(o_ref.shape, o_ref.dtype)
    o_ref[:, :w] = x_ref[...].astype(o_ref.dtype)


def _cast_groups(x3, first, count, cols_out, row_tile):
    _, rows, cols = x3.shape
    return pl.pallas_call(
        _cast_kernel,
        grid=(count, rows // row_tile),
        in_specs=[pl.BlockSpec((None, row_tile, cols), lambda g, r: (first + g, r, 0))],
        out_specs=pl.BlockSpec((None, row_tile, cols_out), lambda g, r: (g, r, 0)),
        out_shape=jax.ShapeDtypeStruct((count, rows, cols_out), BF16),
        compiler_params=_params("parallel", "parallel"),
        name="cast_weights",
    )(x3)


def _cast_transposed_kernel(x_ref, o_ref):
    rows = x_ref.shape[0]
    pad = jnp.zeros((o_ref.shape[-1] - rows, x_ref.shape[1]), F32)
    o_ref[...] = jnp.concatenate([x_ref[...], pad], axis=0).T.astype(o_ref.dtype)


def _cast_transposed(xt3, group, cols_out, col_tile):
    _, cols, rows = xt3.shape
    return pl.pallas_call(
        _cast_transposed_kernel,
        grid=(rows // col_tile,),
        in_specs=[pl.BlockSpec((None, cols, col_tile), lambda r: (group, 0, r))],
        out_specs=pl.BlockSpec((None, col_tile, cols_out), lambda r: (0, r, 0)),
        out_shape=jax.ShapeDtypeStruct((1, rows, cols_out), BF16),
        compiler_params=_params("parallel"),
        name="cast_weights_t",
    )(xt3)


def _gate_up_kernel(g_ref, u_ref, o_ref):
    for c in range(g_ref.shape[-1] // FFN_CHUNK):
        src = slice(c * FFN_CHUNK, (c + 1) * FFN_CHUNK)
        o_ref[:, 2 * c * FFN_CHUNK:(2 * c + 1) * FFN_CHUNK] = g_ref[:, src].astype(BF16)
        o_ref[:, (2 * c + 1) * FFN_CHUNK:(2 * c + 2) * FFN_CHUNK] = u_ref[:, src].astype(BF16)


def _fuse_gate_up(wg3, wu3, first, count, row_tile):
    _, rows, cols = wg3.shape
    spec = pl.BlockSpec((None, row_tile, cols), lambda g, r: (first + g, r, 0))
    return pl.pallas_call(
        _gate_up_kernel,
        grid=(count, rows // row_tile),
        in_specs=[spec, spec],
        out_specs=pl.BlockSpec((None, row_tile, 2 * cols), lambda g, r: (g, r, 0)),
        out_shape=jax.ShapeDtypeStruct((count, rows, 2 * cols), BF16),
        compiler_params=_params("parallel", "parallel"),
        name="fuse_gate_up",
    )(wg3, wu3)


def _ffn_kernel(x_ref, g_ref, wgu_ref, wd_ref, o_ref, u_ref, *, n_chunks, gain_rows):
    _, pre_row, post_row = gain_rows
    x = x_ref[...]
    h = _rms(x, g_ref[pre_row:pre_row + 1, :]).astype(BF16)
    for c in range(n_chunks):
        gu = _dot(h, wgu_ref[:, 2 * c * FFN_CHUNK:2 * (c + 1) * FFN_CHUNK])
        gate = gu[:, :FFN_CHUNK]
        up = gu[:, FFN_CHUNK:]
        u_ref[:, c * FFN_CHUNK:(c + 1) * FFN_CHUNK] = (gate * jax.nn.sigmoid(gate) * up).astype(BF16)
    y = _dot(u_ref[...], wd_ref[...])
    o_ref[...] = x + 0.5 * _rms(y, g_ref[post_row:post_row + 1, :])


def _ffn_two_kernel(xa_ref, maa_ref, mma_ref, xb_ref, mab_ref, mmb_ref, wo_ref, g_ref, wgu_ref, wd_ref,
                    oa_ref, ob_ref, u_ref, *, n_chunks, gain_rows, tiles_a):
    mix_row, pre_row, post_row = gain_rows
    first = pl.program_id(0) < tiles_a

    def run(x_ref, ma_ref, mm_ref, o_ref):
        half = ma_ref.shape[-1]
        y = _dot(ma_ref[...].astype(BF16), wo_ref[:half, :]) + _dot(mm_ref[...].astype(BF16), wo_ref[half:, :])
        x = x_ref[...] + _rms(y, g_ref[mix_row:mix_row + 1, :])
        h = _rms(x, g_ref[pre_row:pre_row + 1, :]).astype(BF16)
        for c in range(n_chunks):
            gu = _dot(h, wgu_ref[:, 2 * c * FFN_CHUNK:2 * (c + 1) * FFN_CHUNK])
            gate = gu[:, :FFN_CHUNK]
            up = gu[:, FFN_CHUNK:]
            u_ref[:, c * FFN_CHUNK:(c + 1) * FFN_CHUNK] = (gate * jax.nn.sigmoid(gate) * up).astype(BF16)
        o_ref[...] = x + 0.5 * _rms(_dot(u_ref[...], wd_ref[...]), g_ref[post_row:post_row + 1, :])

    @pl.when(first)
    def _():
        run(xa_ref, maa_ref, mma_ref, oa_ref)

    @pl.when(jnp.logical_not(first))
    def _():
        run(xb_ref, mab_ref, mmb_ref, ob_ref)


def _ffn_two(xa, mix_a, xb, mix_b, wo, gains, gain_rows, wgu, wd):
    (na, d), nb = xa.shape, xb.shape[0]
    f = wd.shape[0]
    tm = TOKEN_TILE
    tiles_a, tiles_b = na // tm, nb // tm
    spec_a = lambda w: pl.BlockSpec((tm, w), lambda i: (jnp.minimum(i, tiles_a - 1), 0))
    spec_b = lambda w: pl.BlockSpec((tm, w), lambda i: (jnp.maximum(i - tiles_a, 0), 0))
    half = mix_a[0].shape[1]
    return pl.pallas_call(
        functools.partial(_ffn_two_kernel, n_chunks=f // FFN_CHUNK, gain_rows=gain_rows, tiles_a=tiles_a),
        grid=(tiles_a + tiles_b,),
        in_specs=[spec_a(d), spec_a(half), spec_a(half), spec_b(d), spec_b(half), spec_b(half),
                  _resident(wo), _resident(gains), _resident(wgu), _resident(wd)],
        out_specs=[spec_a(d), spec_b(d)],
        out_shape=[jax.ShapeDtypeStruct((na, d), F32), jax.ShapeDtypeStruct((nb, d), F32)],
        scratch_shapes=[pltpu.VMEM((tm, f), BF16)],
        compiler_params=_params("arbitrary"),
        name="ffn_mix",
    )(xa, *mix_a, xb, *mix_b, wo, gains, wgu, wd)


def _ffn(x, gains, gain_rows, wgu, wd):
    n, d = x.shape
    f = wd.shape[0]
    tm = TOKEN_TILE
    row = pl.BlockSpec((tm, d), lambda i: (i, 0))
    return pl.pallas_call(
        functools.partial(_ffn_kernel, n_chunks=f // FFN_CHUNK, gain_rows=gain_rows),
        grid=(n // tm,),
        in_specs=[row, _resident(gains), _resident(wgu), _resident(wd)],
        out_specs=row,
        out_shape=jax.ShapeDtypeStruct((n, d), F32),
        scratch_shapes=[pltpu.VMEM((tm, f), BF16)],
        compiler_params=_params("parallel"),
        name="ffn",
    )(x, gains, wgu, wd)


def _proj_kernel(x_ref, g_ref, w_ref, *rest, head_major, width, gate_chunk, n_later):
    later_weights, outs = rest[:n_later], rest[n_later:]
    (qa_ref, kaf_ref, vaf_ref, ka_ref, va_ref, qm_ref, km_ref, vm_ref, om_ref, gt_ref, gtt_ref) = outs[:11]
    wgu_o, wd_o, wo_o = outs[11:] if later_weights else (None, None, None)
    xn = _rms(x_ref[...], g_ref[...]).astype(BF16)
    tm = x_ref.shape[0]
    n_heads = width // LANES

    def group(i):
        return _dot(xn, w_ref[:, i * width:(i + 1) * width])

    def put(ref, val):
        if head_major:
            for h in range(n_heads):
                ref[h] = val[:, h * LANES:(h + 1) * LANES].astype(ref.dtype)
        else:
            ref[...] = val.astype(ref.dtype)

    def put_rows(ref, val):
        for h in range(n_heads):
            ref[pl.ds(h, tm, stride=n_heads), :] = val[:, h * LANES:(h + 1) * LANES]

    put(qa_ref, group(0) * (ATT_DQ ** -0.5 * LOG2E))
    ka = group(1)
    put_rows(kaf_ref, ka)
    put(ka_ref, ka)
    va = group(2)
    put_rows(vaf_ref, va)
    put(va_ref, va)
    put(qm_ref, group(3))
    put(km_ref, group(4) * (LANES ** -0.5))
    put(vm_ref, group(5))
    om_ref[...] = group(6)
    gates = _dot(xn, w_ref[:, 7 * width:7 * width + LANES])
    gt_ref[...] = gates
    gates_t = gates.T[:2 * H_ML, :]
    for c in range(tm // gate_chunk):
        gtt_ref[c] = gates_t[:, c * gate_chunk:(c + 1) * gate_chunk]

    if later_weights:
        wg_ref, wu_ref, wd_ref, wo_ref = later_weights
        _gate_up_kernel(wg_ref, wu_ref, wgu_o)
        wd_o[...] = wd_ref[...].astype(BF16)
        wo_o[...] = wo_ref[...].astype(BF16)


def _rows_per_step(total, steps):
    share = 1
    while steps % share or total % (steps // share) or (total // (steps // share)) % (2 * SUBLANES):
        share += 1
    return total // (steps // share), share


def _proj(x, gain, w_in_p, head_major, gate_chunk, later=None):
    n, d = x.shape
    width = H_ATT * LANES
    n_heads = width // LANES
    tm = TOKEN_TILE
    row = lambda w: pl.BlockSpec((tm, w), lambda i: (i, 0))
    if head_major:
        hm_spec = pl.BlockSpec((n_heads, tm, LANES), lambda i: (0, i, 0))
        hm_shape = jax.ShapeDtypeStruct((n_heads, n, LANES), BF16)
    else:
        hm_spec = row(width)
        hm_shape = jax.ShapeDtypeStruct((n, width), F32)
    f32_shape = jax.ShapeDtypeStruct((n, width), F32)
    rows_spec = pl.BlockSpec((tm * n_heads, LANES), lambda i: (i, 0))
    rows_shape = jax.ShapeDtypeStruct((n * n_heads, LANES), F32)
    out_specs = [hm_spec, rows_spec, rows_spec, hm_spec, hm_spec, hm_spec, hm_spec, hm_spec, row(width),
                 row(LANES), pl.BlockSpec((tm // gate_chunk, 2 * H_ML, gate_chunk), lambda i: (i, 0, 0))]
    out_shape = [hm_shape, rows_shape, rows_shape, hm_shape, hm_shape, hm_shape, hm_shape, hm_shape, f32_shape,
                 jax.ShapeDtypeStruct((n, LANES), F32),
                 jax.ShapeDtypeStruct((n // gate_chunk, 2 * H_ML, gate_chunk), F32)]
    in_specs = [row(d), _resident(gain), _resident(w_in_p, 0)]
    args = [x, gain, w_in_p]
    if later is not None:
        wg3, wu3, wd3, group, wo3, layer = later
        steps = n // tm
        f = wg3.shape[-1]
        r_gu, s_gu = _rows_per_step(d, steps)
        r_d, s_d = _rows_per_step(f, steps)
        gu_spec = pl.BlockSpec((None, r_gu, f), lambda i: (group, i // s_gu, 0))
        in_specs += [gu_spec, gu_spec, pl.BlockSpec((None, r_d, d), lambda i: (group, i // s_d, 0)),
                     pl.BlockSpec((None, r_gu, d), lambda i: (layer, i // s_gu, 0))]
        args += [wg3, wu3, wd3, wo3]
        out_specs += [pl.BlockSpec((r_gu, 2 * f), lambda i: (i // s_gu, 0)),
                      pl.BlockSpec((r_d, d), lambda i: (i // s_d, 0)),
                      pl.BlockSpec((r_gu, d), lambda i: (i // s_gu, 0))]
        out_shape += [jax.ShapeDtypeStruct((d, 2 * f), BF16), jax.ShapeDtypeStruct((f, d), BF16),
                      jax.ShapeDtypeStruct((d, d), BF16)]
    return pl.pallas_call(
        functools.partial(_proj_kernel, head_major=head_major, width=width, gate_chunk=gate_chunk,
                          n_later=0 if later is None else 4),
        grid=(n // tm,),
        in_specs=in_specs,
        out_specs=out_specs,
        out_shape=out_shape,
        compiler_params=_params("arbitrary" if later is not None else "parallel"),
        name="proj_prompt" if head_major else "proj_sample",
    )(*args)


def _t5_bias(rel, rb_ref, h):
    n = jnp.maximum(rel, 0)
    max_exact = NUM_BUCKETS // 2
    x = (jnp.log(jnp.maximum(n, max_exact).astype(F32) / max_exact)
         / math.log(MAX_DISTANCE / max_exact) * (NUM_BUCKETS - max_exact))
    out = jnp.full(rel.shape, rb_ref[max_exact, h], F32)
    for i in range(1, NUM_BUCKETS - max_exact):
        out = jnp.where(x >= i, rb_ref[max_exact + i, h], out)
    for i in range(max_exact):
        out = jnp.where(n == i, rb_ref[i, h], out)
    return (out - rb_ref[NUM_BUCKETS - 1, h]) * LOG2E


def _prompt_bias_kernel(rb_ref, o_ref, *, tile):
    r = lax.broadcasted_iota(jnp.int32, (tile, tile), 0)
    c = lax.broadcasted_iota(jnp.int32, (tile, tile), 1)
    for h in range(H_ATT):
        o_ref[h, 0] = jnp.where(c <= r, _t5_bias(r - c, rb_ref, h), -jnp.inf)
        o_ref[h, 1] = _t5_bias(tile + r - c, rb_ref, h)


def _prompt_bias(rel_bias, tile):
    return pl.pallas_call(
        functools.partial(_prompt_bias_kernel, tile=tile),
        in_specs=[pl.BlockSpec(memory_space=pltpu.SMEM)],
        out_shape=jax.ShapeDtypeStruct((H_ATT, 2, tile, tile), F32),
        name="prompt_bias",
    )(rel_bias)


def _sample_bias_kernel(rb_ref, o_ref, *, page, valid):
    qi = lax.broadcasted_iota(jnp.int32, (SUBLANES, 2 * page), 0)
    col = lax.broadcasted_iota(jnp.int32, (SUBLANES, 2 * page), 1)
    kj = col - page
    for h in range(H_ATT):
        past = _t5_bias(page + qi - col, rb_ref, h)
        new = jnp.where((kj <= qi) & (kj < valid), _t5_bias(qi - kj, rb_ref, h), -jnp.inf)
        tile = jnp.where(col < page, past, new)
        for comp in range(2):
            o_ref[(comp * H_ATT + h) * SUBLANES:(comp * H_ATT + h + 1) * SUBLANES, :] = tile


def _sample_bias(rel_bias, page, valid):
    return pl.pallas_call(
        functools.partial(_sample_bias_kernel, page=page, valid=valid),
        in_specs=[pl.BlockSpec(memory_space=pltpu.SMEM)],
        out_shape=jax.ShapeDtypeStruct((2 * H_ATT * SUBLANES, 2 * page), F32),
        name="sample_bias",
    )(rel_bias)


def _lambda(lq_ref, lk_ref, lam_init):
    prod = lq_ref[...] * lk_ref[...]
    return (jnp.exp(jnp.sum(prod[0:1, :], axis=-1, keepdims=True))
            - jnp.exp(jnp.sum(prod[1:2, :], axis=-1, keepdims=True)) + lam_init)


def _attn_kernel(q_ref, k_ref, v_ref, bias_ref, lq_ref, lk_ref, gn_ref, o_ref, vx_scr, *, tile, n_q, lam_init):
    dv = v_ref.shape[-1]
    vx_scr[:, :dv] = v_ref[0]
    ones_lane = lax.broadcasted_iota(jnp.int32, (vx_scr.shape[0], vx_scr.shape[1] - dv), 1)
    vx_scr[:, dv:] = jnp.where(ones_lane == 0, 1.0, 0.0).astype(BF16)

    lane = lax.broadcasted_iota(jnp.int32, (tile, dv), 1)
    zero = jnp.zeros((tile, dv), BF16)
    lam = _lambda(lq_ref, lk_ref, lam_init)
    diag = jnp.concatenate([bias_ref[0, 0]] * 2, axis=0)
    near = jnp.concatenate([bias_ref[0, 1]] * 2, axis=0)

    def logits(n):
        q = q_ref[0, n * tile:(n + 1) * tile, :]
        qq = jnp.concatenate([jnp.where(lane < ATT_DQ, q, zero), jnp.where(lane >= ATT_DQ, q, zero)], axis=0)
        n_keys = (n + 1) * tile
        s = _dot_nt(qq, k_ref[0, :n_keys, :])
        parts = [s[:, n_keys - tile:] + diag]
        if n >= 1:
            parts.insert(0, s[:, n_keys - 2 * tile:n_keys - tile] + near)
        if n >= 2:
            parts.insert(0, s[:, :n_keys - 2 * tile])
        return parts

    def probs(parts):
        m = functools.reduce(jnp.maximum, [jnp.max(x, axis=-1, keepdims=True) for x in parts])
        return jnp.concatenate([jnp.exp2(x - m) for x in parts], axis=1).astype(BF16)

    def finish(n, p):
        vx = vx_scr[:(n + 1) * tile, :]
        acc0 = _dot(p[:tile], vx)
        acc1 = _dot(p[tile:], vx)
        out = acc0[:, :dv] / acc0[:, dv:dv + 1] - lam * (acc1[:, :dv] / acc1[:, dv:dv + 1])
        o_ref[n * tile:(n + 1) * tile, :] = (_rms(out, gn_ref[...]) * (1.0 - lam_init)).astype(o_ref.dtype)

    for g in range(0, n_q // 2, ATT_GROUP_PAIRS):
        group = [n for gg in range(g, min(g + ATT_GROUP_PAIRS, n_q // 2)) for n in (gg, n_q - 1 - gg)]
        p_group = [probs(logits(n)) for n in group]
        for n, p in zip(group, p_group):
            finish(n, p)


def _prompt_attention(q, k, v, bias, lam_q, lam_k, gain, batch, lam_init):
    h, n, dv = q.shape
    seq = n // batch
    tile = ATT_TILE
    nq = seq // tile
    assert nq % 2 == 0
    seq_spec = pl.BlockSpec((1, seq, dv), lambda b, hh: (hh, b, 0))
    return pl.pallas_call(
        functools.partial(_attn_kernel, tile=tile, n_q=nq, lam_init=lam_init),
        grid=(batch, h),
        in_specs=[seq_spec, seq_spec, seq_spec,
                  pl.BlockSpec((1, 2, tile, tile), lambda b, hh: (hh, 0, 0, 0)),
                  pl.BlockSpec(lam_q.shape, lambda b, hh: (0, 0)),
                  pl.BlockSpec(lam_k.shape, lambda b, hh: (0, 0)),
                  pl.BlockSpec(gain.shape, lambda b, hh: (0, 0))],
        out_specs=pl.BlockSpec((seq, dv), lambda b, hh: (b, hh)),
        out_shape=jax.ShapeDtypeStruct((n, h * dv), BF16),
        scratch_shapes=[pltpu.VMEM((seq, 2 * dv), BF16)],
        compiler_params=_params("parallel", "parallel"),
        name="attn_prompt",
    )(q, k, v, bias, lam_q, lam_k, gain)


def _sample_attn_core(q, load_k, load_v, load_kn, load_vn, bias_ref, lam, gn_ref, k_buf, v_buf,
                      *, n_pages, page, n_tok_new, lam_init):
    width = q.shape[-1]
    n_rows = 2 * H_ATT * SUBLANES
    qt = jnp.concatenate([q] * (2 * H_ATT), axis=0)
    r = lax.broadcasted_iota(jnp.int32, (n_rows, width), 0)
    c = lax.broadcasted_iota(jnp.int32, (n_rows, width), 1)
    comp = r // (H_ATT * SUBLANES)
    head = (r // SUBLANES) % H_ATT
    own = (c // LANES == head) & ((c // ATT_DQ) % 2 == comp)
    qbd = jnp.where(own, qt, 0.0).astype(BF16)

    n_past = n_pages * page
    new_rows = 2 * SUBLANES
    for load, load_new, buf in ((load_k, load_kn, k_buf), (load_v, load_vn, v_buf)):
        for p in range(n_pages):
            for h in range(H_ATT):
                buf[p * page:(p + 1) * page, h * LANES:(h + 1) * LANES] = load(p, h).astype(BF16)
        new = jnp.concatenate([load_new(h) for h in range(H_ATT)], axis=1)
        new = jnp.concatenate([new, jnp.zeros((new_rows - n_tok_new, width), F32)], axis=0)
        buf[n_past:n_past + new_rows, :] = new.astype(BF16)
        buf[n_past + new_rows:, :] = jnp.zeros((page - new_rows, width), BF16)

    s = _dot_nt(qbd, k_buf[...])
    parts = [s[:, :n_past - page], s[:, n_past - page:] + bias_ref[...]]
    m = functools.reduce(jnp.maximum, [jnp.max(x, axis=-1, keepdims=True) for x in parts])
    p = jnp.concatenate([jnp.exp2(x - m) for x in parts], axis=1)
    inv = 1.0 / jnp.sum(p, axis=-1, keepdims=True)
    half = n_rows // 2
    a = (p[:half] * inv[:half] - lam * (p[half:] * inv[half:])).astype(BF16)
    out = _dot(a, v_buf[...])
    heads = [_rms(out[h * SUBLANES:(h + 1) * SUBLANES, h * LANES:(h + 1) * LANES], gn_ref[...])
             * (1.0 - lam_init) for h in range(H_ATT)]
    return jnp.concatenate(heads, axis=1)


def _ffn_attn_kernel(pt_ref, x_ref, g_ref, wgu_ref, wd_ref, q_ref, kn_ref, vn_ref, bias_ref, lq_ref, lk_ref,
                     gn_ref, ck_hbm, cv_hbm, o_ref, ao_ref, u_ref, k_ring, v_ring, k_buf, v_buf, sem,
                     *, n_chunks, gain_rows, n_pages, page, first_page, n_seq, lam_init):
    _, pre_row, post_row = gain_rows
    i = pl.program_id(0)
    seqs = q_ref.shape[0]
    page_rows = page * H_ATT
    new_rows = kn_ref.shape[0] // seqs
    n_tok_new = new_rows // H_ATT

    def page_copies(g, slot):
        copies = []
        for p in range(n_pages):
            src = pl.ds(pl.multiple_of((first_page + pt_ref[g, p]) * page_rows, page_rows), page_rows)
            dst = pl.ds(p * page_rows, page_rows)
            copies.append(pltpu.make_async_copy(ck_hbm.at[src, :], k_ring.at[slot, dst, :], sem.at[slot, 0]))
            copies.append(pltpu.make_async_copy(cv_hbm.at[src, :], v_ring.at[slot, dst, :], sem.at[slot, 1]))
        return copies

    n_slots = k_ring.shape[0]

    @pl.when(i == 0)
    def _():
        for a in range(n_slots):
            for cp in page_copies(min(a, n_seq - 1), a):
                cp.start()

    x = x_ref[...]
    h = _rms(x, g_ref[pre_row:pre_row + 1, :]).astype(BF16)
    lam = _lambda(lq_ref, lk_ref, lam_init)
    chunk_bounds = [(n_chunks * j) // seqs for j in range(seqs + 1)]
    for j in range(seqs):
        g = i * seqs + j
        slot = lax.rem(g, n_slots)

        for c in range(chunk_bounds[j], chunk_bounds[j + 1]):
            gu = _dot(h, wgu_ref[:, 2 * c * FFN_CHUNK:2 * (c + 1) * FFN_CHUNK])
            gate = gu[:, :FFN_CHUNK]
            up = gu[:, FFN_CHUNK:]
            u_ref[:, c * FFN_CHUNK:(c + 1) * FFN_CHUNK] = (gate * jax.nn.sigmoid(gate) * up).astype(BF16)

        for cp in page_copies(g, slot):
            cp.wait()

        def past(ring, p, hh, slot=slot):
            return ring[slot, pl.ds(p * page_rows + hh, page, stride=H_ATT), :]

        def new(ref, hh, j=j):
            return ref[pl.ds(j * new_rows + hh, n_tok_new, stride=H_ATT), :]

        ao_ref[j] = _sample_attn_core(
            q_ref[j], functools.partial(past, k_ring), functools.partial(past, v_ring),
            functools.partial(new, kn_ref), functools.partial(new, vn_ref), bias_ref, lam, gn_ref, k_buf, v_buf,
            n_pages=n_pages, page=page, n_tok_new=n_tok_new, lam_init=lam_init)

        for cp in page_copies(jnp.minimum(g + n_slots, n_seq - 1), slot):
            cp.start()

    y = _dot(u_ref[...], wd_ref[...])
    o_ref[...] = x + 0.5 * _rms(y, g_ref[post_row:post_row + 1, :])

    @pl.when(i == pl.num_programs(0) - 1)
    def _():
        for a in range(n_slots):
            for cp in page_copies(n_seq - 1, a):
                cp.wait()


def _ffn_with_sample_attention(x, gains, gain_rows, wgu, wd, page_table, q, k_new, v_new, cache_k, cache_v,
                               page, first_page, bias, lam_q, lam_k, gain, lam_init):
    n, d = x.shape
    f = wd.shape[0]
    tm = TOKEN_TILE
    steps = n // tm
    nb, rows, width = q.shape
    n_pages = page_table.shape[1]
    n_heads = width // LANES
    seqs = nb // steps
    assert nb == seqs * steps and nb >= PAGE_RING_SLOTS
    ring = pltpu.VMEM((PAGE_RING_SLOTS, n_pages * page * n_heads, LANES), F32)
    stage = pltpu.VMEM(((n_pages + 1) * page, width), BF16)
    full = lambda a: pl.BlockSpec(a.shape, lambda i, pt: (0,) * a.ndim)
    seq_spec = pl.BlockSpec((seqs, rows, width), lambda i, pt: (i, 0, 0))
    new_spec = pl.BlockSpec((seqs * rows * n_heads, LANES), lambda i, pt: (i, 0))
    row_spec = pl.BlockSpec((tm, d), lambda i, pt: (i, 0))
    grid_spec = pltpu.PrefetchScalarGridSpec(
        num_scalar_prefetch=1,
        grid=(steps,),
        in_specs=[row_spec, _resident(gains), _resident(wgu), _resident(wd),
                  seq_spec, new_spec, new_spec, full(bias), full(lam_q), full(lam_k), full(gain),
                  pl.BlockSpec(memory_space=pl.ANY), pl.BlockSpec(memory_space=pl.ANY)],
        out_specs=[row_spec, seq_spec],
        scratch_shapes=[pltpu.VMEM((tm, f), BF16), ring, ring, stage, stage,
                        pltpu.SemaphoreType.DMA((PAGE_RING_SLOTS, 2))],
    )
    return pl.pallas_call(
        functools.partial(_ffn_attn_kernel, n_chunks=f // FFN_CHUNK, gain_rows=gain_rows, n_pages=n_pages,
                          page=page, first_page=first_page, n_seq=nb, lam_init=lam_init),
        grid_spec=grid_spec,
        out_shape=[jax.ShapeDtypeStruct((n, d), F32), jax.ShapeDtypeStruct((nb, rows, width), F32)],
        compiler_params=pltpu.CompilerParams(dimension_semantics=("arbitrary",),
                                             vmem_limit_bytes=FUSED_VMEM_LIMIT_BYTES),
        name="ffn_attn_sample",
    )(page_table, x, gains, wgu, wd, q, k_new, v_new, bias, lam_q, lam_k, gain, cache_k, cache_v)


def _log_sigmoid(x):
    return jnp.minimum(x, 0.0) - jnp.log1p(jnp.exp(-jnp.abs(x)))


def _mlstm_step_kernel(q_ref, k_ref, v_ref, g_ref, gt_ref, om_ref, brow_ref, bcol_ref, gn_ref,
                       c0_ref, n0_ref, m0_ref, hm_ref, c_ref, n_ref, m_ref, *, rows_per_seq, valid):
    n_seq = c0_ref.shape[0]
    P = rows_per_seq
    R = n_seq * P
    r_i = lax.broadcasted_iota(jnp.int32, (R, R), 0)
    c_i = lax.broadcasted_iota(jnp.int32, (R, R), 1)
    same = (r_i // P) == (c_i // P)
    causal = same & (c_i <= r_i)
    hi = lax.Precision.HIGHEST

    g_col = g_ref[...] + brow_ref[...]
    g_row = gt_ref[0] + bcol_ref[...]
    lf_col = _log_sigmoid(g_col)
    lf_row = _log_sigmoid(g_row)
    tok_c = lax.broadcasted_iota(jnp.int32, g_col.shape, 0) % P
    tok_r = lax.broadcasted_iota(jnp.int32, g_row.shape, 1) % P
    g_col = jnp.where(tok_c < valid, g_col, -1e30)
    g_row = jnp.where(tok_r < valid, g_row, -1e30)
    lf_col = jnp.where(tok_c < valid, lf_col, 0.0)
    lf_row = jnp.where(tok_r < valid, lf_row, 0.0)
    same_f = same.astype(F32)
    b_col = jnp.dot(causal.astype(F32), lf_col, preferred_element_type=F32, precision=hi)
    b_row = jnp.dot(lf_row, (same & (r_i <= c_i)).astype(F32), preferred_element_type=F32, precision=hi)
    bl_col = jnp.dot(same_f, lf_col, preferred_element_type=F32, precision=hi)
    bl_row = jnp.dot(lf_row, same_f, preferred_element_type=F32, precision=hi)

    for h in range(H_ML):
        cols = slice(h * LANES, (h + 1) * LANES)
        q = q_ref[:, cols]
        k = k_ref[:, cols]
        v = v_ref[:, cols]
        bc = b_col[:, H_ML + h:H_ML + h + 1]
        ic = g_col[:, h:h + 1]
        br = b_row[H_ML + h:H_ML + h + 1, :]
        ir = g_row[h:h + 1, :]
        blc = bl_col[:, H_ML + h:H_ML + h + 1]
        blr = bl_row[H_ML + h:H_ML + h + 1, :]
        m = m0_ref[:, h:h + 1]

        log_d = jnp.where(causal, bc - br + ir, -jnp.inf)
        inter = bc + m
        m_t = jnp.maximum(inter, jnp.max(log_d, axis=-1, keepdims=True))
        s = _dot_nt(q.astype(BF16), k.astype(BF16)) * jnp.exp(log_d - m_t)
        w_inter = jnp.exp(inter - m_t)
        q_c = jnp.concatenate([_dot(q[j * P:(j + 1) * P].astype(BF16), c0_ref[j, h].astype(BF16))
                               for j in range(n_seq)], axis=0)
        n_rows = jnp.concatenate([jnp.broadcast_to(n0_ref[j, h:h + 1, :], (P, LANES)) for j in range(n_seq)],
                                 axis=0)
        num = _dot(s.astype(BF16), v.astype(BF16)) + w_inter * q_c
        den = jnp.sum(s, axis=-1, keepdims=True) + w_inter * jnp.sum(q * n_rows, axis=-1, keepdims=True)
        hv = num / jnp.maximum(jnp.abs(den), jnp.exp(-m_t))
        hm_ref[:, cols] = _rms(hv, gn_ref[...]) * jax.nn.sigmoid(om_ref[:, cols])

        w_max = jnp.max(jnp.where(same, blr - br + ir, -jnp.inf), axis=-1, keepdims=True)
        m_new = jnp.maximum(blc + m, w_max)
        ws = jnp.exp(blc - bc + ic - m_new)
        fw = jnp.exp(blc + m - m_new)
        wv = ws * v
        wk = ws * k
        for j in range(n_seq):
            rows = slice(j * P, (j + 1) * P)
            fw_j = fw[j * P:j * P + 1, :]
            c_ref[j, h] = fw_j * c0_ref[j, h] + _dot_tn(k[rows].astype(BF16), wv[rows].astype(BF16))
            n_ref[j, h:h + 1, :] = fw_j * n0_ref[j, h:h + 1, :] + jnp.sum(wk[rows], axis=0, keepdims=True)
            m_ref[j, h:h + 1, :] = jnp.broadcast_to(m_new[j * P:j * P + 1, :], (1, LANES))


def _mlstm_step(q, k, v, gates, gates_t, om, b_row, b_col, gain, state_c, state_n, m_rows, *, rows_per_seq,
                valid, n_seq):
    n, width = om.shape
    batch = state_c.shape[0]
    dh = LANES
    rows = n_seq * rows_per_seq
    row = lambda w: pl.BlockSpec((rows, w), lambda b: (b, 0))
    full = lambda a: pl.BlockSpec(a.shape, lambda b: (0,) * a.ndim)
    c_spec = pl.BlockSpec((n_seq, H_ML, dh, dh), lambda b: (b, 0, 0, 0))
    v_spec = pl.BlockSpec((n_seq, H_ML, dh), lambda b: (b, 0, 0))
    return pl.pallas_call(
        functools.partial(_mlstm_step_kernel, rows_per_seq=rows_per_seq, valid=valid),
        grid=(batch // n_seq,),
        in_specs=[row(width), row(width), row(width), row(LANES),
                  pl.BlockSpec((1, 2 * H_ML, rows), lambda b: (b, 0, 0)),
                  row(width), full(b_row), full(b_col), full(gain), c_spec, v_spec, row(LANES)],
        out_specs=[row(width), c_spec, v_spec, v_spec],
        out_shape=[jax.ShapeDtypeStruct((n, width), F32),
                   jax.ShapeDtypeStruct((batch, H_ML, dh, dh), F32),
                   jax.ShapeDtypeStruct((batch, H_ML, dh), F32),
                   jax.ShapeDtypeStruct((batch, H_ML, dh), F32)],
        compiler_params=_params("parallel"),
        name="mlstm_sample",
    )(q, k, v, gates, gates_t, om, b_row, b_col, gain, state_c, state_n, m_rows)


def _mlstm_seq_kernel(q_ref, k_ref, v_ref, g_ref, gt_ref, om_ref, brow_ref, bcol_ref, gn_ref,
                      hm_ref, c_ref, n_ref, m_ref, fcol_scr, vx_scr, *, tile):
    seq = g_ref.shape[0]
    n_t = seq // tile
    dv = v_ref.shape[-1]
    t_i = lax.broadcasted_iota(jnp.int32, (tile, tile), 0)
    s_i = lax.broadcasted_iota(jnp.int32, (tile, tile), 1)
    causal = s_i <= t_i
    lower = causal.astype(F32)
    upper = (t_i <= s_i).astype(F32)

    off_col = jnp.zeros((1, LANES), F32)
    off_row = jnp.zeros((2 * H_ML, 1), F32)
    u_tiles = []
    for j in range(n_t):
        rows = slice(j * tile, (j + 1) * tile)
        lf_col = _log_sigmoid(g_ref[rows, :] + brow_ref[...])
        f_col = jnp.dot(lower, lf_col, preferred_element_type=F32, precision=lax.Precision.HIGHEST) + off_col
        fcol_scr[rows, :] = f_col
        off_col = f_col[tile - 1:tile, :]
        g_row = gt_ref[j] + bcol_ref[...]
        f_row = jnp.dot(_log_sigmoid(g_row), upper, preferred_element_type=F32,
                        precision=lax.Precision.HIGHEST) + off_row
        off_row = f_row[:, tile - 1:tile]
        u_tiles.append((g_row[:H_ML, :] - f_row[H_ML:, :]) * LOG2E)

    ones_lane = lax.broadcasted_iota(jnp.int32, (seq, vx_scr.shape[2] - dv), 1)
    for h in range(H_ML):
        vx_scr[h, :, :dv] = v_ref[h]
        vx_scr[h, :, dv:] = jnp.where(ones_lane == 0, 1.0, 0.0).astype(BF16)

    for h in range(H_ML):
        u_row = jnp.concatenate([u[h:h + 1, :] for u in u_tiles], axis=1)
        tile_max = [jnp.max(u[h:h + 1, :], axis=-1, keepdims=True) for u in u_tiles]
        before = [None]
        for j in range(n_t):
            before.append(tile_max[j] if before[-1] is None else jnp.maximum(before[-1], tile_max[j]))

        def weights(n):
            n_keys = (n + 1) * tile
            s = _dot_nt(q_ref[h, n * tile:(n + 1) * tile, :], k_ref[h, :n_keys, :])
            u_diag = jnp.where(causal, u_row[:, n_keys - tile:n_keys], -jnp.inf)
            r = jnp.max(u_diag, axis=-1, keepdims=True)
            if before[n] is not None:
                r = jnp.maximum(r, before[n])
            r0 = jnp.maximum(r, 0.0)
            parts = [jnp.exp2(u_diag - r0)]
            if n >= 1:
                parts.insert(0, jnp.exp2(u_row[:, :n_keys - tile] - r0))
            return (s * jnp.concatenate(parts, axis=1)).astype(BF16), r0 * (1.0 / LOG2E)

        def finish(n, p, r0):
            rows = slice(n * tile, (n + 1) * tile)
            acc = _dot(p, vx_scr[h, :(n + 1) * tile, :])
            m_t = fcol_scr[rows, H_ML + h:H_ML + h + 1] + r0
            hv = acc[:, :dv] / jnp.maximum(jnp.abs(acc[:, dv:dv + 1]), jnp.exp(-m_t))
            gate = jax.nn.sigmoid(om_ref[rows, h * LANES:(h + 1) * LANES])
            hm_ref[rows, h * LANES:(h + 1) * LANES] = (_rms(hv, gn_ref[...]) * gate).astype(hm_ref.dtype)

        for g in range(n_t // 2):
            pair = (g, n_t - 1 - g)
            done = [weights(n) for n in pair]
            for n, (p, r0) in zip(pair, done):
                finish(n, p, r0)

        r0_last = jnp.maximum(before[n_t], 0.0) * (1.0 / LOG2E)
        u_col = (g_ref[:, h:h + 1] + brow_ref[:, h:h + 1]) - fcol_scr[:, H_ML + h:H_ML + h + 1]
        w_col = jnp.exp(u_col - r0_last)
        kf = k_ref[h]
        c_ref[0, h] = _dot_tn(kf, (w_col * v_ref[h].astype(F32)).astype(BF16))
        n_ref[0, h:h + 1, :] = jnp.sum(w_col * kf.astype(F32), axis=0, keepdims=True)
        m_last = off_col[:, H_ML + h:H_ML + h + 1] + r0_last
        m_ref[0, h:h + 1, :] = jnp.broadcast_to(m_last, (1, LANES))


def _mlstm_seq(q, k, v, gates, gates_t, om, b_row, b_col, gain, *, batch, tile):
    n, width = om.shape
    seq = n // batch
    n_t = seq // tile
    assert n_t % 2 == 0
    dh = LANES
    qkv_spec = pl.BlockSpec((H_ML, seq, dh), lambda b: (0, b, 0))
    row = lambda w: pl.BlockSpec((seq, w), lambda b: (b, 0))
    full = lambda a: pl.BlockSpec(a.shape, lambda b: (0,) * a.ndim)
    c_spec = pl.BlockSpec((1, H_ML, dh, dh), lambda b: (b, 0, 0, 0))
    v_spec = pl.BlockSpec((1, H_ML, dh), lambda b: (b, 0, 0))
    return pl.pallas_call(
        functools.partial(_mlstm_seq_kernel, tile=tile),
        grid=(batch,),
        in_specs=[qkv_spec, qkv_spec, qkv_spec, row(LANES),
                  pl.BlockSpec((n_t, 2 * H_ML, tile), lambda b: (b, 0, 0)),
                  row(width), full(b_row), full(b_col), full(gain)],
        out_specs=[row(width), c_spec, v_spec, v_spec],
        out_shape=[jax.ShapeDtypeStruct((n, width), BF16),
                   jax.ShapeDtypeStruct((batch, H_ML, dh, dh), F32),
                   jax.ShapeDtypeStruct((batch, H_ML, dh), F32),
                   jax.ShapeDtypeStruct((batch, H_ML, dh), F32)],
        scratch_shapes=[pltpu.VMEM((seq, LANES), F32), pltpu.VMEM((H_ML, seq, 2 * dh), BF16)],
        compiler_params=_params("parallel"),
        name="mlstm_prompt",
    )(q, k, v, gates, gates_t, om, b_row, b_col, gain)


def kernel(x_prompt, x_sample, cache_k, cache_v, state_C, state_n, state_m, page_table, rel_bias, norm_gains,
           ffn_w_gate, ffn_w_up, ffn_w_down, w_in, b_gates, lam_q, lam_k, attn_norm, mlstm_norm, w_out):
    batch, seq, d = x_prompt.shape
    dec_batch, dec_seq, _ = x_sample.shape
    depth = w_in.shape[0]
    page = cache_k.shape[2]
    width = H_ATT * LANES
    d_in = w_in.shape[-1]
    n_gate = 2 * H_ML
    assert d_in == 7 * width + n_gate and dec_seq <= SAMPLE_ROWS
    assert seq % ATT_TILE == 0 and seq % MLSTM_TILE == 0 and (batch * seq) % TOKEN_TILE == 0

    xp = x_prompt.reshape(batch * seq, d)
    xs = jnp.pad(x_sample, ((0, 0), (0, SAMPLE_ROWS - dec_seq), (0, 0))).reshape(dec_batch * SAMPLE_ROWS, d)

    prompt_bias = _prompt_bias(rel_bias, ATT_TILE)
    sample_bias = _sample_bias(rel_bias, page, dec_seq)

    outs = [[] for _ in range(10)]
    for l in range(depth):
        lam_init = 0.8 - 0.6 * math.exp(-0.3 * l)
        g = norm_gains[l]
        f = ffn_w_gate.shape[-1]
        wg3, wu3, wd3 = ffn_w_gate.reshape(-1, d, f), ffn_w_up.reshape(-1, d, f), ffn_w_down.reshape(-1, f, d)
        wgu1 = _fuse_gate_up(wg3, wu3, 2 * l, 1, WEIGHT_ROW_TILE)[0]
        wd1 = _cast_groups(wd3, 2 * l, 1, d, f // 4)[0]
        w_in_p = _cast_transposed(jnp.swapaxes(w_in, 1, 2), l, d_in - n_gate + LANES, WEIGHT_ROW_TILE)
        b_row = jnp.pad(b_gates[l].reshape(1, n_gate), ((0, 0), (0, LANES - n_gate)))
        b_col = b_gates[l].reshape(n_gate, 1)
        gn_att = attn_norm[l].reshape(1, -1)
        gn_ml = mlstm_norm[l].reshape(1, -1)
        ck = cache_k.reshape(-1, LANES)
        cv = cache_v.reshape(-1, LANES)
        first_page = l * cache_k.shape[1]

        xs = _ffn(xs, g, (None, 0, 1), wgu1, wd1)
        sample_seqs = math.gcd(dec_batch, MLSTM_SAMPLE_SEQS)
        (qa_s, kaf_s, vaf_s, _, _, qm_s, km_s, vm_s, om_s, gt_s, gtt_s) = _proj(xs, g[2:3], w_in_p, False,
                                                                                sample_seqs * SAMPLE_ROWS)
        seq3 = lambda a: a.reshape(dec_batch, SAMPLE_ROWS, width)
        xp, mix_a_s = _ffn_with_sample_attention(xp, g, (None, 0, 1), wgu1, wd1, page_table, seq3(qa_s), kaf_s,
                                                 vaf_s, ck, cv, page, first_page, sample_bias, lam_q[l], lam_k[l],
                                                 gn_att, lam_init)

        (qa, kaf, vaf, ka, va, qm, km, vm, om, gt, gtt, wgu2, wd2, wo) = _proj(
            xp, g[2:3], w_in_p, True, MLSTM_TILE, later=(wg3, wu3, wd3, 2 * l + 1, w_out, l))
        mix_a = _prompt_attention(qa, ka, va, prompt_bias, lam_q[l], lam_k[l], gn_att, batch, lam_init)
        mix_m, c_p, n_p, m_p = _mlstm_seq(qm, km, vm, gt, gtt, om, b_row, b_col, gn_ml, batch=batch, tile=MLSTM_TILE)

        m_rows = jnp.pad(jnp.repeat(state_m[l], SAMPLE_ROWS, axis=0), ((0, 0), (0, LANES - H_ML)))
        mix_m_s, c_s, n_s, m_s = _mlstm_step(qm_s, km_s, vm_s, gt_s, gtt_s, om_s, b_row, b_col, gn_ml,
                                             state_C[l], state_n[l], m_rows, rows_per_seq=SAMPLE_ROWS,
                                             valid=dec_seq, n_seq=sample_seqs)

        xp, xs = _ffn_two(xp, (mix_a, mix_m), xs, (mix_a_s.reshape(-1, width), mix_m_s), wo, g, (3, 4, 5),
                          wgu2, wd2)

        new = lambda a: a.reshape(dec_batch, SAMPLE_ROWS, H_ATT, LANES)[:, :dec_seq]
        for lst, val in zip(outs, (kaf.reshape(batch, seq, H_ATT, LANES), vaf.reshape(batch, seq, H_ATT, LANES),
                                   c_p, n_p, m_p[:, :, 0], new(kaf_s), new(vaf_s), c_s, n_s, m_s[:, :, 0])):
            lst.append(val)

    y_prompt = xp.reshape(batch, seq, d)
    y_sample = xs.reshape(dec_batch, SAMPLE_ROWS, d)[:, :dec_seq]
    return (y_prompt, y_sample) + tuple(jnp.stack(o) for o in outs)
```

```python
import functools
import math

import jax
import jax.numpy as jnp
from jax import lax
from jax.experimental import pallas as pl
from jax.experimental.pallas import tpu as pltpu

F32 = jnp.float32
BF16 = jnp.bfloat16

H_ATT = 4
ATT_DQ = 64
H_ML = 4
NUM_BUCKETS = 32
MAX_DISTANCE = 128
EPS = 1e-6
LOG2E = math.log2(math.e)

LANES = 128
SUBLANES = 8
VMEM_LIMIT_BYTES = 56 * 1024 * 1024
FUSED_VMEM_LIMIT_BYTES = 60 * 1024 * 1024
PAGE_RING_SLOTS = 2

TOKEN_TILE = 512
FFN_CHUNK = 256
ATT_TILE = 256
ATT_GROUP_PAIRS = 2
SAMPLE_ROWS = 8
WEIGHT_ROW_TILE = 256
MLSTM_TILE = 256
MLSTM_SAMPLE_SEQS = 32


def _params(*sem):
    return pltpu.CompilerParams(dimension_semantics=sem, vmem_limit_bytes=VMEM_LIMIT_BYTES)


def _rms(x, g):
    return x * lax.rsqrt(jnp.mean(x * x, axis=-1, keepdims=True) + EPS) * g


def _dot(a, b):
    return jnp.dot(a, b, preferred_element_type=F32)


def _dot_nt(a, b):
    return lax.dot_general(a, b, (((1,), (1,)), ((), ())), preferred_element_type=F32)


def _dot_tn(a, b):
    return lax.dot_general(a, b, (((0,), (0,)), ((), ())), preferred_element_type=F32)


def _resident(arr, lead=None):
    if lead is None:
        return pl.BlockSpec(arr.shape, lambda *_: (0,) * arr.ndim, pipeline_mode=pl.Buffered(1))
    return pl.BlockSpec((None,) + arr.shape[1:], lambda *_: (lead,) + (0,) * (arr.ndim - 1),
                        pipeline_mode=pl.Buffered(1))


def _cast_kernel(x_ref, o_ref):
    w = x_ref.shape[-1]
    if o_ref.shape[-1] != w:
        o_ref[...] = jnp.zeros(o_ref.shape, o_ref.dtype)
    o_ref[:, :w] = x_ref[...].astype(o_ref.dtype)


def _cast_groups(x3, first, count, cols_out, row_tile):
    _, rows, cols = x3.shape
    return pl.pallas_call(
        _cast_kernel,
        grid=(count, rows // row_tile),
        in_specs=[pl.BlockSpec((None, row_tile, cols), lambda g, r: (first + g, r, 0))],
        out_specs=pl.BlockSpec((None, row_tile, cols_out), lambda g, r: (g, r, 0)),
        out_shape=jax.ShapeDtypeStruct((count, rows, cols_out), BF16),
        compiler_params=_params("parallel", "parallel"),
        name="cast_weights",
    )(x3)


def _cast_transposed_kernel(x_ref, o_ref):
    rows = x_ref.shape[0]
    pad = jnp.zeros((o_ref.shape[-1] - rows, x_ref.shape[1]), F32)
    o_ref[...] = jnp.concatenate([x_ref[...], pad], axis=0).T.astype(o_ref.dtype)


def _cast_transposed(xt3, group, cols_out, col_tile):
    _, cols, rows = xt3.shape
    return pl.pallas_call(
        _cast_transposed_kernel,
        grid=(rows // col_tile,),
        in_specs=[pl.BlockSpec((None, cols, col_tile), lambda r: (group, 0, r))],
        out_specs=pl.BlockSpec((None, col_tile, cols_out), lambda r: (0, r, 0)),
        out_shape=jax.ShapeDtypeStruct((1, rows, cols_out), BF16),
        compiler_params=_params("parallel"),
        name="cast_weights_t",
    )(xt3)


def _gate_up_kernel(g_ref, u_ref, o_ref):
    for c in range(g_ref.shape[-1] // FFN_CHUNK):
        src = slice(c * FFN_CHUNK, (c + 1) * FFN_CHUNK)
        o_ref[:, 2 * c * FFN_CHUNK:(2 * c + 1) * FFN_CHUNK] = g_ref[:, src].astype(BF16)
        o_ref[:, (2 * c + 1) * FFN_CHUNK:(2 * c + 2) * FFN_CHUNK] = u_ref[:, src].astype(BF16)


def _fuse_gate_up(wg3, wu3, first, count, row_tile):
    _, rows, cols = wg3.shape
    spec = pl.BlockSpec((None, row_tile, cols), lambda g, r: (first + g, r, 0))
    return pl.pallas_call(
        _gate_up_kernel,
        grid=(count, rows // row_tile),
        in_specs=[spec, spec],
        out_specs=pl.BlockSpec((None, row_tile, 2 * cols), lambda g, r: (g, r, 0)),
        out_shape=jax.ShapeDtypeStruct((count, rows, 2 * cols), BF16),
        compiler_params=_params("parallel", "parallel"),
        name="fuse_gate_up",
    )(wg3, wu3)


def _ffn_kernel(x_ref, g_ref, wgu_ref, wd_ref, o_ref, u_ref, *, n_chunks, gain_rows):
    _, pre_row, post_row = gain_rows
    x = x_ref[...]
    h = _rms(x, g_ref[pre_row:pre_row + 1, :]).astype(BF16)
    for c in range(n_chunks):
        gu = _dot(h, wgu_ref[:, 2 * c * FFN_CHUNK:2 * (c + 1) * FFN_CHUNK])
        gate = gu[:, :FFN_CHUNK]
        up = gu[:, FFN_CHUNK:]
        u_ref[:, c * FFN_CHUNK:(c + 1) * FFN_CHUNK] = (gate * jax.nn.sigmoid(gate) * up).astype(BF16)
    y = _dot(u_ref[...], wd_ref[...])
    o_ref[...] = x + 0.5 * _rms(y, g_ref[post_row:post_row + 1, :])


def _ffn_two_kernel(xa_ref, maa_ref, mma_ref, xb_ref, mab_ref, mmb_ref, wo_ref, g_ref, wgu_ref, wd_ref,
                    oa_ref, ob_ref, u_ref, *, n_chunks, gain_rows, tiles_a):
    mix_row, pre_row, post_row = gain_rows
    first = pl.program_id(0) < tiles_a

    def run(x_ref, ma_ref, mm_ref, o_ref):
        half = ma_ref.shape[-1]
        y = _dot(ma_ref[...].astype(BF16), wo_ref[:half, :]) + _dot(mm_ref[...].astype(BF16), wo_ref[half:, :])
        x = x_ref[...] + _rms(y, g_ref[mix_row:mix_row + 1, :])
        h = _rms(x, g_ref[pre_row:pre_row + 1, :]).astype(BF16)
        for c in range(n_chunks):
            gu = _dot(h, wgu_ref[:, 2 * c * FFN_CHUNK:2 * (c + 1) * FFN_CHUNK])
            gate = gu[:, :FFN_CHUNK]
            up = gu[:, FFN_CHUNK:]
            u_ref[:, c * FFN_CHUNK:(c + 1) * FFN_CHUNK] = (gate * jax.nn.sigmoid(gate) * up).astype(BF16)
        o_ref[...] = x + 0.5 * _rms(_dot(u_ref[...], wd_ref[...]), g_ref[post_row:post_row + 1, :])

    @pl.when(first)
    def _():
        run(xa_ref, maa_ref, mma_ref, oa_ref)

    @pl.when(jnp.logical_not(first))
    def _():
        run(xb_ref, mab_ref, mmb_ref, ob_ref)


def _ffn_two(xa, mix_a, xb, mix_b, wo, gains, gain_rows, wgu, wd):
    (na, d), nb = xa.shape, xb.shape[0]
    f = wd.shape[0]
    tm = TOKEN_TILE
    tiles_a, tiles_b = na // tm, nb // tm
    spec_a = lambda w: pl.BlockSpec((tm, w), lambda i: (jnp.minimum(i, tiles_a - 1), 0))
    spec_b = lambda w: pl.BlockSpec((tm, w), lambda i: (jnp.maximum(i - tiles_a, 0), 0))
    half = mix_a[0].shape[1]
    return pl.pallas_call(
        functools.partial(_ffn_two_kernel, n_chunks=f // FFN_CHUNK, gain_rows=gain_rows, tiles_a=tiles_a),
        grid=(tiles_a + tiles_b,),
        in_specs=[spec_a(d), spec_a(half), spec_a(half), spec_b(d), spec_b(half), spec_b(half),
                  _resident(wo), _resident(gains), _resident(wgu), _resident(wd)],
        out_specs=[spec_a(d), spec_b(d)],
        out_shape=[jax.ShapeDtypeStruct((na, d), F32), jax.ShapeDtypeStruct((nb, d), F32)],
        scratch_shapes=[pltpu.VMEM((tm, f), BF16)],
        compiler_params=_params("arbitrary"),
        name="ffn_mix",
    )(xa, *mix_a, xb, *mix_b, wo, gains, wgu, wd)


def _ffn(x, gains, gain_rows, wgu, wd):
    n, d = x.shape
    f = wd.shape[0]
    tm = TOKEN_TILE
    row = pl.BlockSpec((tm, d), lambda i: (i, 0))
    return pl.pallas_call(
        functools.partial(_ffn_kernel, n_chunks=f // FFN_CHUNK, gain_rows=gain_rows),
        grid=(n // tm,),
        in_specs=[row, _resident(gains), _resident(wgu), _resident(wd)],
        out_specs=row,
        out_shape=jax.ShapeDtypeStruct((n, d), F32),
        scratch_shapes=[pltpu.VMEM((tm, f), BF16)],
        compiler_params=_params("parallel"),
        name="ffn",
    )(x, gains, wgu, wd)


def _proj_kernel(x_ref, g_ref, w_ref, *rest, head_major, width, gate_chunk, n_later):
    later_weights, outs = rest[:n_later], rest[n_later:]
    (qa_ref, kaf_ref, vaf_ref, ka_ref, va_ref, qm_ref, km_ref, vm_ref, om_ref, gt_ref, gtt_ref) = outs[:11]
    wgu_o, wd_o, wo_o = outs[11:] if later_weights else (None, None, None)
    xn = _rms(x_ref[...], g_ref[...]).astype(BF16)
    tm = x_ref.shape[0]
    n_heads = width // LANES

    def group(i):
        return _dot(xn, w_ref[:, i * width:(i + 1) * width])

    def put(ref, val):
        if head_major:
            for h in range(n_heads):
                ref[h] = val[:, h * LANES:(h + 1) * LANES].astype(ref.dtype)
        else:
            ref[...] = val.astype(ref.dtype)

    def put_rows(ref, val):
        for h in range(n_heads):
            ref[pl.ds(h, tm, stride=n_heads), :] = val[:, h * LANES:(h + 1) * LANES]

    put(qa_ref, group(0) * (ATT_DQ ** -0.5 * LOG2E))
    ka = group(1)
    put_rows(kaf_ref, ka)
    put(ka_ref, ka)
    va = group(2)
    put_rows(vaf_ref, va)
    put(va_ref, va)
    put(qm_ref, group(3))
    put(km_ref, group(4) * (LANES ** -0.5))
    put(vm_ref, group(5))
    om_ref[...] = group(6)
    gates = _dot(xn, w_ref[:, 7 * width:7 * width + LANES])
    gt_ref[...] = gates
    gates_t = gates.T[:2 * H_ML, :]
    for c in range(tm // gate_chunk):
        gtt_ref[c] = gates_t[:, c * gate_chunk:(c + 1) * gate_chunk]

    if later_weights:
        wg_ref, wu_ref, wd_ref, wo_ref = later_weights
        _gate_up_kernel(wg_ref, wu_ref, wgu_o)
        wd_o[...] = wd_ref[...].astype(BF16)
        wo_o[...] = wo_ref[...].astype(BF16)


def _rows_per_step(total, steps):
    share = 1
    while steps % share or total % (steps // share) or (total // (steps // share)) % (2 * SUBLANES):
        share += 1
    return total // (steps // share), share


def _proj(x, gain, w_in_p, head_major, gate_chunk, later=None):
    n, d = x.shape
    width = H_ATT * LANES
    n_heads = width // LANES
    tm = TOKEN_TILE
    row = lambda w: pl.BlockSpec((tm, w), lambda i: (i, 0))
    if head_major:
        hm_spec = pl.BlockSpec((n_heads, tm, LANES), lambda i: (0, i, 0))
        hm_shape = jax.ShapeDtypeStruct((n_heads, n, LANES), BF16)
    else:
        hm_spec = row(width)
        hm_shape = jax.ShapeDtypeStruct((n, width), F32)
    f32_shape = jax.ShapeDtypeStruct((n, width), F32)
    rows_spec = pl.BlockSpec((tm * n_heads, LANES), lambda i: (i, 0))
    rows_shape = jax.ShapeDtypeStruct((n * n_heads, LANES), F32)
    out_specs = [hm_spec, rows_spec, rows_spec, hm_spec, hm_spec, hm_spec, hm_spec, hm_spec, row(width),
                 row(LANES), pl.BlockSpec((tm // gate_chunk, 2 * H_ML, gate_chunk), lambda i: (i, 0, 0))]
    out_shape = [hm_shape, rows_shape, rows_shape, hm_shape, hm_shape, hm_shape, hm_shape, hm_shape, f32_shape,
                 jax.ShapeDtypeStruct((n, LANES), F32),
                 jax.ShapeDtypeStruct((n // gate_chunk, 2 * H_ML, gate_chunk), F32)]
    in_specs = [row(d), _resident(gain), _resident(w_in_p, 0)]
    args = [x, gain, w_in_p]
    if later is not None:
        wg3, wu3, wd3, group, wo3, layer = later
        steps = n // tm
        f = wg3.shape[-1]
        r_gu, s_gu = _rows_per_step(d, steps)
        r_d, s_d = _rows_per_step(f, steps)
        gu_spec = pl.BlockSpec((None, r_gu, f), lambda i: (group, i // s_gu, 0))
        in_specs += [gu_spec, gu_spec, pl.BlockSpec((None, r_d, d), lambda i: (group, i // s_d, 0)),
                     pl.BlockSpec((None, r_gu, d), lambda i: (layer, i // s_gu, 0))]
        args += [wg3, wu3, wd3, wo3]
        out_specs += [pl.BlockSpec((r_gu, 2 * f), lambda i: (i // s_gu, 0)),
                      pl.BlockSpec((r_d, d), lambda i: (i // s_d, 0)),
                      pl.BlockSpec((r_gu, d), lambda i: (i // s_gu, 0))]
        out_shape += [jax.ShapeDtypeStruct((d, 2 * f), BF16), jax.ShapeDtypeStruct((f, d), BF16),
                      jax.ShapeDtypeStruct((d, d), BF16)]
    return pl.pallas_call(
        functools.partial(_proj_kernel, head_major=head_major, width=width, gate_chunk=gate_chunk,
                          n_later=0 if later is None else 4),
        grid=(n // tm,),
        in_specs=in_specs,
        out_specs=out_specs,
        out_shape=out_shape,
        compiler_params=_params("arbitrary" if later is not None else "parallel"),
        name="proj_prompt" if head_major else "proj_sample",
    )(*args)


def _t5_bias(rel, rb_ref, h):
    n = jnp.maximum(rel, 0)
    max_exact = NUM_BUCKETS // 2
    x = (jnp.log(jnp.maximum(n, max_exact).astype(F32) / max_exact)
         / math.log(MAX_DISTANCE / max_exact) * (NUM_BUCKETS - max_exact))
    out = jnp.full(rel.shape, rb_ref[max_exact, h], F32)
    for i in range(1, NUM_BUCKETS - max_exact):
        out = jnp.where(x >= i, rb_ref[max_exact + i, h], out)
    for i in range(max_exact):
        out = jnp.where(n == i, rb_ref[i, h], out)
    return (out - rb_ref[NUM_BUCKETS - 1, h]) * LOG2E


def _prompt_bias_kernel(rb_ref, o_ref, *, tile):
    r = lax.broadcasted_iota(jnp.int32, (tile, tile), 0)
    c = lax.broadcasted_iota(jnp.int32, (tile, tile), 1)
    for h in range(H_ATT):
        o_ref[h, 0] = jnp.where(c <= r, _t5_bias(r - c, rb_ref, h), -jnp.inf)
        o_ref[h, 1] = _t5_bias(tile + r - c, rb_ref, h)


def _prompt_bias(rel_bias, tile):
    return pl.pallas_call(
        functools.partial(_prompt_bias_kernel, tile=tile),
        in_specs=[pl.BlockSpec(memory_space=pltpu.SMEM)],
        out_shape=jax.ShapeDtypeStruct((H_ATT, 2, tile, tile), F32),
        name="prompt_bias",
    )(rel_bias)


def _sample_bias_kernel(rb_ref, o_ref, *, page, valid):
    qi = lax.broadcasted_iota(jnp.int32, (SUBLANES, 2 * page), 0)
    col = lax.broadcasted_iota(jnp.int32, (SUBLANES, 2 * page), 1)
    kj = col - page
    for h in range(H_ATT):
        past = _t5_bias(page + qi - col, rb_ref, h)
        new = jnp.where((kj <= qi) & (kj < valid), _t5_bias(qi - kj, rb_ref, h), -jnp.inf)
        tile = jnp.where(col < page, past, new)
        for comp in range(2):
            o_ref[(comp * H_ATT + h) * SUBLANES:(comp * H_ATT + h + 1) * SUBLANES, :] = tile


def _sample_bias(rel_bias, page, valid):
    return pl.pallas_call(
        functools.partial(_sample_bias_kernel, page=page, valid=valid),
        in_specs=[pl.BlockSpec(memory_space=pltpu.SMEM)],
        out_shape=jax.ShapeDtypeStruct((2 * H_ATT * SUBLANES, 2 * page), F32),
        name="sample_bias",
    )(rel_bias)


def _lambda(lq_ref, lk_ref, lam_init):
    prod = lq_ref[...] * lk_ref[...]
    return (jnp.exp(jnp.sum(prod[0:1, :], axis=-1, keepdims=True))
            - jnp.exp(jnp.sum(prod[1:2, :], axis=-1, keepdims=True)) + lam_init)


def _attn_kernel(q_ref, k_ref, v_ref, bias_ref, lq_ref, lk_ref, gn_ref, o_ref, vx_scr, *, tile, n_q, lam_init):
    dv = v_ref.shape[-1]
    vx_scr[:, :dv] = v_ref[0]
    ones_lane = lax.broadcasted_iota(jnp.int32, (vx_scr.shape[0], vx_scr.shape[1] - dv), 1)
    vx_scr[:, dv:] = jnp.where(ones_lane == 0, 1.0, 0.0).astype(BF16)

    lane = lax.broadcasted_iota(jnp.int32, (tile, dv), 1)
    zero = jnp.zeros((tile, dv), BF16)
    lam = _lambda(lq_ref, lk_ref, lam_init)
    diag = jnp.concatenate([bias_ref[0, 0]] * 2, axis=0)
    near = jnp.concatenate([bias_ref[0, 1]] * 2, axis=0)

    def logits(n):
        q = q_ref[0, n * tile:(n + 1) * tile, :]
        qq = jnp.concatenate([jnp.where(lane < ATT_DQ, q, zero), jnp.where(lane >= ATT_DQ, q, zero)], axis=0)
        n_keys = (n + 1) * tile
        s = _dot_nt(qq, k_ref[0, :n_keys, :])
        parts = [s[:, n_keys - tile:] + diag]
        if n >= 1:
            parts.insert(0, s[:, n_keys - 2 * tile:n_keys - tile] + near)
        if n >= 2:
            parts.insert(0, s[:, :n_keys - 2 * tile])
        return parts

    def probs(parts):
        m = functools.reduce(jnp.maximum, [jnp.max(x, axis=-1, keepdims=True) for x in parts])
        return jnp.concatenate([jnp.exp2(x - m) for x in parts], axis=1).astype(BF16)

    def finish(n, p):
        vx = vx_scr[:(n + 1) * tile, :]
        acc0 = _dot(p[:tile], vx)
        acc1 = _dot(p[tile:], vx)
        out = acc0[:, :dv] / acc0[:, dv:dv + 1] - lam * (acc1[:, :dv] / acc1[:, dv:dv + 1])
        o_ref[n * tile:(n + 1) * tile, :] = (_rms(out, gn_ref[...]) * (1.0 - lam_init)).astype(o_ref.dtype)

    for g in range(0, n_q // 2, ATT_GROUP_PAIRS):
        group = [n for gg in range(g, min(g + ATT_GROUP_PAIRS, n_q // 2)) for n in (gg, n_q - 1 - gg)]
        p_group = [probs(logits(n)) for n in group]
        for n, p in zip(group, p_group):
            finish(n, p)


def _prompt_attention(q, k, v, bias, lam_q, lam_k, gain, batch, lam_init):
    h, n, dv = q.shape
    seq = n // batch
    tile = ATT_TILE
    nq = seq // tile
    assert nq % 2 == 0
    seq_spec = pl.BlockSpec((1, seq, dv), lambda b, hh: (hh, b, 0))
    return pl.pallas_call(
        functools.partial(_attn_kernel, tile=tile, n_q=nq, lam_init=lam_init),
        grid=(batch, h),
        in_specs=[seq_spec, seq_spec, seq_spec,
                  pl.BlockSpec((1, 2, tile, tile), lambda b, hh: (hh, 0, 0, 0)),
                  pl.BlockSpec(lam_q.shape, lambda b, hh: (0, 0)),
                  pl.BlockSpec(lam_k.shape, lambda b, hh: (0, 0)),
                  pl.BlockSpec(gain.shape, lambda b, hh: (0, 0))],
        out_specs=pl.BlockSpec((seq, dv), lambda b, hh: (b, hh)),
        out_shape=jax.ShapeDtypeStruct((n, h * dv), BF16),
        scratch_shapes=[pltpu.VMEM((seq, 2 * dv), BF16)],
        compiler_params=_params("parallel", "parallel"),
        name="attn_prompt",
    )(q, k, v, bias, lam_q, lam_k, gain)


def _sample_attn_core(q, load_k, load_v, load_kn, load_vn, bias_ref, lam, gn_ref, k_buf, v_buf,
                      *, n_pages, page, n_tok_new, lam_init):
    width = q.shape[-1]
    n_rows = 2 * H_ATT * SUBLANES
    qt = jnp.concatenate([q] * (2 * H_ATT), axis=0)
    r = lax.broadcasted_iota(jnp.int32, (n_rows, width), 0)
    c = lax.broadcasted_iota(jnp.int32, (n_rows, width), 1)
    comp = r // (H_ATT * SUBLANES)
    head = (r // SUBLANES) % H_ATT
    own = (c // LANES == head) & ((c // ATT_DQ) % 2 == comp)
    qbd = jnp.where(own, qt, 0.0).astype(BF16)

    n_past = n_pages * page
    new_rows = 2 * SUBLANES
    for load, load_new, buf in ((load_k, load_kn, k_buf), (load_v, load_vn, v_buf)):
        for p in range(n_pages):
            for h in range(H_ATT):
                buf[p * page:(p + 1) * page, h * LANES:(h + 1) * LANES] = load(p, h).astype(BF16)
        new = jnp.concatenate([load_new(h) for h in range(H_ATT)], axis=1)
        new = jnp.concatenate([new, jnp.zeros((new_rows - n_tok_new, width), F32)], axis=0)
        buf[n_past:n_past + new_rows, :] = new.astype(BF16)
        buf[n_past + new_rows:, :] = jnp.zeros((page - new_rows, width), BF16)

    s = _dot_nt(qbd, k_buf[...])
    parts = [s[:, :n_past - page], s[:, n_past - page:] + bias_ref[...]]
    m = functools.reduce(jnp.maximum, [jnp.max(x, axis=-1, keepdims=True) for x in parts])
    p = jnp.concatenate([jnp.exp2(x - m) for x in parts], axis=1)
    inv = 1.0 / jnp.sum(p, axis=-1, keepdims=True)
    half = n_rows // 2
    a = (p[:half] * inv[:half] - lam * (p[half:] * inv[half:])).astype(BF16)
    out = _dot(a, v_buf[...])
    heads = [_rms(out[h * SUBLANES:(h + 1) * SUBLANES, h * LANES:(h + 1) * LANES], gn_ref[...])
             * (1.0 - lam_init) for h in range(H_ATT)]
    return jnp.concatenate(heads, axis=1)


def _ffn_attn_kernel(pt_ref, x_ref, g_ref, wgu_ref, wd_ref, q_ref, kn_ref, vn_ref, bias_ref, lq_ref, lk_ref,
                     gn_ref, ck_hbm, cv_hbm, o_ref, ao_ref, u_ref, k_ring, v_ring, k_buf, v_buf, sem,
                     *, n_chunks, gain_rows, n_pages, page, first_page, n_seq, lam_init):
    _, pre_row, post_row = gain_rows
    i = pl.program_id(0)
    seqs = q_ref.shape[0]
    page_rows = page * H_ATT
    new_rows = kn_ref.shape[0] // seqs
    n_tok_new = new_rows // H_ATT

    def page_copies(g, slot):
        copies = []
        for p in range(n_pages):
            src = pl.ds(pl.multiple_of((first_page + pt_ref[g, p]) * page_rows, page_rows), page_rows)
            dst = pl.ds(p * page_rows, page_rows)
            copies.append(pltpu.make_async_copy(ck_hbm.at[src, :], k_ring.at[slot, dst, :], sem.at[slot, 0]))
            copies.append(pltpu.make_async_copy(cv_hbm.at[src, :], v_ring.at[slot, dst, :], sem.at[slot, 1]))
        return copies

    n_slots = k_ring.shape[0]

    @pl.when(i == 0)
    def _():
        for a in range(n_slots):
            for n, cp in enumerate(page_copies(min(a, n_seq - 1), a)):
                cp.start(priority=n % 2)

    x = x_ref[...]
    h = _rms(x, g_ref[pre_row:pre_row + 1, :]).astype(BF16)
    lam = _lambda(lq_ref, lk_ref, lam_init)
    chunk_bounds = [(n_chunks * j) // seqs for j in range(seqs + 1)]
    for j in range(seqs):
        g = i * seqs + j
        slot = lax.rem(g, n_slots)

        for c in range(chunk_bounds[j], chunk_bounds[j + 1]):
            gu = _dot(h, wgu_ref[:, 2 * c * FFN_CHUNK:2 * (c + 1) * FFN_CHUNK])
            gate = gu[:, :FFN_CHUNK]
            up = gu[:, FFN_CHUNK:]
            u_ref[:, c * FFN_CHUNK:(c + 1) * FFN_CHUNK] = (gate * jax.nn.sigmoid(gate) * up).astype(BF16)

        for cp in page_copies(g, slot):
            cp.wait()

        def past(ring, p, hh, slot=slot):
            return ring[slot, pl.ds(p * page_rows + hh, page, stride=H_ATT), :]

        def new(ref, hh, j=j):
            return ref[pl.ds(j * new_rows + hh, n_tok_new, stride=H_ATT), :]

        ao_ref[j] = _sample_attn_core(
            q_ref[j], functools.partial(past, k_ring), functools.partial(past, v_ring),
            functools.partial(new, kn_ref), functools.partial(new, vn_ref), bias_ref, lam, gn_ref, k_buf, v_buf,
            n_pages=n_pages, page=page, n_tok_new=n_tok_new, lam_init=lam_init)

        for n, cp in enumerate(page_copies(jnp.minimum(g + n_slots, n_seq - 1), slot)):
            cp.start(priority=n % 2)

    y = _dot(u_ref[...], wd_ref[...])
    o_ref[...] = x + 0.5 * _rms(y, g_ref[post_row:post_row + 1, :])

    @pl.when(i == pl.num_programs(0) - 1)
    def _():
        for a in range(n_slots):
            for cp in page_copies(n_seq - 1, a):
                cp.wait()


def _ffn_with_sample_attention(x, gains, gain_rows, wgu, wd, page_table, q, k_new, v_new, cache_k, cache_v,
                               page, first_page, bias, lam_q, lam_k, gain, lam_init):
    n, d = x.shape
    f = wd.shape[0]
    tm = TOKEN_TILE
    steps = n // tm
    nb, rows, width = q.shape
    n_pages = page_table.shape[1]
    n_heads = width // LANES
    seqs = nb // steps
    assert nb == seqs * steps and nb >= PAGE_RING_SLOTS
    ring = pltpu.VMEM((PAGE_RING_SLOTS, n_pages * page * n_heads, LANES), F32)
    stage = pltpu.VMEM(((n_pages + 1) * page, width), BF16)
    full = lambda a: pl.BlockSpec(a.shape, lambda i, pt: (0,) * a.ndim)
    seq_spec = pl.BlockSpec((seqs, rows, width), lambda i, pt: (i, 0, 0))
    new_spec = pl.BlockSpec((seqs * rows * n_heads, LANES), lambda i, pt: (i, 0))
    row_spec = pl.BlockSpec((tm, d), lambda i, pt: (i, 0))
    grid_spec = pltpu.PrefetchScalarGridSpec(
        num_scalar_prefetch=1,
        grid=(steps,),
        in_specs=[row_spec, _resident(gains), _resident(wgu), _resident(wd),
                  seq_spec, new_spec, new_spec, full(bias), full(lam_q), full(lam_k), full(gain),
                  pl.BlockSpec(memory_space=pl.ANY), pl.BlockSpec(memory_space=pl.ANY)],
        out_specs=[row_spec, seq_spec],
        scratch_shapes=[pltpu.VMEM((tm, f), BF16), ring, ring, stage, stage,
                        pltpu.SemaphoreType.DMA((PAGE_RING_SLOTS, 2))],
    )
    return pl.pallas_call(
        functools.partial(_ffn_attn_kernel, n_chunks=f // FFN_CHUNK, gain_rows=gain_rows, n_pages=n_pages,
                          page=page, first_page=first_page, n_seq=nb, lam_init=lam_init),
        grid_spec=grid_spec,
        out_shape=[jax.ShapeDtypeStruct((n, d), F32), jax.ShapeDtypeStruct((nb, rows, width), F32)],
        compiler_params=pltpu.CompilerParams(dimension_semantics=("arbitrary",),
                                             vmem_limit_bytes=FUSED_VMEM_LIMIT_BYTES),
        name="ffn_attn_sample",
    )(page_table, x, gains, wgu, wd, q, k_new, v_new, bias, lam_q, lam_k, gain, cache_k, cache_v)


def _log_sigmoid(x):
    return jnp.minimum(x, 0.0) - jnp.log1p(jnp.exp(-jnp.abs(x)))


def _mlstm_step_kernel(q_ref, k_ref, v_ref, g_ref, gt_ref, om_ref, brow_ref, bcol_ref, gn_ref,
                       c0_ref, n0_ref, m0_ref, hm_ref, c_ref, n_ref, m_ref, *, rows_per_seq, valid):
    n_seq = c0_ref.shape[0]
    P = rows_per_seq
    R = n_seq * P
    r_i = lax.broadcasted_iota(jnp.int32, (R, R), 0)
    c_i = lax.broadcasted_iota(jnp.int32, (R, R), 1)
    same = (r_i // P) == (c_i // P)
    causal = same & (c_i <= r_i)
    hi = lax.Precision.HIGHEST

    g_col = g_ref[...] + brow_ref[...]
    g_row = gt_ref[0] + bcol_ref[...]
    lf_col = _log_sigmoid(g_col)
    lf_row = _log_sigmoid(g_row)
    tok_c = lax.broadcasted_iota(jnp.int32, g_col.shape, 0) % P
    tok_r = lax.broadcasted_iota(jnp.int32, g_row.shape, 1) % P
    g_col = jnp.where(tok_c < valid, g_col, -1e30)
    g_row = jnp.where(tok_r < valid, g_row, -1e30)
    lf_col = jnp.where(tok_c < valid, lf_col, 0.0)
    lf_row = jnp.where(tok_r < valid, lf_row, 0.0)
    same_f = same.astype(F32)
    b_col = jnp.dot(causal.astype(F32), lf_col, preferred_element_type=F32, precision=hi)
    b_row = jnp.dot(lf_row, (same & (r_i <= c_i)).astype(F32), preferred_element_type=F32, precision=hi)
    bl_col = jnp.dot(same_f, lf_col, preferred_element_type=F32, precision=hi)
    bl_row = jnp.dot(lf_row, same_f, preferred_element_type=F32, precision=hi)

    for h in range(H_ML):
        cols = slice(h * LANES, (h + 1) * LANES)
        q = q_ref[:, cols]
        k = k_ref[:, cols]
        v = v_ref[:, cols]
        bc = b_col[:, H_ML + h:H_ML + h + 1]
        ic = g_col[:, h:h + 1]
        br = b_row[H_ML + h:H_ML + h + 1, :]
        ir = g_row[h:h + 1, :]
        blc = bl_col[:, H_ML + h:H_ML + h + 1]
        blr = bl_row[H_ML + h:H_ML + h + 1, :]
        m = m0_ref[:, h:h + 1]

        log_d = jnp.where(causal, bc - br + ir, -jnp.inf)
        inter = bc + m
        m_t = jnp.maximum(inter, jnp.max(log_d, axis=-1, keepdims=True))
        s = _dot_nt(q.astype(BF16), k.astype(BF16)) * jnp.exp(log_d - m_t)
        w_inter = jnp.exp(inter - m_t)
        q_c = jnp.concatenate([_dot(q[j * P:(j + 1) * P].astype(BF16), c0_ref[j, h].astype(BF16))
                               for j in range(n_seq)], axis=0)
        n_rows = jnp.concatenate([jnp.broadcast_to(n0_ref[j, h:h + 1, :], (P, LANES)) for j in range(n_seq)],
                                 axis=0)
        num = _dot(s.astype(BF16), v.astype(BF16)) + w_inter * q_c
        den = jnp.sum(s, axis=-1, keepdims=True) + w_inter * jnp.sum(q * n_rows, axis=-1, keepdims=True)
        hv = num / jnp.maximum(jnp.abs(den), jnp.exp(-m_t))
        hm_ref[:, cols] = _rms(hv, gn_ref[...]) * jax.nn.sigmoid(om_ref[:, cols])

        w_max = jnp.max(jnp.where(same, blr - br + ir, -jnp.inf), axis=-1, keepdims=True)
        m_new = jnp.maximum(blc + m, w_max)
        ws = jnp.exp(blc - bc + ic - m_new)
        fw = jnp.exp(blc + m - m_new)
        wv = ws * v
        wk = ws * k
        for j in range(n_seq):
            rows = slice(j * P, (j + 1) * P)
            fw_j = fw[j * P:j * P + 1, :]
            c_ref[j, h] = fw_j * c0_ref[j, h] + _dot_tn(k[rows].astype(BF16), wv[rows].astype(BF16))
            n_ref[j, h:h + 1, :] = fw_j * n0_ref[j, h:h + 1, :] + jnp.sum(wk[rows], axis=0, keepdims=True)
            m_ref[j, h:h + 1, :] = jnp.broadcast_to(m_new[j * P:j * P + 1, :], (1, LANES))


def _mlstm_step(q, k, v, gates, gates_t, om, b_row, b_col, gain, state_c, state_n, m_rows, *, rows_per_seq,
                valid, n_seq):
    n, width = om.shape
    batch = state_c.shape[0]
    dh = LANES
    rows = n_seq * rows_per_seq
    row = lambda w: pl.BlockSpec((rows, w), lambda b: (b, 0))
    full = lambda a: pl.BlockSpec(a.shape, lambda b: (0,) * a.ndim)
    c_spec = pl.BlockSpec((n_seq, H_ML, dh, dh), lambda b: (b, 0, 0, 0))
    v_spec = pl.BlockSpec((n_seq, H_ML, dh), lambda b: (b, 0, 0))
    return pl.pallas_call(
        functools.partial(_mlstm_step_kernel, rows_per_seq=rows_per_seq, valid=valid),
        grid=(batch // n_seq,),
        in_specs=[row(width), row(width), row(width), row(LANES),
                  pl.BlockSpec((1, 2 * H_ML, rows), lambda b: (b, 0, 0)),
                  row(width), full(b_row), full(b_col), full(gain), c_spec, v_spec, row(LANES)],
        out_specs=[row(width), c_spec, v_spec, v_spec],
        out_shape=[jax.ShapeDtypeStruct((n, width), F32),
                   jax.ShapeDtypeStruct((batch, H_ML, dh, dh), F32),
                   jax.ShapeDtypeStruct((batch, H_ML, dh), F32),
                   jax.ShapeDtypeStruct((batch, H_ML, dh), F32)],
        compiler_params=_params("parallel"),
        name="mlstm_sample",
    )(q, k, v, gates, gates_t, om, b_row, b_col, gain, state_c, state_n, m_rows)


def _mlstm_seq_kernel(q_ref, k_ref, v_ref, g_ref, gt_ref, om_ref, brow_ref, bcol_ref, gn_ref,
                      hm_ref, c_ref, n_ref, m_ref, fcol_scr, vx_scr, *, tile):
    seq = g_ref.shape[0]
    n_t = seq // tile
    dv = v_ref.shape[-1]
    t_i = lax.broadcasted_iota(jnp.int32, (tile, tile), 0)
    s_i = lax.broadcasted_iota(jnp.int32, (tile, tile), 1)
    causal = s_i <= t_i
    lower = causal.astype(F32)
    upper = (t_i <= s_i).astype(F32)

    off_col = jnp.zeros((1, LANES), F32)
    off_row = jnp.zeros((2 * H_ML, 1), F32)
    u_tiles = []
    for j in range(n_t):
        rows = slice(j * tile, (j + 1) * tile)
        lf_col = _log_sigmoid(g_ref[rows, :] + brow_ref[...])
        f_col = jnp.dot(lower, lf_col, preferred_element_type=F32, precision=lax.Precision.HIGHEST) + off_col
        fcol_scr[rows, :] = f_col
        off_col = f_col[tile - 1:tile, :]
        g_row = gt_ref[j] + bcol_ref[...]
        f_row = jnp.dot(_log_sigmoid(g_row), upper, preferred_element_type=F32,
                        precision=lax.Precision.HIGHEST) + off_row
        off_row = f_row[:, tile - 1:tile]
        u_tiles.append((g_row[:H_ML, :] - f_row[H_ML:, :]) * LOG2E)

    ones_lane = lax.broadcasted_iota(jnp.int32, (seq, vx_scr.shape[2] - dv), 1)
    for h in range(H_ML):
        vx_scr[h, :, :dv] = v_ref[h]
        vx_scr[h, :, dv:] = jnp.where(ones_lane == 0, 1.0, 0.0).astype(BF16)

    for h in range(H_ML):
        u_row = jnp.concatenate([u[h:h + 1, :] for u in u_tiles], axis=1)
        tile_max = [jnp.max(u[h:h + 1, :], axis=-1, keepdims=True) for u in u_tiles]
        before = [None]
        for j in range(n_t):
            before.append(tile_max[j] if before[-1] is None else jnp.maximum(before[-1], tile_max[j]))

        def weights(n):
            n_keys = (n + 1) * tile
            s = _dot_nt(q_ref[h, n * tile:(n + 1) * tile, :], k_ref[h, :n_keys, :])
            u_diag = jnp.where(causal, u_row[:, n_keys - tile:n_keys], -jnp.inf)
            r = jnp.max(u_diag, axis=-1, keepdims=True)
            if before[n] is not None:
                r = jnp.maximum(r, before[n])
            r0 = jnp.maximum(r, 0.0)
            parts = [jnp.exp2(u_diag - r0)]
            if n >= 1:
                parts.insert(0, jnp.exp2(u_row[:, :n_keys - tile] - r0))
            return (s * jnp.concatenate(parts, axis=1)).astype(BF16), r0 * (1.0 / LOG2E)

        def finish(n, p, r0):
            rows = slice(n * tile, (n + 1) * tile)
            acc = _dot(p, vx_scr[h, :(n + 1) * tile, :])
            m_t = fcol_scr[rows, H_ML + h:H_ML + h + 1] + r0
            hv = acc[:, :dv] / jnp.maximum(jnp.abs(acc[:, dv:dv + 1]), jnp.exp(-m_t))
            gate = jax.nn.sigmoid(om_ref[rows, h * LANES:(h + 1) * LANES])
            hm_ref[rows, h * LANES:(h + 1) * LANES] = (_rms(hv, gn_ref[...]) * gate).astype(hm_ref.dtype)

        for g in range(n_t // 2):
            pair = (g, n_t - 1 - g)
            done = [weights(n) for n in pair]
            for n, (p, r0) in zip(pair, done):
                finish(n, p, r0)

        r0_last = jnp.maximum(before[n_t], 0.0) * (1.0 / LOG2E)
        u_col = (g_ref[:, h:h + 1] + brow_ref[:, h:h + 1]) - fcol_scr[:, H_ML + h:H_ML + h + 1]
        w_col = jnp.exp(u_col - r0_last)
        kf = k_ref[h]
        c_ref[0, h] = _dot_tn(kf, (w_col * v_ref[h].astype(F32)).astype(BF16))
        n_ref[0, h:h + 1, :] = jnp.sum(w_col * kf.astype(F32), axis=0, keepdims=True)
        m_last = off_col[:, H_ML + h:H_ML + h + 1] + r0_last
        m_ref[0, h:h + 1, :] = jnp.broadcast_to(m_last, (1, LANES))


def _mlstm_seq(q, k, v, gates, gates_t, om, b_row, b_col, gain, *, batch, tile):
    n, width = om.shape
    seq = n // batch
    n_t = seq // tile
    assert n_t % 2 == 0
    dh = LANES
    qkv_spec = pl.BlockSpec((H_ML, seq, dh), lambda b: (0, b, 0))
    row = lambda w: pl.BlockSpec((seq, w), lambda b: (b, 0))
    full = lambda a: pl.BlockSpec(a.shape, lambda b: (0,) * a.ndim)
    c_spec = pl.BlockSpec((1, H_ML, dh, dh), lambda b: (b, 0, 0, 0))
    v_spec = pl.BlockSpec((1, H_ML, dh), lambda b: (b, 0, 0))
    return pl.pallas_call(
        functools.partial(_mlstm_seq_kernel, tile=tile),
        grid=(batch,),
        in_specs=[qkv_spec, qkv_spec, qkv_spec, row(LANES),
                  pl.BlockSpec((n_t, 2 * H_ML, tile), lambda b: (b, 0, 0)),
                  row(width), full(b_row), full(b_col), full(gain)],
        out_specs=[row(width), c_spec, v_spec, v_spec],
        out_shape=[jax.ShapeDtypeStruct((n, width), BF16),
                   jax.ShapeDtypeStruct((batch, H_ML, dh, dh), F32),
                   jax.ShapeDtypeStruct((batch, H_ML, dh), F32),
                   jax.ShapeDtypeStruct((batch, H_ML, dh), F32)],
        scratch_shapes=[pltpu.VMEM((seq, LANES), F32), pltpu.VMEM((H_ML, seq, 2 * dh), BF16)],
        compiler_params=_params("parallel"),
        name="mlstm_prompt",
    )(q, k, v, gates, gates_t, om, b_row, b_col, gain)


def kernel(x_prompt, x_sample, cache_k, cache_v, state_C, state_n, state_m, page_table, rel_bias, norm_gains,
           ffn_w_gate, ffn_w_up, ffn_w_down, w_in, b_gates, lam_q, lam_k, attn_norm, mlstm_norm, w_out):
    batch, seq, d = x_prompt.shape
    dec_batch, dec_seq, _ = x_sample.shape
    depth = w_in.shape[0]
    page = cache_k.shape[2]
    width = H_ATT * LANES
    d_in = w_in.shape[-1]
    n_gate = 2 * H_ML
    assert d_in == 7 * width + n_gate and dec_seq <= SAMPLE_ROWS
    assert seq % ATT_TILE == 0 and seq % MLSTM_TILE == 0 and (batch * seq) % TOKEN_TILE == 0

    xp = x_prompt.reshape(batch * seq, d)
    xs = jnp.pad(x_sample, ((0, 0), (0, SAMPLE_ROWS - dec_seq), (0, 0))).reshape(dec_batch * SAMPLE_ROWS, d)

    prompt_bias = _prompt_bias(rel_bias, ATT_TILE)
    sample_bias = _sample_bias(rel_bias, page, dec_seq)

    outs = [[] for _ in range(10)]
    for l in range(depth):
        lam_init = 0.8 - 0.6 * math.exp(-0.3 * l)
        g = norm_gains[l]
        f = ffn_w_gate.shape[-1]
        wg3, wu3, wd3 = ffn_w_gate.reshape(-1, d, f), ffn_w_up.reshape(-1, d, f), ffn_w_down.reshape(-1, f, d)
        wgu1 = _fuse_gate_up(wg3, wu3, 2 * l, 1, WEIGHT_ROW_TILE)[0]
        wd1 = _cast_groups(wd3, 2 * l, 1, d, f // 4)[0]
        w_in_p = _cast_transposed(jnp.swapaxes(w_in, 1, 2), l, d_in - n_gate + LANES, WEIGHT_ROW_TILE)
        b_row = jnp.pad(b_gates[l].reshape(1, n_gate), ((0, 0), (0, LANES - n_gate)))
        b_col = b_gates[l].reshape(n_gate, 1)
        gn_att = attn_norm[l].reshape(1, -1)
        gn_ml = mlstm_norm[l].reshape(1, -1)
        ck = cache_k.reshape(-1, LANES)
        cv = cache_v.reshape(-1, LANES)
        first_page = l * cache_k.shape[1]

        xs = _ffn(xs, g, (None, 0, 1), wgu1, wd1)
        sample_seqs = math.gcd(dec_batch, MLSTM_SAMPLE_SEQS)
        (qa_s, kaf_s, vaf_s, _, _, qm_s, km_s, vm_s, om_s, gt_s, gtt_s) = _proj(xs, g[2:3], w_in_p, False,
                                                                                sample_seqs * SAMPLE_ROWS)
        seq3 = lambda a: a.reshape(dec_batch, SAMPLE_ROWS, width)
        xp, mix_a_s = _ffn_with_sample_attention(xp, g, (None, 0, 1), wgu1, wd1, page_table, seq3(qa_s), kaf_s,
                                                 vaf_s, ck, cv, page, first_page, sample_bias, lam_q[l], lam_k[l],
                                                 gn_att, lam_init)

        (qa, kaf, vaf, ka, va, qm, km, vm, om, gt, gtt, wgu2, wd2, wo) = _proj(
            xp, g[2:3], w_in_p, True, MLSTM_TILE, later=(wg3, wu3, wd3, 2 * l + 1, w_out, l))
        mix_a = _prompt_attention(qa, ka, va, prompt_bias, lam_q[l], lam_k[l], gn_att, batch, lam_init)
        mix_m, c_p, n_p, m_p = _mlstm_seq(qm, km, vm, gt, gtt, om, b_row, b_col, gn_ml, batch=batch, tile=MLSTM_TILE)

        m_rows = jnp.pad(jnp.repeat(state_m[l], SAMPLE_ROWS, axis=0), ((0, 0), (0, LANES - H_ML)))
        mix_m_s, c_s, n_s, m_s = _mlstm_step(qm_s, km_s, vm_s, gt_s, gtt_s, om_s, b_row, b_col, gn_ml,
                                             state_C[l], state_n[l], m_rows, rows_per_seq=SAMPLE_ROWS,
                                             valid=dec_seq, n_seq=sample_seqs)

        xp, xs = _ffn_two(xp, (mix_a, mix_m), xs, (mix_a_s.reshape(-1, width), mix_m_s), wo, g, (3, 4, 5),
                          wgu2, wd2)

        new = lambda a: a.reshape(dec_batch, SAMPLE_ROWS, H_ATT, LANES)[:, :dec_seq]
        for lst, val in zip(outs, (kaf.reshape(batch, seq, H_ATT, LANES), vaf.reshape(batch, seq, H_ATT, LANES),
                                   c_p, n_p, m_p[:, :, 0], new(kaf_s), new(vaf_s), c_s, n_s, m_s[:, :, 0])):
            lst.append(val)

    y_prompt = xp.reshape(batch, seq, d)
    y_sample = xs.reshape(dec_batch, SAMPLE_ROWS, d)[:, :dec_seq]
    return (y_prompt, y_sample) + tuple(jnp.stack(o) for o in outs)
```
